```python
import jax, jax.numpy as jnp
from jax import lax
import numpy as np

D_MODEL = 1024
BATCH = 8
SEQ = 2048
DEPTH = 1
DEC_BATCH = 128
DEC_SEQ = 1
PAST_LEN = 16384
PAGE_SIZE = 128

N_META = 16
POOL_WINDOWS = (2, 4, 8, 16)
N_POOL_GROUPS = len(POOL_WINDOWS)
POOL_WIDTH = D_MODEL // 2
POOL_GC = POOL_WIDTH // N_POOL_GROUPS
POOL_OUT_GC = D_MODEL // N_POOL_GROUPS
POOL_BUF = max(POOL_WINDOWS) - 1
CONV_WIDTH = D_MODEL // 2
CONV_W = 3
D_FF = 2816
EPS = 1e-6
IN_SIZES = (POOL_WIDTH, CONV_WIDTH, CONV_WIDTH, CONV_WIDTH, D_MODEL, D_MODEL)
IN_SPLITS = tuple(int(s) for s in np.cumsum(IN_SIZES)[:-1])
IN_TOTAL = int(sum(IN_SIZES))

kernel_name = "pool_shortconv_gated_hybrid_step"


def rmsnorm(x, g):
    xf = x.astype(jnp.float32)
    y = xf * lax.rsqrt(jnp.mean(xf * xf, axis=-1, keepdims=True) + EPS)
    return (y * g.astype(jnp.float32)).astype(x.dtype)


def causal_dwconv(x, buf, w):
    T = x.shape[1]
    xp = jnp.concatenate([buf.astype(x.dtype), x], axis=1)
    y = xp[:, 0:T] * w[0]
    for k in range(1, CONV_W):
        y = y + xp[:, k:k + T] * w[k]
    return y, xp[:, T:]


def multiscale_pool(u, buf, pos0, w_pool, scale):
    B, T, _ = u.shape
    up = jnp.concatenate([buf.astype(u.dtype), u], axis=1)
    cs = jnp.cumsum(up.astype(jnp.float32), axis=1)
    cs = jnp.pad(cs, ((0, 0), (1, 0), (0, 0)))
    hi = cs[:, POOL_BUF + 1:POOL_BUF + 1 + T]
    pos = (pos0 + jnp.arange(T)).astype(jnp.int32)
    uf = u.astype(jnp.float32)
    outs = []
    for g, w in enumerate(POOL_WINDOWS):
        sl = slice(g * POOL_GC, (g + 1) * POOL_GC)
        lo = cs[:, POOL_BUF + 1 - w:POOL_BUF + 1 - w + T, sl]
        cnt = jnp.minimum(w, pos + 1).astype(jnp.float32)[None, :, None]
        outs.append((hi[..., sl] - lo) / cnt - uf[..., sl])
    pooled = jnp.stack(outs, axis=2).astype(u.dtype)
    mixed = jnp.einsum('btgc,gco->btgo', pooled, w_pool).reshape(B, T, D_MODEL)
    return mixed * scale, up[:, T:]


def trunk_layer(x, pos0, buf_pool, buf_conv, buf_ffn, w_in, w_pool, pool_scale, w_conv,
                w_conv_out, w_o, w_up, w_ffn_conv, w_down, g_pre_mix, g_post_mix,
                g_pre_ffn, g_post_ffn):
    h = rmsnorm(x, g_pre_mix)
    z = h @ w_in
    u, v, bg, cg, ga, gb = jnp.split(z, IN_SPLITS, axis=-1)
    a, new_pool = multiscale_pool(u, buf_pool, pos0, w_pool, pool_scale)
    cv, new_conv = causal_dwconv(cg * v, buf_conv, w_conv)
    b = (bg * cv) @ w_conv_out
    merged = jax.nn.sigmoid(ga) * a + jax.nn.sigmoid(gb) * b
    x = x + rmsnorm(merged @ w_o, g_post_mix)
    h = rmsnorm(x, g_pre_ffn)
    upc, new_ffn = causal_dwconv(h @ w_up, buf_ffn, w_ffn_conv)
    gate, val = jnp.split(upc, 2, axis=-1)
    y = (jax.nn.silu(gate) * val) @ w_down
    x = x + rmsnorm(y, g_post_ffn)
    return x, new_pool, new_conv, new_ffn


def setup_inputs(seed: int = 0) -> dict:
    key = jax.random.key(seed)
    ks = jax.random.split(key, 20)
    f32 = jnp.float32
    nrm = lambda k, shape, s: jax.random.normal(k, shape, f32) * s
    return {
        "x_prompt": nrm(ks[0], (BATCH, SEQ, D_MODEL), 1.0),
        "x_sample": nrm(ks[1], (DEC_BATCH, DEC_SEQ, D_MODEL), 1.0),
        "state_pool": nrm(ks[2], (DEPTH, DEC_BATCH, POOL_BUF, POOL_WIDTH), 1.0),
        "state_conv": nrm(ks[3], (DEPTH, DEC_BATCH, CONV_W - 1, CONV_WIDTH), 1.0),
        "state_ffn": nrm(ks[4], (DEPTH, DEC_BATCH, CONV_W - 1, 2 * D_FF), 1.0),
        "meta_tokens": nrm(ks[5], (N_META, D_MODEL), 1.0),
        "w_in": nrm(ks[6], (DEPTH, D_MODEL, IN_TOTAL), D_MODEL ** -0.5),
        "w_pool": nrm(ks[7], (DEPTH, N_POOL_GROUPS, POOL_GC, POOL_OUT_GC), POOL_GC ** -0.5),
        "pool_scale": 1.0 + nrm(ks[8], (DEPTH, D_MODEL), 0.05),
        "w_conv": nrm(ks[9], (DEPTH, CONV_W, CONV_WIDTH), CONV_W ** -0.5),
        "w_conv_out": nrm(ks[10], (DEPTH, CONV_WIDTH, D_MODEL), CONV_WIDTH ** -0.5),
        "w_o": nrm(ks[11], (DEPTH, D_MODEL, D_MODEL), D_MODEL ** -0.5),
        "w_up": nrm(ks[12], (DEPTH, D_MODEL, 2 * D_FF), D_MODEL ** -0.5),
        "w_ffn_conv": nrm(ks[13], (DEPTH, CONV_W, 2 * D_FF), CONV_W ** -0.5),
        "w_down": nrm(ks[14], (DEPTH, D_FF, D_MODEL), D_FF ** -0.5),
        "g_pre_mix": 1.0 + nrm(ks[15], (DEPTH, D_MODEL), 0.05),
        "g_post_mix": 1.0 + nrm(ks[16], (DEPTH, D_MODEL), 0.05),
        "g_pre_ffn": 1.0 + nrm(ks[17], (DEPTH, D_MODEL), 0.05),
        "g_post_ffn": 1.0 + nrm(ks[18], (DEPTH, D_MODEL), 0.05),
    }


def reference(x_prompt, x_sample, state_pool, state_conv, state_ffn, meta_tokens, w_in, w_pool,
              pool_scale, w_conv, w_conv_out, w_o, w_up, w_ffn_conv, w_down, g_pre_mix,
              g_post_mix, g_pre_ffn, g_post_ffn):
    meta = jnp.broadcast_to(meta_tokens.astype(x_prompt.dtype)[None], (BATCH, N_META, D_MODEL))
    xp = jnp.concatenate([meta, x_prompt], axis=1)
    xs = x_sample
    zp_pool = jnp.zeros((BATCH, POOL_BUF, POOL_WIDTH), xp.dtype)
    zp_conv = jnp.zeros((BATCH, CONV_W - 1, CONV_WIDTH), xp.dtype)
    zp_ffn = jnp.zeros((BATCH, CONV_W - 1, 2 * D_FF), xp.dtype)
    pp, pc, pf, sp, sc, sf = [], [], [], [], [], []
    for l in range(DEPTH):
        params = (w_in[l], w_pool[l], pool_scale[l], w_conv[l], w_conv_out[l], w_o[l], w_up[l],
                  w_ffn_conv[l], w_down[l], g_pre_mix[l], g_post_mix[l], g_pre_ffn[l],
                  g_post_ffn[l])
        xp, np_, nc_, nf_ = trunk_layer(xp, 0, zp_pool, zp_conv, zp_ffn, *params)
        pp.append(np_); pc.append(nc_); pf.append(nf_)
        xs, ns_p, ns_c, ns_f = trunk_layer(xs, PAST_LEN, state_pool[l], state_conv[l],
                                           state_ffn[l], *params)
        sp.append(ns_p); sc.append(ns_c); sf.append(ns_f)
    y_prompt = xp[:, N_META:]
    y_sample = xs
    return (y_prompt, y_sample, jnp.stack(pp), jnp.stack(pc), jnp.stack(pf),
            jnp.stack(sp), jnp.stack(sc), jnp.stack(sf))
```

```python
import functools

import jax
import jax.numpy as jnp
from jax import lax
from jax.experimental import pallas as pl
from jax.experimental.pallas import tpu as pltpu

D_MODEL = 1024
N_META = 16
POOL_WINDOWS = (2, 4, 8, 16)
POOL_WIDTH = D_MODEL // 2
POOL_GC = POOL_WIDTH // len(POOL_WINDOWS)
POOL_OUT_GC = D_MODEL // len(POOL_WINDOWS)
POOL_BUF = max(POOL_WINDOWS) - 1
CONV_WIDTH = D_MODEL // 2
CONV_W = 3
D_FF = 2816
EPS = 1e-6
OFF_U, OFF_V, OFF_B, OFF_C, OFF_GA, OFF_GB = 0, 512, 1024, 1536, 2048, 3072

FF_CHUNK = 256
POOL_HALO = 16
CONV_HALO = 8
TM_PROMPT = 256
VMEM_LIMIT_BYTES = 56 * 1024 * 1024

_BF16 = jnp.bfloat16
_F32 = jnp.float32


def _dot(a, b):
    return jnp.dot(a, b, preferred_element_type=_F32)


def _rms(x, g):
    ms = jnp.mean(x * x, axis=-1, keepdims=True)
    return x * lax.rsqrt(ms + EPS) * g


def _sigmoid(x):
    return 1.0 / (1.0 + jnp.exp(-x))


def _layer(x, w, hist, mbuf, actbuf, pool_div):
    (w_in, w_pool, pscale, wconv, wco, wo, wup, wfc, wdown, g1, g2, g3, g4) = w
    h = _rms(x, g1[...]).astype(_BF16)

    hist.put_u(_dot(h, w_in[:, OFF_U:OFF_U + POOL_WIDTH]))
    v = _dot(h, w_in[:, OFF_V:OFF_V + CONV_WIDTH])
    cg = _dot(h, w_in[:, OFF_C:OFF_C + CONV_WIDTH])
    hist.put_c(cg * v)
    cv = wconv[CONV_W - 1:CONV_W, :] * hist.get_c(0)
    for k in range(1, CONV_W):
        cv = cv + wconv[CONV_W - 1 - k:CONV_W - k, :] * hist.get_c(k)
    bg = _dot(h, w_in[:, OFF_B:OFF_B + CONV_WIDTH])
    b_in = (bg * cv).astype(_BF16)

    for g, win in enumerate(POOL_WINDOWS):
        cs = slice(g * POOL_GC, (g + 1) * POOL_GC)
        os_ = slice(g * POOL_OUT_GC, (g + 1) * POOL_OUT_GC)
        ug = hist.get_u(0, cs)
        s = ug
        for k in range(1, win):
            s = s + hist.get_u(k, cs)
        pooled = (pool_div(s, win) - ug).astype(_BF16)
        a = _dot(pooled, w_pool[g]) * pscale[:, os_]
        ga = _dot(h, w_in[:, OFF_GA + g * POOL_OUT_GC:OFF_GA + (g + 1) * POOL_OUT_GC])
        gb = _dot(h, w_in[:, OFF_GB + g * POOL_OUT_GC:OFF_GB + (g + 1) * POOL_OUT_GC])
        b = _dot(b_in, wco[:, os_])
        mbuf[:, os_] = (_sigmoid(ga) * a + _sigmoid(gb) * b).astype(_BF16)

    x1 = x + _rms(_dot(mbuf[...], wo[...]), g2[...])
    h2 = _rms(x1, g3[...]).astype(_BF16)

    for c in range(D_FF // FF_CHUNK):
        conv = []
        for col in (c * FF_CHUNK, D_FF + c * FF_CHUNK):
            cols = slice(col, col + FF_CHUNK)
            hist.put_up(_dot(h2, wup[:, cols]), cols)
            y = wfc[CONV_W - 1:CONV_W, cols] * hist.get_up(0, cols)
            for k in range(1, CONV_W):
                y = y + wfc[CONV_W - 1 - k:CONV_W - k, cols] * hist.get_up(k, cols)
            conv.append(y)
        gate, val = conv
        actbuf[:, c * FF_CHUNK:(c + 1) * FF_CHUNK] = (gate * _sigmoid(gate) * val).astype(_BF16)

    y = _dot(actbuf[...], wdown[...])
    return x1 + _rms(y, g4[...])


class _ScratchHistory:
    def __init__(self, ubuf, cbuf, upbuf, tm):
        self.ubuf, self.cbuf, self.upbuf, self.tm = ubuf, cbuf, upbuf, tm

    def put_u(self, val):
        self.ubuf[POOL_HALO:POOL_HALO + self.tm, :] = val

    def get_u(self, k, cs):
        return self.ubuf[POOL_HALO - k:POOL_HALO - k + self.tm, cs]

    def put_c(self, val):
        self.cbuf[CONV_HALO:CONV_HALO + self.tm, :] = val

    def get_c(self, k):
        return self.cbuf[CONV_HALO - k:CONV_HALO - k + self.tm, :]

    def put_up(self, val, cols):
        self.upbuf[CONV_HALO:CONV_HALO + self.tm, cols] = val

    def get_up(self, k, cols):
        return self.upbuf[CONV_HALO - k:CONV_HALO - k + self.tm, cols]


def _seq_kernel(x_ref, ip_ref, ic_ref, if_ref, *rest, tm, first_pos_is_zero):
    w = rest[:13]
    y_ref, op_ref, oc_ref, of_ref, ubuf, cbuf, upbuf, mbuf, actbuf = rest[13:]
    i = pl.program_id(1)

    @pl.when(i == 0)
    def _():
        ubuf[0:POOL_HALO, :] = ip_ref[...]
        cbuf[0:CONV_HALO, :] = ic_ref[...]
        upbuf[0:CONV_HALO, :] = if_ref[...]

    if first_pos_is_zero:
        row1 = lax.broadcasted_iota(jnp.int32, (tm, POOL_GC), 0) + 1

        def pool_div(s, win):
            return s / jnp.minimum(win, row1).astype(_F32)
    else:
        def pool_div(s, win):
            return s * (1.0 / win)

    hist = _ScratchHistory(ubuf, cbuf, upbuf, tm)
    y_ref[...] = _layer(x_ref[...], w, hist, mbuf, actbuf, pool_div)

    new_u = ubuf[tm:tm + POOL_HALO, :]
    new_c = cbuf[tm:tm + CONV_HALO, :]
    new_f = upbuf[tm:tm + CONV_HALO, :]
    ubuf[0:POOL_HALO, :] = new_u
    cbuf[0:CONV_HALO, :] = new_c
    upbuf[0:CONV_HALO, :] = new_f

    @pl.when(i == pl.num_programs(1) - 1)
    def _():
        op_ref[...] = new_u
        oc_ref[...] = new_c
        of_ref[...] = new_f


class _StateHistory:
    def __init__(self, sp_ref, sc_ref, sf_ref, u_ref, c_ref, f_ref):
        self.sp, self.sc, self.sf = sp_ref, sc_ref, sf_ref
        self.u, self.c, self.f = u_ref, c_ref, f_ref

    def put_u(self, val):
        self.u[...] = val

    def get_u(self, k, cs):
        return self.u[:, cs] if k == 0 else self.sp[POOL_BUF - k, :, cs]

    def put_c(self, val):
        self.c[...] = val

    def get_c(self, k):
        return self.c[...] if k == 0 else self.sc[CONV_W - 1 - k]

    def put_up(self, val, cols):
        self.f[:, cols] = val

    def get_up(self, k, cols):
        return self.f[:, cols] if k == 0 else self.sf[CONV_W - 1 - k, :, cols]


def _sample_kernel(x_ref, sp_ref, sc_ref, sf_ref, *rest):
    w = rest[:13]
    y_ref, u_ref, c_ref, f_ref, mbuf, actbuf = rest[13:]
    hist = _StateHistory(sp_ref, sc_ref, sf_ref, u_ref, c_ref, f_ref)
    y_ref[...] = _layer(x_ref[...], w, hist, mbuf, actbuf, lambda s, win: s * (1.0 / win))


def _resident(shape):
    zeros = (0,) * len(shape)
    return pl.BlockSpec(shape, lambda *_: zeros, pipeline_mode=pl.Buffered(1))


def _weight_specs(weights):
    return [_resident(a.shape) for a in weights]


def _seq_call(x, init_pool, init_conv, init_ffn, weights, tm, first_pos_is_zero, name):
    nb, t, d = x.shape
    assert t % tm == 0 and tm >= POOL_HALO
    grid = (nb, t // tm)
    state_spec = lambda rows, c: pl.BlockSpec((None, rows, c), lambda b, i: (b, 0, 0))
    return pl.pallas_call(
        functools.partial(_seq_kernel, tm=tm, first_pos_is_zero=first_pos_is_zero),
        grid=grid,
        in_specs=[pl.BlockSpec((None, tm, d), lambda b, i: (b, i, 0)),
                  _resident(init_pool.shape), _resident(init_conv.shape), _resident(init_ffn.shape)]
                 + _weight_specs(weights),
        out_specs=[pl.BlockSpec((None, tm, d), lambda b, i: (b, i, 0)),
                   state_spec(POOL_HALO, POOL_WIDTH), state_spec(CONV_HALO, CONV_WIDTH),
                   state_spec(CONV_HALO, 2 * D_FF)],
        out_shape=[jax.ShapeDtypeStruct((nb, t, d), _F32),
                   jax.ShapeDtypeStruct((nb, POOL_HALO, POOL_WIDTH), _F32),
                   jax.ShapeDtypeStruct((nb, CONV_HALO, CONV_WIDTH), _F32),
                   jax.ShapeDtypeStruct((nb, CONV_HALO, 2 * D_FF), _F32)],
        scratch_shapes=[pltpu.VMEM((POOL_HALO + tm, POOL_WIDTH), _F32),
                        pltpu.VMEM((CONV_HALO + tm, CONV_WIDTH), _F32),
                        pltpu.VMEM((CONV_HALO + tm, 2 * D_FF), _F32),
                        pltpu.VMEM((tm, d), _BF16),
                        pltpu.VMEM((tm, D_FF), _BF16)],
        compiler_params=pltpu.CompilerParams(
            dimension_semantics=("arbitrary", "arbitrary"), vmem_limit_bytes=VMEM_LIMIT_BYTES),
        name=name,
    )(x, init_pool, init_conv, init_ffn, *weights)


def _sample_call(x, sp, sc, sf, weights):
    n, d = x.shape
    return pl.pallas_call(
        _sample_kernel,
        grid=(1,),
        in_specs=[_resident(x.shape), _resident(sp.shape), _resident(sc.shape), _resident(sf.shape)]
                 + _weight_specs(weights),
        out_specs=[_resident((n, d)), _resident((n, POOL_WIDTH)), _resident((n, CONV_WIDTH)),
                   _resident((n, 2 * D_FF))],
        out_shape=[jax.ShapeDtypeStruct((n, d), _F32),
                   jax.ShapeDtypeStruct((n, POOL_WIDTH), _F32),
                   jax.ShapeDtypeStruct((n, CONV_WIDTH), _F32),
                   jax.ShapeDtypeStruct((n, 2 * D_FF), _F32)],
        scratch_shapes=[pltpu.VMEM((n, d), _BF16), pltpu.VMEM((n, D_FF), _BF16)],
        compiler_params=pltpu.CompilerParams(
            dimension_semantics=("arbitrary",), vmem_limit_bytes=VMEM_LIMIT_BYTES),
        name="sample_step",
    )(x, sp, sc, sf, *weights)


def kernel(x_prompt, x_sample, state_pool, state_conv, state_ffn, meta_tokens, w_in, w_pool,
           pool_scale, w_conv, w_conv_out, w_o, w_up, w_ffn_conv, w_down, g_pre_mix,
           g_post_mix, g_pre_ffn, g_post_ffn):
    depth = w_in.shape[0]
    assert depth == 1, "the prompt / sample streams are chained for a single layer only"
    assert x_sample.shape[1] == 1
    assert N_META >= max(POOL_WINDOWS)
    row = lambda a: a.reshape(1, -1)
    l = 0
    weights = (w_in[l].astype(_BF16), w_pool[l].astype(_BF16), row(pool_scale[l]), w_conv[l],
               w_conv_out[l].astype(_BF16), w_o[l].astype(_BF16), w_up[l].astype(_BF16),
               w_ffn_conv[l], w_down[l].astype(_BF16), row(g_pre_mix[l]), row(g_post_mix[l]),
               row(g_pre_ffn[l]), row(g_post_ffn[l]))

    zeros = lambda r, c: jnp.zeros((r, c), _F32)
    _, meta_pool, meta_conv, meta_ffn = _seq_call(
        meta_tokens.astype(_F32)[None], zeros(POOL_HALO, POOL_WIDTH), zeros(CONV_HALO, CONV_WIDTH),
        zeros(CONV_HALO, 2 * D_FF), weights, N_META, True, "meta_prefix")

    y_prompt, pp, pc, pf = _seq_call(x_prompt, meta_pool[0], meta_conv[0], meta_ffn[0], weights,
                                     TM_PROMPT, False, "prompt_step")

    to_rows_first = lambda s: jnp.transpose(s, (1, 0, 2))
    ys, us, cs, fs = _sample_call(x_sample[:, 0, :], to_rows_first(state_pool[l]),
                                  to_rows_first(state_conv[l]), to_rows_first(state_ffn[l]), weights)
    push = lambda state, new: jnp.concatenate([state[:, 1:], new[:, None, :]], axis=1)

    return (y_prompt, ys[:, None, :],
            pp[None, :, POOL_HALO - POOL_BUF:], pc[None, :, CONV_HALO - (CONV_W - 1):],
            pf[None, :, CONV_HALO - (CONV_W - 1):],
            push(state_pool[l], us)[None], push(state_conv[l], cs)[None],
            push(state_ffn[l], fs)[None])
```

```python
import functools

import jax
import jax.numpy as jnp
from jax import lax
from jax.experimental import pallas as pl
from jax.experimental.pallas import tpu as pltpu

D_MODEL = 1024
N_META = 16
POOL_WINDOWS = (2, 4, 8, 16)
POOL_WIDTH = D_MODEL // 2
POOL_GC = POOL_WIDTH // len(POOL_WINDOWS)
POOL_OUT_GC = D_MODEL // len(POOL_WINDOWS)
POOL_BUF = max(POOL_WINDOWS) - 1
CONV_WIDTH = D_MODEL // 2
CONV_W = 3
D_FF = 2816
EPS = 1e-6
OFF_U, OFF_V, OFF_B, OFF_C, OFF_GA, OFF_GB = 0, 512, 1024, 1536, 2048, 3072

SUBLANES = 8
FF_CHUNK = 256
POOL_HALO = 16
CONV_HALO = 8
TM_PROMPT = 256
VMEM_LIMIT_BYTES = 56 * 1024 * 1024

_BF16 = jnp.bfloat16
_F32 = jnp.float32


def _dot(a, b):
    return jnp.dot(a, b, preferred_element_type=_F32)


def _rms(x, g):
    ms = jnp.mean(x * x, axis=-1, keepdims=True)
    return x * lax.rsqrt(ms + EPS) * g


def _sigmoid(x):
    return 1.0 / (1.0 + jnp.exp(-x))


def _layer(x, w, hist, mbuf, actbuf, pool_div):
    (w_in, w_pool, pscale, wconv, wco, wo, wup, wfc, wdown, g1, g2, g3, g4) = w
    h = _rms(x, g1[...]).astype(_BF16)

    hist.put_u(_dot(h, w_in[:, OFF_U:OFF_U + POOL_WIDTH]))
    v = _dot(h, w_in[:, OFF_V:OFF_V + CONV_WIDTH])
    cg = _dot(h, w_in[:, OFF_C:OFF_C + CONV_WIDTH])
    hist.put_c(cg * v)
    cv = wconv[CONV_W - 1:CONV_W, :] * hist.get_c(0)
    for k in range(1, CONV_W):
        cv = cv + wconv[CONV_W - 1 - k:CONV_W - k, :] * hist.get_c(k)
    bg = _dot(h, w_in[:, OFF_B:OFF_B + CONV_WIDTH])
    b_in = (bg * cv).astype(_BF16)

    for g, win in enumerate(POOL_WINDOWS):
        cs = slice(g * POOL_GC, (g + 1) * POOL_GC)
        os_ = slice(g * POOL_OUT_GC, (g + 1) * POOL_OUT_GC)
        ug = hist.get_u(0, cs)
        s = ug
        for k in range(1, win):
            s = s + hist.get_u(k, cs)
        pooled = (pool_div(s, win) - ug).astype(_BF16)
        a = _dot(pooled, w_pool[g]) * pscale[:, os_]
        ga = _dot(h, w_in[:, OFF_GA + g * POOL_OUT_GC:OFF_GA + (g + 1) * POOL_OUT_GC])
        gb = _dot(h, w_in[:, OFF_GB + g * POOL_OUT_GC:OFF_GB + (g + 1) * POOL_OUT_GC])
        b = _dot(b_in, wco[:, os_])
        mbuf[:, os_] = (_sigmoid(ga) * a + _sigmoid(gb) * b).astype(_BF16)

    x1 = x + _rms(_dot(mbuf[...], wo[...]), g2[...])
    h2 = _rms(x1, g3[...]).astype(_BF16)

    for c in range(D_FF // FF_CHUNK):
        conv = []
        for col in (c * FF_CHUNK, D_FF + c * FF_CHUNK):
            cols = slice(col, col + FF_CHUNK)
            hist.put_up(_dot(h2, wup[:, cols]), cols)
            y = wfc[CONV_W - 1:CONV_W, cols] * hist.get_up(0, cols)
            for k in range(1, CONV_W):
                y = y + wfc[CONV_W - 1 - k:CONV_W - k, cols] * hist.get_up(k, cols)
            conv.append(y)
        gate, val = conv
        actbuf[:, c * FF_CHUNK:(c + 1) * FF_CHUNK] = (gate * _sigmoid(gate) * val).astype(_BF16)

    y = _dot(actbuf[...], wdown[...])
    return x1 + _rms(y, g4[...])


class _ScratchHistory:
    def __init__(self, ubuf, cbuf, upbuf, tm):
        self.ubuf, self.cbuf, self.upbuf, self.tm = ubuf, cbuf, upbuf, tm

    def put_u(self, val):
        self.ubuf[POOL_HALO:POOL_HALO + self.tm, :] = val

    def get_u(self, k, cs):
        return self.ubuf[POOL_HALO - k:POOL_HALO - k + self.tm, cs]

    def put_c(self, val):
        self.cbuf[CONV_HALO:CONV_HALO + self.tm, :] = val

    def get_c(self, k):
        return self.cbuf[CONV_HALO - k:CONV_HALO - k + self.tm, :]

    def put_up(self, val, cols):
        self.upbuf[CONV_HALO:CONV_HALO + self.tm, cols] = val

    def get_up(self, k, cols):
        return self.upbuf[CONV_HALO - k:CONV_HALO - k + self.tm, cols]


def _seq_kernel(x_ref, ip_ref, ic_ref, if_ref, *rest, tm, first_pos_is_zero):
    w = rest[:13]
    y_ref, op_ref, oc_ref, of_ref, ubuf, cbuf, upbuf, mbuf, actbuf = rest[13:]
    i = pl.program_id(1)

    @pl.when(i == 0)
    def _():
        ubuf[0:POOL_HALO, :] = ip_ref[...]
        cbuf[0:CONV_HALO, :] = ic_ref[...]
        upbuf[0:CONV_HALO, :] = if_ref[...]

    if first_pos_is_zero:
        row1 = lax.broadcasted_iota(jnp.int32, (tm, POOL_GC), 0) + 1

        def pool_div(s, win):
            return s / jnp.minimum(win, row1).astype(_F32)
    else:
        def pool_div(s, win):
            return s * (1.0 / win)

    hist = _ScratchHistory(ubuf, cbuf, upbuf, tm)
    y_ref[...] = _layer(x_ref[...], w, hist, mbuf, actbuf, pool_div)

    new_u = ubuf[tm:tm + POOL_HALO, :]
    new_c = cbuf[tm:tm + CONV_HALO, :]
    new_f = upbuf[tm:tm + CONV_HALO, :]
    ubuf[0:POOL_HALO, :] = new_u
    cbuf[0:CONV_HALO, :] = new_c
    upbuf[0:CONV_HALO, :] = new_f

    @pl.when(i == pl.num_programs(1) - 1)
    def _():
        op_ref[...] = new_u
        oc_ref[...] = new_c
        of_ref[...] = new_f


class _SegmentHistory:
    def __init__(self, ubuf, cbuf, upbuf, utail, ctail, ftail, tm):
        self.ubuf, self.cbuf, self.upbuf, self.tm = ubuf, cbuf, upbuf, tm
        self.utail, self.ctail, self.ftail = utail, ctail, ftail

    def _put(self, buf, tail, val, cols, depth):
        tm = self.tm
        buf[depth * SUBLANES:depth * SUBLANES + tm, cols] = val
        sublane = lax.broadcasted_iota(jnp.int32, (SUBLANES, val.shape[1]), 0)
        for j in range(depth):
            rows = slice(j * SUBLANES, (j + 1) * SUBLANES)
            cur = val[tm - (depth - j) * SUBLANES:tm - (depth - j - 1) * SUBLANES, :]
            mixed = jnp.where(sublane == SUBLANES - 1, tail[rows, cols], cur)
            buf[rows, cols] = pltpu.roll(mixed, 1, axis=0)
            tail[rows, cols] = cur

    def _get(self, buf, k, cols, depth):
        start = (depth - k) * SUBLANES
        return buf[start:start + self.tm, cols]

    def put_u(self, val):
        self._put(self.ubuf, self.utail, val, slice(None), POOL_BUF)

    def get_u(self, k, cs):
        return self._get(self.ubuf, k, cs, POOL_BUF)

    def put_c(self, val):
        self._put(self.cbuf, self.ctail, val, slice(None), CONV_W - 1)

    def get_c(self, k):
        return self._get(self.cbuf, k, slice(None), CONV_W - 1)

    def put_up(self, val, cols):
        self._put(self.upbuf, self.ftail, val, cols, CONV_W - 1)

    def get_up(self, k, cols):
        return self._get(self.upbuf, k, cols, CONV_W - 1)


def _prompt_kernel(x_hbm, ip_ref, ic_ref, if_ref, *rest, tm):
    w = rest[:13]
    (y_hbm, op_ref, oc_ref, of_ref, xbuf, ybuf, in_sem, out_sem,
     ubuf, cbuf, upbuf, utail, ctail, ftail, mbuf, actbuf) = rest[13:]
    seg = tm // SUBLANES
    nt = pl.num_programs(1)
    total = pl.num_programs(0) * nt
    i = pl.program_id(1)
    n = pl.program_id(0) * nt + i
    slot = n % 2

    def x_copies(step, sl):
        b, t0 = step // nt, (step % nt) * tm
        return [pltpu.make_async_copy(x_hbm.at[b, pl.ds(t0 + s * seg, seg), :],
                                      xbuf.at[sl, :, s, :], in_sem.at[sl]) for s in range(SUBLANES)]

    def y_copies(step, sl):
        b, t0 = step // nt, (step % nt) * tm
        return [pltpu.make_async_copy(ybuf.at[sl, :, s, :],
                                      y_hbm.at[b, pl.ds(t0 + s * seg, seg), :], out_sem.at[sl])
                for s in range(SUBLANES)]

    @pl.when(n == 0)
    def _():
        for cp in x_copies(n, slot):
            cp.start()

    @pl.when(n + 1 < total)
    def _():
        for cp in x_copies(n + 1, 1 - slot):
            cp.start()

    @pl.when(i == 0)
    def _():
        for j in range(POOL_BUF):
            r = POOL_HALO - POOL_BUF + j
            utail[j * SUBLANES:(j + 1) * SUBLANES, :] = jnp.broadcast_to(
                ip_ref[r:r + 1, :], (SUBLANES, POOL_WIDTH))
        for j in range(CONV_W - 1):
            r = CONV_HALO - (CONV_W - 1) + j
            ctail[j * SUBLANES:(j + 1) * SUBLANES, :] = jnp.broadcast_to(
                ic_ref[r:r + 1, :], (SUBLANES, CONV_WIDTH))
            ftail[j * SUBLANES:(j + 1) * SUBLANES, :] = jnp.broadcast_to(
                if_ref[r:r + 1, :], (SUBLANES, 2 * D_FF))

    for cp in x_copies(n, slot):
        cp.wait()

    @pl.when(n >= 2)
    def _():
        for cp in y_copies(n - 2, slot):
            cp.wait()

    hist = _SegmentHistory(ubuf, cbuf, upbuf, utail, ctail, ftail, tm)
    x = xbuf[slot].reshape(tm, D_MODEL)
    y = _layer(x, w, hist, mbuf, actbuf, lambda s, win: s * (1.0 / win))
    ybuf[slot] = y.reshape(seg, SUBLANES, D_MODEL)

    for cp in y_copies(n, slot):
        cp.start()

    @pl.when(i == nt - 1)
    def _():
        last = SUBLANES - 1
        for j in range(POOL_BUF):
            op_ref[j:j + 1, :] = utail[j * SUBLANES + last:(j + 1) * SUBLANES, :]
        for j in range(CONV_W - 1):
            oc_ref[j:j + 1, :] = ctail[j * SUBLANES + last:(j + 1) * SUBLANES, :]
            of_ref[j:j + 1, :] = ftail[j * SUBLANES + last:(j + 1) * SUBLANES, :]

    @pl.when(n == total - 1)
    def _():
        @pl.when(n >= 1)
        def _():
            for cp in y_copies(n - 1, 1 - slot):
                cp.wait()
        for cp in y_copies(n, slot):
            cp.wait()


class _StateHistory:
    def __init__(self, sp_ref, sc_ref, sf_ref, u_ref, c_ref, f_ref):
        self.sp, self.sc, self.sf = sp_ref, sc_ref, sf_ref
        self.u, self.c, self.f = u_ref, c_ref, f_ref

    def put_u(self, val):
        self.u[...] = val

    def get_u(self, k, cs):
        return self.u[:, cs] if k == 0 else self.sp[POOL_BUF - k, :, cs]

    def put_c(self, val):
        self.c[...] = val

    def get_c(self, k):
        return self.c[...] if k == 0 else self.sc[CONV_W - 1 - k]

    def put_up(self, val, cols):
        self.f[:, cols] = val

    def get_up(self, k, cols):
        return self.f[:, cols] if k == 0 else self.sf[CONV_W - 1 - k, :, cols]


def _sample_kernel(x_ref, sp_ref, sc_ref, sf_ref, *rest):
    w = rest[:13]
    y_ref, u_ref, c_ref, f_ref, mbuf, actbuf = rest[13:]
    hist = _StateHistory(sp_ref, sc_ref, sf_ref, u_ref, c_ref, f_ref)
    y_ref[...] = _layer(x_ref[...], w, hist, mbuf, actbuf, lambda s, win: s * (1.0 / win))


def _resident(shape):
    zeros = (0,) * len(shape)
    return pl.BlockSpec(shape, lambda *_: zeros, pipeline_mode=pl.Buffered(1))


def _weight_specs(weights):
    return [_resident(a.shape) for a in weights]


def _seq_call(x, init_pool, init_conv, init_ffn, weights, tm, first_pos_is_zero, name):
    nb, t, d = x.shape
    assert t % tm == 0 and tm >= POOL_HALO
    grid = (nb, t // tm)
    state_spec = lambda rows, c: pl.BlockSpec((None, rows, c), lambda b, i: (b, 0, 0))
    return pl.pallas_call(
        functools.partial(_seq_kernel, tm=tm, first_pos_is_zero=first_pos_is_zero),
        grid=grid,
        in_specs=[pl.BlockSpec((None, tm, d), lambda b, i: (b, i, 0)),
                  _resident(init_pool.shape), _resident(init_conv.shape), _resident(init_ffn.shape)]
                 + _weight_specs(weights),
        out_specs=[pl.BlockSpec((None, tm, d), lambda b, i: (b, i, 0)),
                   state_spec(POOL_HALO, POOL_WIDTH), state_spec(CONV_HALO, CONV_WIDTH),
                   state_spec(CONV_HALO, 2 * D_FF)],
        out_shape=[jax.ShapeDtypeStruct((nb, t, d), _F32),
                   jax.ShapeDtypeStruct((nb, POOL_HALO, POOL_WIDTH), _F32),
                   jax.ShapeDtypeStruct((nb, CONV_HALO, CONV_WIDTH), _F32),
                   jax.ShapeDtypeStruct((nb, CONV_HALO, 2 * D_FF), _F32)],
        scratch_shapes=[pltpu.VMEM((POOL_HALO + tm, POOL_WIDTH), _F32),
                        pltpu.VMEM((CONV_HALO + tm, CONV_WIDTH), _F32),
                        pltpu.VMEM((CONV_HALO + tm, 2 * D_FF), _F32),
                        pltpu.VMEM((tm, d), _BF16),
                        pltpu.VMEM((tm, D_FF), _BF16)],
        compiler_params=pltpu.CompilerParams(
            dimension_semantics=("arbitrary", "arbitrary"), vmem_limit_bytes=VMEM_LIMIT_BYTES),
        name=name,
    )(x, init_pool, init_conv, init_ffn, *weights)


def _prompt_call(x, init_pool, init_conv, init_ffn, weights, tm):
    nb, t, d = x.shape
    seg = tm // SUBLANES
    assert t % tm == 0 and tm % SUBLANES == 0 and seg >= POOL_BUF
    state_spec = lambda rows, c: pl.BlockSpec((None, rows, c), lambda b, i: (b, 0, 0))
    hbm = pl.BlockSpec(memory_space=pl.ANY)
    rows_with_head = lambda depth: (depth + seg) * SUBLANES
    return pl.pallas_call(
        functools.partial(_prompt_kernel, tm=tm),
        grid=(nb, t // tm),
        in_specs=[hbm, _resident(init_pool.shape), _resident(init_conv.shape),
                  _resident(init_ffn.shape)] + _weight_specs(weights),
        out_specs=[hbm, state_spec(POOL_BUF, POOL_WIDTH), state_spec(CONV_W - 1, CONV_WIDTH),
                   state_spec(CONV_W - 1, 2 * D_FF)],
        out_shape=[jax.ShapeDtypeStruct((nb, t, d), _F32),
                   jax.ShapeDtypeStruct((nb, POOL_BUF, POOL_WIDTH), _F32),
                   jax.ShapeDtypeStruct((nb, CONV_W - 1, CONV_WIDTH), _F32),
                   jax.ShapeDtypeStruct((nb, CONV_W - 1, 2 * D_FF), _F32)],
        scratch_shapes=[pltpu.VMEM((2, seg, SUBLANES, d), _F32),
                        pltpu.VMEM((2, seg, SUBLANES, d), _F32),
                        pltpu.SemaphoreType.DMA((2,)),
                        pltpu.SemaphoreType.DMA((2,)),
                        pltpu.VMEM((rows_with_head(POOL_BUF), POOL_WIDTH), _F32),
                        pltpu.VMEM((rows_with_head(CONV_W - 1), CONV_WIDTH), _F32),
                        pltpu.VMEM((rows_with_head(CONV_W - 1), 2 * D_FF), _F32),
                        pltpu.VMEM((POOL_BUF * SUBLANES, POOL_WIDTH), _F32),
                        pltpu.VMEM(((CONV_W - 1) * SUBLANES, CONV_WIDTH), _F32),
                        pltpu.VMEM(((CONV_W - 1) * SUBLANES, 2 * D_FF), _F32),
                        pltpu.VMEM((tm, d), _BF16),
                        pltpu.VMEM((tm, D_FF), _BF16)],
        compiler_params=pltpu.CompilerParams(
            dimension_semantics=("arbitrary", "arbitrary"), vmem_limit_bytes=VMEM_LIMIT_BYTES),
        name="prompt_step",
    )(x, init_pool, init_conv, init_ffn, *weights)


def _sample_call(x, sp, sc, sf, weights):
    n, d = x.shape
    return pl.pallas_call(
        _sample_kernel,
        grid=(1,),
        in_specs=[_resident(x.shape), _resident(sp.shape), _resident(sc.shape), _resident(sf.shape)]
                 + _weight_specs(weights),
        out_specs=[_resident((n, d)), _resident((n, POOL_WIDTH)), _resident((n, CONV_WIDTH)),
                   _resident((n, 2 * D_FF))],
        out_shape=[jax.ShapeDtypeStruct((n, d), _F32),
                   jax.ShapeDtypeStruct((n, POOL_WIDTH), _F32),
                   jax.ShapeDtypeStruct((n, CONV_WIDTH), _F32),
                   jax.ShapeDtypeStruct((n, 2 * D_FF), _F32)],
        scratch_shapes=[pltpu.VMEM((n, d), _BF16), pltpu.VMEM((n, D_FF), _BF16)],
        compiler_params=pltpu.CompilerParams(
            dimension_semantics=("arbitrary",), vmem_limit_bytes=VMEM_LIMIT_BYTES),
        name="sample_step",
    )(x, sp, sc, sf, *weights)


def kernel(x_prompt, x_sample, state_pool, state_conv, state_ffn, meta_tokens, w_in, w_pool,
           pool_scale, w_conv, w_conv_out, w_o, w_up, w_ffn_conv, w_down, g_pre_mix,
           g_post_mix, g_pre_ffn, g_post_ffn):
    depth = w_in.shape[0]
    assert depth == 1, "the prompt / sample streams are chained for a single layer only"
    assert x_sample.shape[1] == 1
    assert N_META >= max(POOL_WINDOWS)
    row = lambda a: a.reshape(1, -1)
    l = 0
    weights = (w_in[l].astype(_BF16), w_pool[l].astype(_BF16), row(pool_scale[l]), w_conv[l],
               w_conv_out[l].astype(_BF16), w_o[l].astype(_BF16), w_up[l].astype(_BF16),
               w_ffn_conv[l], w_down[l].astype(_BF16), row(g_pre_mix[l]), row(g_post_mix[l]),
               row(g_pre_ffn[l]), row(g_post_ffn[l]))

    zeros = lambda r, c: jnp.zeros((r, c), _F32)
    _, meta_pool, meta_conv, meta_ffn = _seq_call(
        meta_tokens.astype(_F32)[None], zeros(POOL_HALO, POOL_WIDTH), zeros(CONV_HALO, CONV_WIDTH),
        zeros(CONV_HALO, 2 * D_FF), weights, N_META, True, "meta_prefix")

    y_prompt, pp, pc, pf = _prompt_call(x_prompt, meta_pool[0], meta_conv[0], meta_ffn[0], weights,
                                        TM_PROMPT)

    to_rows_first = lambda s: jnp.transpose(s, (1, 0, 2))
    ys, us, cs, fs = _sample_call(x_sample[:, 0, :], to_rows_first(state_pool[l]),
                                  to_rows_first(state_conv[l]), to_rows_first(state_ffn[l]), weights)
    push = lambda state, new: jnp.concatenate([state[:, 1:], new[:, None, :]], axis=1)

    return (y_prompt, ys[:, None, :], pp[None], pc[None], pf[None],
            push(state_pool[l], us)[None], push(state_conv[l], cs)[None],
            push(state_ffn[l], fs)[None])
```

```python
import functools

import jax
import jax.numpy as jnp
from jax import lax
from jax.experimental import pallas as pl
from jax.experimental.pallas import tpu as pltpu

D_MODEL = 1024
N_META = 16
POOL_WINDOWS = (2, 4, 8, 16)
POOL_WIDTH = D_MODEL // 2
POOL_GC = POOL_WIDTH // len(POOL_WINDOWS)
POOL_OUT_GC = D_MODEL // len(POOL_WINDOWS)
POOL_BUF = max(POOL_WINDOWS) - 1
CONV_WIDTH = D_MODEL // 2
CONV_W = 3
D_FF = 2816
EPS = 1e-6
OFF_U, OFF_V, OFF_B, OFF_C, OFF_GA, OFF_GB = 0, 512, 1024, 1536, 2048, 3072

SUBLANES = 8
FF_CHUNK = 256
POOL_HALO = 16
CONV_HALO = 8
TM_PROMPT = 256
VMEM_LIMIT_BYTES = 56 * 1024 * 1024

_BF16 = jnp.bfloat16
_F32 = jnp.float32


def _dot(a, b):
    return jnp.dot(a, b, preferred_element_type=_F32)


def _rms(x, g):
    ms = jnp.mean(x * x, axis=-1, keepdims=True)
    return x * lax.rsqrt(ms + EPS) * g


def _sigmoid(x):
    return 1.0 / (1.0 + jnp.exp(-x))


def _mixer(x, w, hist, mbuf, pool_div):
    (w_in, w_pool, pscale, wconv, wco, wo, _, _, _, g1, g2, _, _) = w
    h = _rms(x, g1[...]).astype(_BF16)

    hist.put_u(_dot(h, w_in[:, OFF_U:OFF_U + POOL_WIDTH]))
    v = _dot(h, w_in[:, OFF_V:OFF_V + CONV_WIDTH])
    cg = _dot(h, w_in[:, OFF_C:OFF_C + CONV_WIDTH])
    hist.put_c(cg * v)
    cv = wconv[CONV_W - 1:CONV_W, :] * hist.get_c(0)
    for k in range(1, CONV_W):
        cv = cv + wconv[CONV_W - 1 - k:CONV_W - k, :] * hist.get_c(k)
    bg = _dot(h, w_in[:, OFF_B:OFF_B + CONV_WIDTH])
    b_in = (bg * cv).astype(_BF16)

    for g, win in enumerate(POOL_WINDOWS):
        cs = slice(g * POOL_GC, (g + 1) * POOL_GC)
        os_ = slice(g * POOL_OUT_GC, (g + 1) * POOL_OUT_GC)
        ug = hist.get_u(0, cs)
        s = ug
        for k in range(1, win):
            s = s + hist.get_u(k, cs)
        pooled = (pool_div(s, win) - ug).astype(_BF16)
        a = _dot(pooled, w_pool[g]) * pscale[:, os_]
        ga = _dot(h, w_in[:, OFF_GA + g * POOL_OUT_GC:OFF_GA + (g + 1) * POOL_OUT_GC])
        gb = _dot(h, w_in[:, OFF_GB + g * POOL_OUT_GC:OFF_GB + (g + 1) * POOL_OUT_GC])
        b = _dot(b_in, wco[:, os_])
        mbuf[:, os_] = (_sigmoid(ga) * a + _sigmoid(gb) * b).astype(_BF16)

    return x + _rms(_dot(mbuf[...], wo[...]), g2[...])


def _ffn_norm(x1, w):
    return _rms(x1, w[11][...]).astype(_BF16)


def _ffn_up(h2, w, hist, actbuf):
    wup, wfc = w[6], w[7]
    for c in range(D_FF // FF_CHUNK):
        conv = []
        for col in (c * FF_CHUNK, D_FF + c * FF_CHUNK):
            cols = slice(col, col + FF_CHUNK)
            hist.put_up(_dot(h2, wup[:, cols]), cols)
            y = wfc[CONV_W - 1:CONV_W, cols] * hist.get_up(0, cols)
            for k in range(1, CONV_W):
                y = y + wfc[CONV_W - 1 - k:CONV_W - k, cols] * hist.get_up(k, cols)
            conv.append(y)
        gate, val = conv
        actbuf[:, c * FF_CHUNK:(c + 1) * FF_CHUNK] = (gate * _sigmoid(gate) * val).astype(_BF16)


def _ffn_down(x1, w, actbuf):
    return x1 + _rms(_dot(actbuf[...], w[8][...]), w[12][...])


def _layer(x, w, hist, mbuf, actbuf, pool_div):
    x1 = _mixer(x, w, hist, mbuf, pool_div)
    _ffn_up(_ffn_norm(x1, w), w, hist, actbuf)
    return _ffn_down(x1, w, actbuf)


def _full_windows(s, win):
    return s * (1.0 / win)


class _ScratchHistory:
    def __init__(self, ubuf, cbuf, upbuf, tm):
        self.ubuf, self.cbuf, self.upbuf, self.tm = ubuf, cbuf, upbuf, tm

    def put_u(self, val):
        self.ubuf[POOL_HALO:POOL_HALO + self.tm, :] = val

    def get_u(self, k, cs):
        return self.ubuf[POOL_HALO - k:POOL_HALO - k + self.tm, cs]

    def put_c(self, val):
        self.cbuf[CONV_HALO:CONV_HALO + self.tm, :] = val

    def get_c(self, k):
        return self.cbuf[CONV_HALO - k:CONV_HALO - k + self.tm, :]

    def put_up(self, val, cols):
        self.upbuf[CONV_HALO:CONV_HALO + self.tm, cols] = val

    def get_up(self, k, cols):
        return self.upbuf[CONV_HALO - k:CONV_HALO - k + self.tm, cols]


def _seq_kernel(x_ref, ip_ref, ic_ref, if_ref, *rest, tm, first_pos_is_zero):
    w = rest[:13]
    y_ref, op_ref, oc_ref, of_ref, ubuf, cbuf, upbuf, mbuf, actbuf = rest[13:]
    i = pl.program_id(1)

    @pl.when(i == 0)
    def _():
        ubuf[0:POOL_HALO, :] = ip_ref[...]
        cbuf[0:CONV_HALO, :] = ic_ref[...]
        upbuf[0:CONV_HALO, :] = if_ref[...]

    if first_pos_is_zero:
        row1 = lax.broadcasted_iota(jnp.int32, (tm, POOL_GC), 0) + 1

        def pool_div(s, win):
            return s / jnp.minimum(win, row1).astype(_F32)
    else:
        pool_div = _full_windows

    hist = _ScratchHistory(ubuf, cbuf, upbuf, tm)
    y_ref[...] = _layer(x_ref[...], w, hist, mbuf, actbuf, pool_div)

    new_u = ubuf[tm:tm + POOL_HALO, :]
    new_c = cbuf[tm:tm + CONV_HALO, :]
    new_f = upbuf[tm:tm + CONV_HALO, :]
    ubuf[0:POOL_HALO, :] = new_u
    cbuf[0:CONV_HALO, :] = new_c
    upbuf[0:CONV_HALO, :] = new_f

    @pl.when(i == pl.num_programs(1) - 1)
    def _():
        op_ref[...] = new_u
        oc_ref[...] = new_c
        of_ref[...] = new_f


class _SegmentHistory:
    def __init__(self, ubuf, cbuf, upbuf, utail, ctail, ftail, tm):
        self.ubuf, self.cbuf, self.upbuf, self.tm = ubuf, cbuf, upbuf, tm
        self.utail, self.ctail, self.ftail = utail, ctail, ftail

    def _put(self, buf, tail, val, cols, depth):
        tm = self.tm
        buf[depth * SUBLANES:depth * SUBLANES + tm, cols] = val
        sublane = lax.broadcasted_iota(jnp.int32, (SUBLANES, val.shape[1]), 0)
        for j in range(depth):
            rows = slice(j * SUBLANES, (j + 1) * SUBLANES)
            cur = val[tm - (depth - j) * SUBLANES:tm - (depth - j - 1) * SUBLANES, :]
            mixed = jnp.where(sublane == SUBLANES - 1, tail[rows, cols], cur)
            buf[rows, cols] = pltpu.roll(mixed, 1, axis=0)
            tail[rows, cols] = cur

    def _get(self, buf, k, cols, depth):
        start = (depth - k) * SUBLANES
        return buf[start:start + self.tm, cols]

    def put_u(self, val):
        self._put(self.ubuf, self.utail, val, slice(None), POOL_BUF)

    def get_u(self, k, cs):
        return self._get(self.ubuf, k, cs, POOL_BUF)

    def put_c(self, val):
        self._put(self.cbuf, self.ctail, val, slice(None), CONV_W - 1)

    def get_c(self, k):
        return self._get(self.cbuf, k, slice(None), CONV_W - 1)

    def put_up(self, val, cols):
        self._put(self.upbuf, self.ftail, val, cols, CONV_W - 1)

    def get_up(self, k, cols):
        return self._get(self.upbuf, k, cols, CONV_W - 1)


def _fill_tail(tail, state_ref, first_row, depth):
    for j in range(depth):
        r = first_row + j
        tail[j * SUBLANES:(j + 1) * SUBLANES, :] = jnp.broadcast_to(
            state_ref[r:r + 1, :], (SUBLANES, tail.shape[1]))


def _read_tail(out_ref, tail, depth):
    for j in range(depth):
        out_ref[j:j + 1, :] = tail[(j + 1) * SUBLANES - 1:(j + 1) * SUBLANES, :]


def _prompt_kernel(x_hbm, ip_ref, ic_ref, if_ref, *rest, tm, nt, total):
    w = rest[:13]
    (y_hbm, op_ref, oc_ref, of_ref, xbuf, ybuf, in_sem, out_sem, ubuf, cbuf, upbuf,
     utail, ctail, ftail, mbuf, actbuf, h2buf, x1_new, x1_old) = rest[13:]
    seg = tm // SUBLANES
    n = pl.program_id(0)
    slot = n % 2

    def x_copies(tile, sl):
        b, t0 = tile // nt, (tile % nt) * tm
        return [pltpu.make_async_copy(x_hbm.at[b, pl.ds(t0 + s * seg, seg), :],
                                      xbuf.at[sl, :, s, :], in_sem.at[sl]) for s in range(SUBLANES)]

    def y_copies(tile, sl):
        b, t0 = tile // nt, (tile % nt) * tm
        return [pltpu.make_async_copy(ybuf.at[sl, :, s, :],
                                      y_hbm.at[b, pl.ds(t0 + s * seg, seg), :], out_sem.at[sl])
                for s in range(SUBLANES)]

    @pl.when(n == 0)
    def _():
        for cp in x_copies(n, slot):
            cp.start()
        h2buf[...] = jnp.zeros_like(h2buf)
        x1_new[...] = jnp.zeros_like(x1_new)
        ftail[...] = jnp.zeros_like(ftail)

    @pl.when(n + 1 < total)
    def _():
        for cp in x_copies(n + 1, 1 - slot):
            cp.start()

    @pl.when(n < total)
    def _():
        for cp in x_copies(n, slot):
            cp.wait()

    @pl.when(n >= 3)
    def _():
        for cp in y_copies(n - 3, 1 - slot):
            cp.wait()

    @pl.when(n % nt == 0)
    def _():
        _fill_tail(utail, ip_ref, POOL_HALO - POOL_BUF, POOL_BUF)
        _fill_tail(ctail, ic_ref, CONV_HALO - (CONV_W - 1), CONV_W - 1)

    @pl.when(n % nt == 1)
    def _():
        _fill_tail(ftail, if_ref, CONV_HALO - (CONV_W - 1), CONV_W - 1)

    hist = _SegmentHistory(ubuf, cbuf, upbuf, utail, ctail, ftail, tm)
    x1_old[...] = x1_new[...]
    _ffn_up(h2buf[...], w, hist, actbuf)
    x1 = _mixer(xbuf[slot].reshape(tm, D_MODEL), w, hist, mbuf, _full_windows)
    x1_new[...] = x1
    h2buf[...] = _ffn_norm(x1, w)
    y = _ffn_down(x1_old[...], w, actbuf)
    ybuf[1 - slot] = y.reshape(seg, SUBLANES, D_MODEL)

    @pl.when(n >= 1)
    def _():
        for cp in y_copies(n - 1, 1 - slot):
            cp.start()

    @pl.when((n % nt == nt - 1) & (n < total))
    def _():
        _read_tail(op_ref, utail, POOL_BUF)
        _read_tail(oc_ref, ctail, CONV_W - 1)

    @pl.when((n % nt == 0) & (n >= 1))
    def _():
        _read_tail(of_ref, ftail, CONV_W - 1)

    @pl.when(n == total)
    def _():
        if total >= 2:
            for cp in y_copies(n - 2, slot):
                cp.wait()
        for cp in y_copies(n - 1, 1 - slot):
            cp.wait()


class _StateHistory:
    def __init__(self, sp_ref, sc_ref, sf_ref, u_ref, c_ref, f_ref):
        self.sp, self.sc, self.sf = sp_ref, sc_ref, sf_ref
        self.u, self.c, self.f = u_ref, c_ref, f_ref

    def put_u(self, val):
        self.u[...] = val

    def get_u(self, k, cs):
        return self.u[:, cs] if k == 0 else self.sp[POOL_BUF - k, :, cs]

    def put_c(self, val):
        self.c[...] = val

    def get_c(self, k):
        return self.c[...] if k == 0 else self.sc[CONV_W - 1 - k]

    def put_up(self, val, cols):
        self.f[:, cols] = val

    def get_up(self, k, cols):
        return self.f[:, cols] if k == 0 else self.sf[CONV_W - 1 - k, :, cols]


def _sample_kernel(x_ref, sp_ref, sc_ref, sf_ref, *rest):
    w = rest[:13]
    y_ref, u_ref, c_ref, f_ref, mbuf, actbuf = rest[13:]
    hist = _StateHistory(sp_ref, sc_ref, sf_ref, u_ref, c_ref, f_ref)
    y_ref[...] = _layer(x_ref[...], w, hist, mbuf, actbuf, _full_windows)


def _resident(shape):
    zeros = (0,) * len(shape)
    return pl.BlockSpec(shape, lambda *_: zeros, pipeline_mode=pl.Buffered(1))


def _weight_specs(weights):
    return [_resident(a.shape) for a in weights]


def _seq_call(x, init_pool, init_conv, init_ffn, weights, tm, first_pos_is_zero, name):
    nb, t, d = x.shape
    assert t % tm == 0 and tm >= POOL_HALO
    grid = (nb, t // tm)
    state_spec = lambda rows, c: pl.BlockSpec((None, rows, c), lambda b, i: (b, 0, 0))
    return pl.pallas_call(
        functools.partial(_seq_kernel, tm=tm, first_pos_is_zero=first_pos_is_zero),
        grid=grid,
        in_specs=[pl.BlockSpec((None, tm, d), lambda b, i: (b, i, 0)),
                  _resident(init_pool.shape), _resident(init_conv.shape), _resident(init_ffn.shape)]
                 + _weight_specs(weights),
        out_specs=[pl.BlockSpec((None, tm, d), lambda b, i: (b, i, 0)),
                   state_spec(POOL_HALO, POOL_WIDTH), state_spec(CONV_HALO, CONV_WIDTH),
                   state_spec(CONV_HALO, 2 * D_FF)],
        out_shape=[jax.ShapeDtypeStruct((nb, t, d), _F32),
                   jax.ShapeDtypeStruct((nb, POOL_HALO, POOL_WIDTH), _F32),
                   jax.ShapeDtypeStruct((nb, CONV_HALO, CONV_WIDTH), _F32),
                   jax.ShapeDtypeStruct((nb, CONV_HALO, 2 * D_FF), _F32)],
        scratch_shapes=[pltpu.VMEM((POOL_HALO + tm, POOL_WIDTH), _F32),
                        pltpu.VMEM((CONV_HALO + tm, CONV_WIDTH), _F32),
                        pltpu.VMEM((CONV_HALO + tm, 2 * D_FF), _F32),
                        pltpu.VMEM((tm, d), _BF16),
                        pltpu.VMEM((tm, D_FF), _BF16)],
        compiler_params=pltpu.CompilerParams(
            dimension_semantics=("arbitrary", "arbitrary"), vmem_limit_bytes=VMEM_LIMIT_BYTES),
        name=name,
    )(x, init_pool, init_conv, init_ffn, *weights)


def _prompt_call(x, init_pool, init_conv, init_ffn, weights, tm):
    nb, t, d = x.shape
    seg = tm // SUBLANES
    assert t % tm == 0 and tm % SUBLANES == 0 and seg >= POOL_BUF
    nt = t // tm
    assert nt >= 2
    total = nb * nt
    mix_state = lambda rows, c: pl.BlockSpec(
        (None, rows, c), lambda n: (jnp.minimum(n, total - 1) // nt, 0, 0))
    ffn_state = lambda rows, c: pl.BlockSpec(
        (None, rows, c), lambda n: (jnp.maximum(n - 1, 0) // nt, 0, 0))
    hbm = pl.BlockSpec(memory_space=pl.ANY)
    rows_with_head = lambda depth: (depth + seg) * SUBLANES
    return pl.pallas_call(
        functools.partial(_prompt_kernel, tm=tm, nt=nt, total=total),
        grid=(total + 1,),
        in_specs=[hbm, _resident(init_pool.shape), _resident(init_conv.shape),
                  _resident(init_ffn.shape)] + _weight_specs(weights),
        out_specs=[hbm, mix_state(POOL_BUF, POOL_WIDTH), mix_state(CONV_W - 1, CONV_WIDTH),
                   ffn_state(CONV_W - 1, 2 * D_FF)],
        out_shape=[jax.ShapeDtypeStruct((nb, t, d), _F32),
                   jax.ShapeDtypeStruct((nb, POOL_BUF, POOL_WIDTH), _F32),
                   jax.ShapeDtypeStruct((nb, CONV_W - 1, CONV_WIDTH), _F32),
                   jax.ShapeDtypeStruct((nb, CONV_W - 1, 2 * D_FF), _F32)],
        scratch_shapes=[pltpu.VMEM((2, seg, SUBLANES, d), _F32),
                        pltpu.VMEM((2, seg, SUBLANES, d), _F32),
                        pltpu.SemaphoreType.DMA((2,)),
                        pltpu.SemaphoreType.DMA((2,)),
                        pltpu.VMEM((rows_with_head(POOL_BUF), POOL_WIDTH), _F32),
                        pltpu.VMEM((rows_with_head(CONV_W - 1), CONV_WIDTH), _F32),
                        pltpu.VMEM((rows_with_head(CONV_W - 1), 2 * D_FF), _F32),
                        pltpu.VMEM((POOL_BUF * SUBLANES, POOL_WIDTH), _F32),
                        pltpu.VMEM(((CONV_W - 1) * SUBLANES, CONV_WIDTH), _F32),
                        pltpu.VMEM(((CONV_W - 1) * SUBLANES, 2 * D_FF), _F32),
                        pltpu.VMEM((tm, d), _BF16),
                        pltpu.VMEM((tm, D_FF), _BF16),
                        pltpu.VMEM((tm, d), _BF16),
                        pltpu.VMEM((tm, d), _F32),
                        pltpu.VMEM((tm, d), _F32)],
        compiler_params=pltpu.CompilerParams(
            dimension_semantics=("arbitrary",), vmem_limit_bytes=VMEM_LIMIT_BYTES),
        name="prompt_step",
    )(x, init_pool, init_conv, init_ffn, *weights)


def _sample_call(x, sp, sc, sf, weights):
    n, d = x.shape
    return pl.pallas_call(
        _sample_kernel,
        grid=(1,),
        in_specs=[_resident(x.shape), _resident(sp.shape), _resident(sc.shape), _resident(sf.shape)]
                 + _weight_specs(weights),
        out_specs=[_resident((n, d)), _resident((n, POOL_WIDTH)), _resident((n, CONV_WIDTH)),
                   _resident((n, 2 * D_FF))],
        out_shape=[jax.ShapeDtypeStruct((n, d), _F32),
                   jax.ShapeDtypeStruct((n, POOL_WIDTH), _F32),
                   jax.ShapeDtypeStruct((n, CONV_WIDTH), _F32),
                   jax.ShapeDtypeStruct((n, 2 * D_FF), _F32)],
        scratch_shapes=[pltpu.VMEM((n, d), _BF16), pltpu.VMEM((n, D_FF), _BF16)],
        compiler_params=pltpu.CompilerParams(
            dimension_semantics=("arbitrary",), vmem_limit_bytes=VMEM_LIMIT_BYTES),
        name="sample_step",
    )(x, sp, sc, sf, *weights)


def kernel(x_prompt, x_sample, state_pool, state_conv, state_ffn, meta_tokens, w_in, w_pool,
           pool_scale, w_conv, w_conv_out, w_o, w_up, w_ffn_conv, w_down, g_pre_mix,
           g_post_mix, g_pre_ffn, g_post_ffn):
    depth = w_in.shape[0]
    assert depth == 1, "the prompt / sample streams are chained for a single layer only"
    assert x_sample.shape[1] == 1
    assert N_META >= max(POOL_WINDOWS)
    row = lambda a: a.reshape(1, -1)
    l = 0
    weights = (w_in[l].astype(_BF16), w_pool[l].astype(_BF16), row(pool_scale[l]), w_conv[l],
               w_conv_out[l].astype(_BF16), w_o[l].astype(_BF16), w_up[l].astype(_BF16),
               w_ffn_conv[l], w_down[l].astype(_BF16), row(g_pre_mix[l]), row(g_post_mix[l]),
               row(g_pre_ffn[l]), row(g_post_ffn[l]))

    zeros = lambda r, c: jnp.zeros((r, c), _F32)
    _, meta_pool, meta_conv, meta_ffn = _seq_call(
        meta_tokens.astype(_F32)[None], zeros(POOL_HALO, POOL_WIDTH), zeros(CONV_HALO, CONV_WIDTH),
        zeros(CONV_HALO, 2 * D_FF), weights, N_META, True, "meta_prefix")

    y_prompt, pp, pc, pf = _prompt_call(x_prompt, meta_pool[0], meta_conv[0], meta_ffn[0], weights,
                                        TM_PROMPT)

    to_rows_first = lambda s: jnp.transpose(s, (1, 0, 2))
    ys, us, cs, fs = _sample_call(x_sample[:, 0, :], to_rows_first(state_pool[l]),
                                  to_rows_first(state_conv[l]), to_rows_first(state_ffn[l]), weights)
    push = lambda state, new: jnp.concatenate([state[:, 1:], new[:, None, :]], axis=1)

    return (y_prompt, ys[:, None, :], pp[None], pc[None], pf[None],
            push(state_pool[l], us)[None], push(state_conv[l], cs)[None],
            push(state_ffn[l], fs)[None])
```

```python
import functools

import jax
import jax.numpy as jnp
from jax import lax
from jax.experimental import pallas as pl
from jax.experimental.pallas import tpu as pltpu

D_MODEL = 1024
N_META = 16
POOL_WINDOWS = (2, 4, 8, 16)
POOL_WIDTH = D_MODEL // 2
POOL_GC = POOL_WIDTH // len(POOL_WINDOWS)
POOL_OUT_GC = D_MODEL // len(POOL_WINDOWS)
POOL_BUF = max(POOL_WINDOWS) - 1
CONV_WIDTH = D_MODEL // 2
CONV_W = 3
D_FF = 2816
EPS = 1e-6
OFF_U, OFF_V, OFF_B, OFF_C, OFF_GA, OFF_GB = 0, 512, 1024, 1536, 2048, 3072

SUBLANES = 8
FF_CHUNK = 256
POOL_HALO = 16
CONV_HALO = 8
TM_PROMPT = 512
ROW_BLOCK = 256
VMEM_LIMIT_BYTES = 56 * 1024 * 1024

_BF16 = jnp.bfloat16
_F32 = jnp.float32


def _dot(a, b):
    return jnp.dot(a, b, preferred_element_type=_F32)


def _rms(x, g):
    ms = jnp.mean(x * x, axis=-1, keepdims=True)
    return x * lax.rsqrt(ms + EPS) * g


def _sigmoid(x):
    return 1.0 / (1.0 + jnp.exp(-x))


def _mixer(x, w, hist, mbuf, pool_div):
    (w_in, w_pool, pscale, wconv, wco, wo, _, _, _, g1, _, _, _) = w
    h = _rms(x, g1[...]).astype(_BF16)

    hist.put_u(_dot(h, w_in[:, OFF_U:OFF_U + POOL_WIDTH]))
    v = _dot(h, w_in[:, OFF_V:OFF_V + CONV_WIDTH])
    cg = _dot(h, w_in[:, OFF_C:OFF_C + CONV_WIDTH])
    hist.put_c(cg * v)
    cv = wconv[CONV_W - 1:CONV_W, :] * hist.get_c(0)
    for k in range(1, CONV_W):
        cv = cv + wconv[CONV_W - 1 - k:CONV_W - k, :] * hist.get_c(k)
    bg = _dot(h, w_in[:, OFF_B:OFF_B + CONV_WIDTH])
    b_in = (bg * cv).astype(_BF16)

    for g, win in enumerate(POOL_WINDOWS):
        cs = slice(g * POOL_GC, (g + 1) * POOL_GC)
        os_ = slice(g * POOL_OUT_GC, (g + 1) * POOL_OUT_GC)
        ug = hist.get_u(0, cs)
        s = ug
        for k in range(1, win):
            s = s + hist.get_u(k, cs)
        pooled = (pool_div(s, win) - ug).astype(_BF16)
        a = _dot(pooled, w_pool[g]) * pscale[:, os_]
        ga = _dot(h, w_in[:, OFF_GA + g * POOL_OUT_GC:OFF_GA + (g + 1) * POOL_OUT_GC])
        gb = _dot(h, w_in[:, OFF_GB + g * POOL_OUT_GC:OFF_GB + (g + 1) * POOL_OUT_GC])
        b = _dot(b_in, wco[:, os_])
        mbuf[:, os_] = (_sigmoid(ga) * a + _sigmoid(gb) * b).astype(_BF16)


def _mixer_out(w, mbuf, rows=slice(None)):
    return _dot(mbuf[rows, :], w[5][...])


def _mixer_residual(x, mixed, w):
    return x + _rms(mixed, w[10][...])


def _ffn_norm(x1, w):
    return _rms(x1, w[11][...]).astype(_BF16)


def _ffn_up(h2, w, hist, actbuf):
    wup, wfc = w[6], w[7]
    for c in range(D_FF // FF_CHUNK):
        conv = []
        for col in (c * FF_CHUNK, D_FF + c * FF_CHUNK):
            cols = slice(col, col + FF_CHUNK)
            hist.put_up(_dot(h2, wup[:, cols]), cols)
            y = wfc[CONV_W - 1:CONV_W, cols] * hist.get_up(0, cols)
            for k in range(1, CONV_W):
                y = y + wfc[CONV_W - 1 - k:CONV_W - k, cols] * hist.get_up(k, cols)
            conv.append(y)
        gate, val = conv
        actbuf[:, c * FF_CHUNK:(c + 1) * FF_CHUNK] = (gate * _sigmoid(gate) * val).astype(_BF16)


def _ffn_down(w, actbuf, rows=slice(None)):
    return _dot(actbuf[rows, :], w[8][...])


def _ffn_residual(x1, down, w):
    return x1 + _rms(down, w[12][...])


def _layer(x, w, hist, mbuf, actbuf, pool_div):
    _mixer(x, w, hist, mbuf, pool_div)
    x1 = _mixer_residual(x, _mixer_out(w, mbuf), w)
    _ffn_up(_ffn_norm(x1, w), w, hist, actbuf)
    return _ffn_residual(x1, _ffn_down(w, actbuf), w)


def _full_windows(s, win):
    return s * (1.0 / win)


class _ScratchHistory:
    def __init__(self, ubuf, cbuf, upbuf, tm):
        self.ubuf, self.cbuf, self.upbuf, self.tm = ubuf, cbuf, upbuf, tm

    def put_u(self, val):
        self.ubuf[POOL_HALO:POOL_HALO + self.tm, :] = val

    def get_u(self, k, cs):
        return self.ubuf[POOL_HALO - k:POOL_HALO - k + self.tm, cs]

    def put_c(self, val):
        self.cbuf[CONV_HALO:CONV_HALO + self.tm, :] = val

    def get_c(self, k):
        return self.cbuf[CONV_HALO - k:CONV_HALO - k + self.tm, :]

    def put_up(self, val, cols):
        self.upbuf[CONV_HALO:CONV_HALO + self.tm, cols] = val

    def get_up(self, k, cols):
        return self.upbuf[CONV_HALO - k:CONV_HALO - k + self.tm, cols]


def _seq_kernel(x_ref, ip_ref, ic_ref, if_ref, *rest, tm, first_pos_is_zero):
    w = rest[:13]
    y_ref, op_ref, oc_ref, of_ref, ubuf, cbuf, upbuf, mbuf, actbuf = rest[13:]
    i = pl.program_id(1)

    @pl.when(i == 0)
    def _():
        ubuf[0:POOL_HALO, :] = ip_ref[...]
        cbuf[0:CONV_HALO, :] = ic_ref[...]
        upbuf[0:CONV_HALO, :] = if_ref[...]

    if first_pos_is_zero:
        row1 = lax.broadcasted_iota(jnp.int32, (tm, POOL_GC), 0) + 1

        def pool_div(s, win):
            return s / jnp.minimum(win, row1).astype(_F32)
    else:
        pool_div = _full_windows

    hist = _ScratchHistory(ubuf, cbuf, upbuf, tm)
    y_ref[...] = _layer(x_ref[...], w, hist, mbuf, actbuf, pool_div)

    new_u = ubuf[tm:tm + POOL_HALO, :]
    new_c = cbuf[tm:tm + CONV_HALO, :]
    new_f = upbuf[tm:tm + CONV_HALO, :]
    ubuf[0:POOL_HALO, :] = new_u
    cbuf[0:CONV_HALO, :] = new_c
    upbuf[0:CONV_HALO, :] = new_f

    @pl.when(i == pl.num_programs(1) - 1)
    def _():
        op_ref[...] = new_u
        oc_ref[...] = new_c
        of_ref[...] = new_f


class _SegmentHistory:
    def __init__(self, ubuf, cbuf, upbuf, utail, ctail, ftail, tm):
        self.ubuf, self.cbuf, self.upbuf, self.tm = ubuf, cbuf, upbuf, tm
        self.utail, self.ctail, self.ftail = utail, ctail, ftail

    def _put(self, buf, tail, val, cols, depth):
        tm = self.tm
        buf[depth * SUBLANES:depth * SUBLANES + tm, cols] = val
        sublane = lax.broadcasted_iota(jnp.int32, (SUBLANES, val.shape[1]), 0)
        for j in range(depth):
            rows = slice(j * SUBLANES, (j + 1) * SUBLANES)
            cur = val[tm - (depth - j) * SUBLANES:tm - (depth - j - 1) * SUBLANES, :]
            mixed = jnp.where(sublane == SUBLANES - 1, tail[rows, cols], cur)
            buf[rows, cols] = pltpu.roll(mixed, 1, axis=0)
            tail[rows, cols] = cur

    def _get(self, buf, k, cols, depth):
        start = (depth - k) * SUBLANES
        return buf[start:start + self.tm, cols]

    def put_u(self, val):
        self._put(self.ubuf, self.utail, val, slice(None), POOL_BUF)

    def get_u(self, k, cs):
        return self._get(self.ubuf, k, cs, POOL_BUF)

    def put_c(self, val):
        self._put(self.cbuf, self.ctail, val, slice(None), CONV_W - 1)

    def get_c(self, k):
        return self._get(self.cbuf, k, slice(None), CONV_W - 1)

    def put_up(self, val, cols):
        self._put(self.upbuf, self.ftail, val, cols, CONV_W - 1)

    def get_up(self, k, cols):
        return self._get(self.upbuf, k, cols, CONV_W - 1)


def _fill_tail(tail, state_ref, first_row, depth):
    for j in range(depth):
        r = first_row + j
        tail[j * SUBLANES:(j + 1) * SUBLANES, :] = jnp.broadcast_to(
            state_ref[r:r + 1, :], (SUBLANES, tail.shape[1]))


def _read_tail(out_ref, tail, depth):
    for j in range(depth):
        out_ref[j:j + 1, :] = tail[(j + 1) * SUBLANES - 1:(j + 1) * SUBLANES, :]


def _prompt_kernel(x_hbm, ip_ref, ic_ref, if_ref, *rest, tm, nt, total):
    w = rest[:13]
    (y_hbm, op_ref, oc_ref, of_ref, xbuf, ybuf, in_sem, out_sem, ubuf, cbuf, upbuf,
     utail, ctail, ftail, mbuf, actbuf, h2buf, x1_new, x1_old) = rest[13:]
    seg = tm // SUBLANES
    n = pl.program_id(0)
    slot = n % 2

    def x_copies(tile, sl):
        b, t0 = tile // nt, (tile % nt) * tm
        return [pltpu.make_async_copy(x_hbm.at[b, pl.ds(t0 + s * seg, seg), :],
                                      xbuf.at[sl, :, s, :], in_sem.at[sl]) for s in range(SUBLANES)]

    def y_copies(tile, sl):
        b, t0 = tile // nt, (tile % nt) * tm
        return [pltpu.make_async_copy(ybuf.at[sl, :, s, :],
                                      y_hbm.at[b, pl.ds(t0 + s * seg, seg), :], out_sem.at[sl])
                for s in range(SUBLANES)]

    @pl.when(n == 0)
    def _():
        for cp in x_copies(n, slot):
            cp.start()
        h2buf[...] = jnp.zeros_like(h2buf)
        x1_new[...] = jnp.zeros_like(x1_new)
        ftail[...] = jnp.zeros_like(ftail)

    @pl.when(n + 1 < total)
    def _():
        for cp in x_copies(n + 1, 1 - slot):
            cp.start()

    @pl.when(n < total)
    def _():
        for cp in x_copies(n, slot):
            cp.wait()

    @pl.when(n >= 3)
    def _():
        for cp in y_copies(n - 3, 1 - slot):
            cp.wait()

    @pl.when(n % nt == 0)
    def _():
        _fill_tail(utail, ip_ref, POOL_HALO - POOL_BUF, POOL_BUF)
        _fill_tail(ctail, ic_ref, CONV_HALO - (CONV_W - 1), CONV_W - 1)

    @pl.when(n % nt == 1)
    def _():
        _fill_tail(ftail, if_ref, CONV_HALO - (CONV_W - 1), CONV_W - 1)

    hist = _SegmentHistory(ubuf, cbuf, upbuf, utail, ctail, ftail, tm)
    x1_old[...] = x1_new[...]
    _ffn_up(h2buf[...], w, hist, actbuf)
    _mixer(xbuf[slot].reshape(tm, D_MODEL), w, hist, mbuf, _full_windows)
    for blk in range(tm // ROW_BLOCK):
        rows = slice(blk * ROW_BLOCK, (blk + 1) * ROW_BLOCK)
        vregs = slice(blk * ROW_BLOCK // SUBLANES, (blk + 1) * ROW_BLOCK // SUBLANES)
        x = xbuf[slot, vregs].reshape(ROW_BLOCK, D_MODEL)
        x1 = _mixer_residual(x, _mixer_out(w, mbuf, rows), w)
        x1_new[rows, :] = x1
        h2buf[rows, :] = _ffn_norm(x1, w)
    for blk in range(tm // ROW_BLOCK):
        rows = slice(blk * ROW_BLOCK, (blk + 1) * ROW_BLOCK)
        vregs = slice(blk * ROW_BLOCK // SUBLANES, (blk + 1) * ROW_BLOCK // SUBLANES)
        y = _ffn_residual(x1_old[rows, :], _ffn_down(w, actbuf, rows), w)
        ybuf[1 - slot, vregs] = y.reshape(ROW_BLOCK // SUBLANES, SUBLANES, D_MODEL)

    @pl.when(n >= 1)
    def _():
        for cp in y_copies(n - 1, 1 - slot):
            cp.start()

    @pl.when((n % nt == nt - 1) & (n < total))
    def _():
        _read_tail(op_ref, utail, POOL_BUF)
        _read_tail(oc_ref, ctail, CONV_W - 1)

    @pl.when((n % nt == 0) & (n >= 1))
    def _():
        _read_tail(of_ref, ftail, CONV_W - 1)

    @pl.when(n == total)
    def _():
        if total >= 2:
            for cp in y_copies(n - 2, slot):
                cp.wait()
        for cp in y_copies(n - 1, 1 - slot):
            cp.wait()


class _StateHistory:
    def __init__(self, sp_ref, sc_ref, sf_ref, u_ref, c_ref, f_ref):
        self.sp, self.sc, self.sf = sp_ref, sc_ref, sf_ref
        self.u, self.c, self.f = u_ref, c_ref, f_ref

    def put_u(self, val):
        self.u[...] = val

    def get_u(self, k, cs):
        return self.u[:, cs] if k == 0 else self.sp[POOL_BUF - k, :, cs]

    def put_c(self, val):
        self.c[...] = val

    def get_c(self, k):
        return self.c[...] if k == 0 else self.sc[CONV_W - 1 - k]

    def put_up(self, val, cols):
        self.f[:, cols] = val

    def get_up(self, k, cols):
        return self.f[:, cols] if k == 0 else self.sf[CONV_W - 1 - k, :, cols]


def _sample_kernel(x_ref, sp_ref, sc_ref, sf_ref, *rest):
    w = rest[:13]
    y_ref, u_ref, c_ref, f_ref, mbuf, actbuf = rest[13:]
    hist = _StateHistory(sp_ref, sc_ref, sf_ref, u_ref, c_ref, f_ref)
    y_ref[...] = _layer(x_ref[...], w, hist, mbuf, actbuf, _full_windows)


def _resident(shape):
    zeros = (0,) * len(shape)
    return pl.BlockSpec(shape, lambda *_: zeros, pipeline_mode=pl.Buffered(1))


def _weight_specs(weights):
    return [_resident(a.shape) for a in weights]


def _seq_call(x, init_pool, init_conv, init_ffn, weights, tm, first_pos_is_zero, name):
    nb, t, d = x.shape
    assert t % tm == 0 and tm >= POOL_HALO
    grid = (nb, t // tm)
    state_spec = lambda rows, c: pl.BlockSpec((None, rows, c), lambda b, i: (b, 0, 0))
    return pl.pallas_call(
        functools.partial(_seq_kernel, tm=tm, first_pos_is_zero=first_pos_is_zero),
        grid=grid,
        in_specs=[pl.BlockSpec((None, tm, d), lambda b, i: (b, i, 0)),
                  _resident(init_pool.shape), _resident(init_conv.shape), _resident(init_ffn.shape)]
                 + _weight_specs(weights),
        out_specs=[pl.BlockSpec((None, tm, d), lambda b, i: (b, i, 0)),
                   state_spec(POOL_HALO, POOL_WIDTH), state_spec(CONV_HALO, CONV_WIDTH),
                   state_spec(CONV_HALO, 2 * D_FF)],
        out_shape=[jax.ShapeDtypeStruct((nb, t, d), _F32),
                   jax.ShapeDtypeStruct((nb, POOL_HALO, POOL_WIDTH), _F32),
                   jax.ShapeDtypeStruct((nb, CONV_HALO, CONV_WIDTH), _F32),
                   jax.ShapeDtypeStruct((nb, CONV_HALO, 2 * D_FF), _F32)],
        scratch_shapes=[pltpu.VMEM((POOL_HALO + tm, POOL_WIDTH), _F32),
                        pltpu.VMEM((CONV_HALO + tm, CONV_WIDTH), _F32),
                        pltpu.VMEM((CONV_HALO + tm, 2 * D_FF), _F32),
                        pltpu.VMEM((tm, d), _BF16),
                        pltpu.VMEM((tm, D_FF), _BF16)],
        compiler_params=pltpu.CompilerParams(
            dimension_semantics=("arbitrary", "arbitrary"), vmem_limit_bytes=VMEM_LIMIT_BYTES),
        name=name,
    )(x, init_pool, init_conv, init_ffn, *weights)


def _prompt_call(x, init_pool, init_conv, init_ffn, weights, tm):
    nb, t, d = x.shape
    seg = tm // SUBLANES
    assert t % tm == 0 and tm % SUBLANES == 0 and seg >= POOL_BUF
    nt = t // tm
    assert nt >= 2
    total = nb * nt
    mix_state = lambda rows, c: pl.BlockSpec(
        (None, rows, c), lambda n: (jnp.minimum(n, total - 1) // nt, 0, 0))
    ffn_state = lambda rows, c: pl.BlockSpec(
        (None, rows, c), lambda n: (jnp.maximum(n - 1, 0) // nt, 0, 0))
    hbm = pl.BlockSpec(memory_space=pl.ANY)
    rows_with_head = lambda depth: (depth + seg) * SUBLANES
    return pl.pallas_call(
        functools.partial(_prompt_kernel, tm=tm, nt=nt, total=total),
        grid=(total + 1,),
        in_specs=[hbm, _resident(init_pool.shape), _resident(init_conv.shape),
                  _resident(init_ffn.shape)] + _weight_specs(weights),
        out_specs=[hbm, mix_state(POOL_BUF, POOL_WIDTH), mix_state(CONV_W - 1, CONV_WIDTH),
                   ffn_state(CONV_W - 1, 2 * D_FF)],
        out_shape=[jax.ShapeDtypeStruct((nb, t, d), _F32),
                   jax.ShapeDtypeStruct((nb, POOL_BUF, POOL_WIDTH), _F32),
                   jax.ShapeDtypeStruct((nb, CONV_W - 1, CONV_WIDTH), _F32),
                   jax.ShapeDtypeStruct((nb, CONV_W - 1, 2 * D_FF), _F32)],
        scratch_shapes=[pltpu.VMEM((2, seg, SUBLANES, d), _F32),
                        pltpu.VMEM((2, seg, SUBLANES, d), _F32),
                        pltpu.SemaphoreType.DMA((2,)),
                        pltpu.SemaphoreType.DMA((2,)),
                        pltpu.VMEM((rows_with_head(POOL_BUF), POOL_WIDTH), _F32),
                        pltpu.VMEM((rows_with_head(CONV_W - 1), CONV_WIDTH), _F32),
                        pltpu.VMEM((rows_with_head(CONV_W - 1), 2 * D_FF), _F32),
                        pltpu.VMEM((POOL_BUF * SUBLANES, POOL_WIDTH), _F32),
                        pltpu.VMEM(((CONV_W - 1) * SUBLANES, CONV_WIDTH), _F32),
                        pltpu.VMEM(((CONV_W - 1) * SUBLANES, 2 * D_FF), _F32),
                        pltpu.VMEM((tm, d), _BF16),
                        pltpu.VMEM((tm, D_FF), _BF16),
                        pltpu.VMEM((tm, d), _BF16),
                        pltpu.VMEM((tm, d), _F32),
                        pltpu.VMEM((tm, d), _F32)],
        compiler_params=pltpu.CompilerParams(
            dimension_semantics=("arbitrary",), vmem_limit_bytes=VMEM_LIMIT_BYTES),
        name="prompt_step",
    )(x, init_pool, init_conv, init_ffn, *weights)


def _sample_call(x, sp, sc, sf, weights):
    n, d = x.shape
    return pl.pallas_call(
        _sample_kernel,
        grid=(1,),
        in_specs=[_resident(x.shape), _resident(sp.shape), _resident(sc.shape), _resident(sf.shape)]
                 + _weight_specs(weights),
        out_specs=[_resident((n, d)), _resident((n, POOL_WIDTH)), _resident((n, CONV_WIDTH)),
                   _resident((n, 2 * D_FF))],
        out_shape=[jax.ShapeDtypeStruct((n, d), _F32),
                   jax.ShapeDtypeStruct((n, POOL_WIDTH), _F32),
                   jax.ShapeDtypeStruct((n, CONV_WIDTH), _F32),
                   jax.ShapeDtypeStruct((n, 2 * D_FF), _F32)],
        scratch_shapes=[pltpu.VMEM((n, d), _BF16), pltpu.VMEM((n, D_FF), _BF16)],
        compiler_params=pltpu.CompilerParams(
            dimension_semantics=("arbitrary",), vmem_limit_bytes=VMEM_LIMIT_BYTES),
        name="sample_step",
    )(x, sp, sc, sf, *weights)


def kernel(x_prompt, x_sample, state_pool, state_conv, state_ffn, meta_tokens, w_in, w_pool,
           pool_scale, w_conv, w_conv_out, w_o, w_up, w_ffn_conv, w_down, g_pre_mix,
           g_post_mix, g_pre_ffn, g_post_ffn):
    depth = w_in.shape[0]
    assert depth == 1, "the prompt / sample streams are chained for a single layer only"
    assert x_sample.shape[1] == 1
    assert N_META >= max(POOL_WINDOWS)
    row = lambda a: a.reshape(1, -1)
    l = 0
    weights = (w_in[l].astype(_BF16), w_pool[l].astype(_BF16), row(pool_scale[l]), w_conv[l],
               w_conv_out[l].astype(_BF16), w_o[l].astype(_BF16), w_up[l].astype(_BF16),
               w_ffn_conv[l], w_down[l].astype(_BF16), row(g_pre_mix[l]), row(g_post_mix[l]),
               row(g_pre_ffn[l]), row(g_post_ffn[l]))

    zeros = lambda r, c: jnp.zeros((r, c), _F32)
    _, meta_pool, meta_conv, meta_ffn = _seq_call(
        meta_tokens.astype(_F32)[None], zeros(POOL_HALO, POOL_WIDTH), zeros(CONV_HALO, CONV_WIDTH),
        zeros(CONV_HALO, 2 * D_FF), weights, N_META, True, "meta_prefix")

    y_prompt, pp, pc, pf = _prompt_call(x_prompt, meta_pool[0], meta_conv[0], meta_ffn[0], weights,
                                        TM_PROMPT)

    to_rows_first = lambda s: jnp.transpose(s, (1, 0, 2))
    ys, us, cs, fs = _sample_call(x_sample[:, 0, :], to_rows_first(state_pool[l]),
                                  to_rows_first(state_conv[l]), to_rows_first(state_ffn[l]), weights)
    push = lambda state, new: jnp.concatenate([state[:, 1:], new[:, None, :]], axis=1)

    return (y_prompt, ys[:, None, :], pp[None], pc[None], pf[None],
            push(state_pool[l], us)[None], push(state_conv[l], cs)[None],
            push(state_ffn[l], fs)[None])
```

```python
import functools

import jax
import jax.numpy as jnp
from jax import lax
from jax.experimental import pallas as pl
from jax.experimental.pallas import tpu as pltpu

D_MODEL = 1024
N_META = 16
POOL_WINDOWS = (2, 4, 8, 16)
N_POOL_GROUPS = len(POOL_WINDOWS)
POOL_WIDTH = D_MODEL // 2
POOL_GC = POOL_WIDTH // N_POOL_GROUPS
POOL_OUT_GC = D_MODEL // N_POOL_GROUPS
POOL_BUF = max(POOL_WINDOWS) - 1
CONV_WIDTH = D_MODEL // 2
CONV_W = 3
D_FF = 2816
IN_TOTAL = 2 * POOL_WIDTH + 2 * CONV_WIDTH + 2 * D_MODEL
EPS = 1e-6
OFF_U, OFF_V, OFF_B, OFF_C, OFF_GA, OFF_GB = 0, 512, 1024, 1536, 2048, 3072

SUBLANES = 8
FF_CHUNK = 256
POOL_HALO = 16
CONV_HALO = 8
TM_PROMPT = 256
ROW_BLOCK = 256
CAST_ROWS = 128
STAGE_COLS = 2 * D_FF
VMEM_LIMIT_BYTES = 56 * 1024 * 1024

_BF16 = jnp.bfloat16
_F32 = jnp.float32


def _dot(a, b):
    return jnp.dot(a, b, preferred_element_type=_F32)


def _rms(x, g):
    ms = jnp.mean(x * x, axis=-1, keepdims=True)
    return x * lax.rsqrt(ms + EPS) * g


def _sigmoid(x):
    return 1.0 / (1.0 + jnp.exp(-x))


def _mixer(x, w, hist, mbuf, pool_div):
    (w_in, w_pool, pscale, wconv, wco, _, _, _, _, g1, _, _, _) = w
    h = _rms(x, g1[...]).astype(_BF16)

    hist.put_u(_dot(h, w_in[:, OFF_U:OFF_U + POOL_WIDTH]))
    v = _dot(h, w_in[:, OFF_V:OFF_V + CONV_WIDTH])
    cg = _dot(h, w_in[:, OFF_C:OFF_C + CONV_WIDTH])
    hist.put_c(cg * v)
    cv = wconv[CONV_W - 1:CONV_W, :] * hist.get_c(0)
    for k in range(1, CONV_W):
        cv = cv + wconv[CONV_W - 1 - k:CONV_W - k, :] * hist.get_c(k)
    bg = _dot(h, w_in[:, OFF_B:OFF_B + CONV_WIDTH])
    b_in = (bg * cv).astype(_BF16)

    for g, win in enumerate(POOL_WINDOWS):
        cs = slice(g * POOL_GC, (g + 1) * POOL_GC)
        os_ = slice(g * POOL_OUT_GC, (g + 1) * POOL_OUT_GC)
        ug = hist.get_u(0, cs)
        s = ug
        for k in range(1, win):
            s = s + hist.get_u(k, cs)
        pooled = (pool_div(s, win) - ug).astype(_BF16)
        a = _dot(pooled, w_pool[g]) * pscale[:, os_]
        ga = _dot(h, w_in[:, OFF_GA + g * POOL_OUT_GC:OFF_GA + (g + 1) * POOL_OUT_GC])
        gb = _dot(h, w_in[:, OFF_GB + g * POOL_OUT_GC:OFF_GB + (g + 1) * POOL_OUT_GC])
        b = _dot(b_in, wco[:, os_])
        mbuf[:, os_] = (_sigmoid(ga) * a + _sigmoid(gb) * b).astype(_BF16)


def _mixer_out(w, mbuf, rows=slice(None)):
    return _dot(mbuf[rows, :], w[5][...])


def _mixer_residual(x, mixed, w):
    return x + _rms(mixed, w[10][...])


def _ffn_norm(x1, w):
    return _rms(x1, w[11][...]).astype(_BF16)


def _ffn_up(h2, w, hist, actbuf):
    wup, wfc = w[6], w[7]
    for c in range(D_FF // FF_CHUNK):
        conv = []
        for col in (c * FF_CHUNK, D_FF + c * FF_CHUNK):
            cols = slice(col, col + FF_CHUNK)
            hist.put_up(_dot(h2, wup[:, cols]), cols)
            y = wfc[CONV_W - 1:CONV_W, cols] * hist.get_up(0, cols)
            for k in range(1, CONV_W):
                y = y + wfc[CONV_W - 1 - k:CONV_W - k, cols] * hist.get_up(k, cols)
            conv.append(y)
        gate, val = conv
        actbuf[:, c * FF_CHUNK:(c + 1) * FF_CHUNK] = (gate * _sigmoid(gate) * val).astype(_BF16)


def _ffn_down(w, actbuf, rows=slice(None)):
    return _dot(actbuf[rows, :], w[8][...])


def _ffn_residual(x1, down, w):
    return x1 + _rms(down, w[12][...])


def _layer(x, w, hist, mbuf, actbuf, pool_div):
    _mixer(x, w, hist, mbuf, pool_div)
    x1 = _mixer_residual(x, _mixer_out(w, mbuf), w)
    _ffn_up(_ffn_norm(x1, w), w, hist, actbuf)
    return _ffn_residual(x1, _ffn_down(w, actbuf), w)


def _full_windows(s, win):
    return s * (1.0 / win)


class _RowHistory:
    def __init__(self, ubuf, cbuf, upbuf, m):
        self.ubuf, self.cbuf, self.upbuf, self.m = ubuf, cbuf, upbuf, m

    def put_u(self, val):
        self.ubuf[POOL_HALO:POOL_HALO + self.m, :] = val

    def get_u(self, k, cs):
        return self.ubuf[POOL_HALO - k:POOL_HALO - k + self.m, cs]

    def put_c(self, val):
        self.cbuf[CONV_HALO:CONV_HALO + self.m, :] = val

    def get_c(self, k):
        return self.cbuf[CONV_HALO - k:CONV_HALO - k + self.m, :]

    def put_up(self, val, cols):
        self.upbuf[CONV_HALO:CONV_HALO + self.m, cols] = val

    def get_up(self, k, cols):
        return self.upbuf[CONV_HALO - k:CONV_HALO - k + self.m, cols]


class _SegmentHistory:
    def __init__(self, ubuf, cbuf, upbuf, utail, ctail, ftail, tm):
        self.ubuf, self.cbuf, self.upbuf, self.tm = ubuf, cbuf, upbuf, tm
        self.utail, self.ctail, self.ftail = utail, ctail, ftail

    def _put(self, buf, tail, val, cols, depth):
        tm = self.tm
        buf[depth * SUBLANES:depth * SUBLANES + tm, cols] = val
        sublane = lax.broadcasted_iota(jnp.int32, (SUBLANES, val.shape[1]), 0)
        for j in range(depth):
            rows = slice(j * SUBLANES, (j + 1) * SUBLANES)
            cur = val[tm - (depth - j) * SUBLANES:tm - (depth - j - 1) * SUBLANES, :]
            mixed = jnp.where(sublane == SUBLANES - 1, tail[rows, cols], cur)
            buf[rows, cols] = pltpu.roll(mixed, 1, axis=0)
            tail[rows, cols] = cur

    def _get(self, buf, k, cols, depth):
        start = (depth - k) * SUBLANES
        return buf[start:start + self.tm, cols]

    def put_u(self, val):
        self._put(self.ubuf, self.utail, val, slice(None), POOL_BUF)

    def get_u(self, k, cs):
        return self._get(self.ubuf, k, cs, POOL_BUF)

    def put_c(self, val):
        self._put(self.cbuf, self.ctail, val, slice(None), CONV_W - 1)

    def get_c(self, k):
        return self._get(self.cbuf, k, slice(None), CONV_W - 1)

    def put_up(self, val, cols):
        self._put(self.upbuf, self.ftail, val, cols, CONV_W - 1)

    def get_up(self, k, cols):
        return self._get(self.upbuf, k, cols, CONV_W - 1)


class _StateHistory:
    def __init__(self, sp_ref, sc_ref, sf_ref, u_ref, c_ref, f_ref):
        self.sp, self.sc, self.sf = sp_ref, sc_ref, sf_ref
        self.u, self.c, self.f = u_ref, c_ref, f_ref

    def put_u(self, val):
        self.u[...] = val

    def get_u(self, k, cs):
        return self.u[:, cs] if k == 0 else self.sp[POOL_BUF - k, :, cs]

    def put_c(self, val):
        self.c[...] = val

    def get_c(self, k):
        return self.c[...] if k == 0 else self.sc[CONV_W - 1 - k]

    def put_up(self, val, cols):
        self.f[:, cols] = val

    def get_up(self, k, cols):
        return self.f[:, cols] if k == 0 else self.sf[CONV_W - 1 - k, :, cols]


def _fill_tail(tail, state_ref, first_row, depth):
    for j in range(depth):
        r = first_row + j
        tail[j * SUBLANES:(j + 1) * SUBLANES, :] = jnp.broadcast_to(
            state_ref[r:r + 1, :], (SUBLANES, tail.shape[1]))


def _read_tail(out_ref, tail, depth):
    for j in range(depth):
        out_ref[j:j + 1, :] = tail[(j + 1) * SUBLANES - 1:(j + 1) * SUBLANES, :]


def _cast_batches(pairs):
    jobs = []
    for src, dst in pairs:
        if len(src.shape) == 3:
            assert src.shape[1] == CAST_ROWS
            jobs += [(src.at[g], dst.at[g], src.shape[2]) for g in range(src.shape[0])]
        else:
            assert src.shape[0] % CAST_ROWS == 0
            jobs += [(src.at[pl.ds(r, CAST_ROWS), :], dst.at[pl.ds(r, CAST_ROWS), :], src.shape[1])
                     for r in range(0, src.shape[0], CAST_ROWS)]
    batches, cur, used = [], [], 0
    for s, d, cols in jobs:
        if used + cols > STAGE_COLS:
            batches.append(cur)
            cur, used = [], 0
        cur.append((s, d, used, cols))
        used += cols
    batches.append(cur)
    return batches


def _cast_weights(pairs, stage, sem):
    batches = _cast_batches(pairs)

    def copies(b, slot):
        return [pltpu.make_async_copy(s, stage.at[slot, :, pl.ds(lane, cols)], sem.at[slot])
                for s, _, lane, cols in batches[b]]

    for b in range(min(2, len(batches))):
        for cp in copies(b, b):
            cp.start()
    for b in range(len(batches)):
        slot = b % 2
        for cp in copies(b, slot):
            cp.wait()
        for _, d, lane, cols in batches[b]:
            d[...] = stage[slot, :, lane:lane + cols].astype(_BF16)
        if b + 2 < len(batches):
            for cp in copies(b + 2, slot):
                cp.start()


def _fused_kernel(x_hbm, xs_ref, sp_hbm, sc_hbm, sf_hbm, meta_ref,
                  w_in_hbm, w_pool_hbm, wco_hbm, wo_hbm, wup_hbm, wdown_hbm,
                  pscale, wconv, wfc, g1, g2, g3, g4,
                  y_hbm, ys_ref, op_ref, oc_ref, of_ref, us_ref, cs_ref, fs_ref,
                  w_in, w_pool, wco, wo, wup, wdown, stage, sp_v, sc_v, m_ubuf, m_cbuf, m_upbuf,
                  xbuf, ybuf, in_sem, out_sem, cast_sem, state_sem,
                  ubuf, cbuf, upbuf, utail, ctail, ftail, mbuf, actbuf, h2buf, x1_new, x1_old,
                  *, tm, nt, total):
    w = (w_in, w_pool, pscale, wconv, wco, wo, wup, wfc, wdown, g1, g2, g3, g4)
    seg = tm // SUBLANES
    n = pl.program_id(0)
    slot = n % 2
    n_sample = xs_ref.shape[0]

    def x_copies(tile, sl):
        b, t0 = tile // nt, (tile % nt) * tm
        return [pltpu.make_async_copy(x_hbm.at[b, pl.ds(t0 + s * seg, seg), :],
                                      xbuf.at[sl, :, s, :], in_sem.at[sl]) for s in range(SUBLANES)]

    def y_copies(tile, sl):
        b, t0 = tile // nt, (tile % nt) * tm
        return [pltpu.make_async_copy(ybuf.at[sl, :, s, :],
                                      y_hbm.at[b, pl.ds(t0 + s * seg, seg), :], out_sem.at[sl])
                for s in range(SUBLANES)]

    @pl.when(n == 0)
    def _():
        for cp in x_copies(n, slot):
            cp.start()
        pool_cp = pltpu.make_async_copy(sp_hbm, sp_v, state_sem.at[0])
        conv_cp = pltpu.make_async_copy(sc_hbm, sc_v, state_sem.at[1])
        ffn_cp = pltpu.make_async_copy(sf_hbm, stage, state_sem.at[2])
        pool_cp.start()
        conv_cp.start()

        _cast_weights([(w_in_hbm, w_in), (w_pool_hbm, w_pool), (wco_hbm, wco), (wo_hbm, wo),
                       (wup_hbm, wup), (wdown_hbm, wdown)], stage, cast_sem)
        ffn_cp.start()

        m_ubuf[0:POOL_HALO, :] = jnp.zeros((POOL_HALO, POOL_WIDTH), _F32)
        m_cbuf[0:CONV_HALO, :] = jnp.zeros((CONV_HALO, CONV_WIDTH), _F32)
        m_upbuf[0:CONV_HALO, :] = jnp.zeros((CONV_HALO, 2 * D_FF), _F32)
        row1 = lax.broadcasted_iota(jnp.int32, (N_META, POOL_GC), 0) + 1

        def meta_windows(s, win):
            return s / jnp.minimum(win, row1).astype(_F32)

        meta_hist = _RowHistory(m_ubuf, m_cbuf, m_upbuf, N_META)
        _mixer(meta_ref[...], w, meta_hist, mbuf.at[pl.ds(0, N_META)], meta_windows)
        meta_x1 = _mixer_residual(meta_ref[...], _mixer_out(w, mbuf.at[pl.ds(0, N_META)]), w)
        _ffn_up(_ffn_norm(meta_x1, w), w, meta_hist, actbuf.at[pl.ds(0, N_META)])

        pool_cp.wait()
        conv_cp.wait()
        ffn_cp.wait()
        sample_hist = _StateHistory(sp_v, sc_v, stage, us_ref, cs_ref, fs_ref)
        ys_ref[...] = _layer(xs_ref[...], w, sample_hist, mbuf.at[pl.ds(0, n_sample)],
                             actbuf.at[pl.ds(0, n_sample)], _full_windows)

        h2buf[...] = jnp.zeros_like(h2buf)
        x1_new[...] = jnp.zeros_like(x1_new)
        ftail[...] = jnp.zeros_like(ftail)

    @pl.when(n + 1 < total)
    def _():
        for cp in x_copies(n + 1, 1 - slot):
            cp.start()

    @pl.when(n < total)
    def _():
        for cp in x_copies(n, slot):
            cp.wait()

    @pl.when(n >= 3)
    def _():
        for cp in y_copies(n - 3, 1 - slot):
            cp.wait()

    @pl.when(n % nt == 0)
    def _():
        _fill_tail(utail, m_ubuf, POOL_HALO + N_META - POOL_BUF, POOL_BUF)
        _fill_tail(ctail, m_cbuf, CONV_HALO + N_META - (CONV_W - 1), CONV_W - 1)

    @pl.when(n % nt == 1)
    def _():
        _fill_tail(ftail, m_upbuf, CONV_HALO + N_META - (CONV_W - 1), CONV_W - 1)

    hist = _SegmentHistory(ubuf, cbuf, upbuf, utail, ctail, ftail, tm)
    x1_old[...] = x1_new[...]
    _ffn_up(h2buf[...], w, hist, actbuf)
    _mixer(xbuf[slot].reshape(tm, D_MODEL), w, hist, mbuf, _full_windows)
    for blk in range(tm // ROW_BLOCK):
        rows = slice(blk * ROW_BLOCK, (blk + 1) * ROW_BLOCK)
        vregs = slice(blk * ROW_BLOCK // SUBLANES, (blk + 1) * ROW_BLOCK // SUBLANES)
        x = xbuf[slot, vregs].reshape(ROW_BLOCK, D_MODEL)
        x1 = _mixer_residual(x, _mixer_out(w, mbuf, rows), w)
        x1_new[rows, :] = x1
        h2buf[rows, :] = _ffn_norm(x1, w)
    for blk in range(tm // ROW_BLOCK):
        rows = slice(blk * ROW_BLOCK, (blk + 1) * ROW_BLOCK)
        vregs = slice(blk * ROW_BLOCK // SUBLANES, (blk + 1) * ROW_BLOCK // SUBLANES)
        y = _ffn_residual(x1_old[rows, :], _ffn_down(w, actbuf, rows), w)
        ybuf[1 - slot, vregs] = y.reshape(ROW_BLOCK // SUBLANES, SUBLANES, D_MODEL)

    @pl.when(n >= 1)
    def _():
        for cp in y_copies(n - 1, 1 - slot):
            cp.start()

    @pl.when((n % nt == nt - 1) & (n < total))
    def _():
        _read_tail(op_ref, utail, POOL_BUF)
        _read_tail(oc_ref, ctail, CONV_W - 1)

    @pl.when((n % nt == 0) & (n >= 1))
    def _():
        _read_tail(of_ref, ftail, CONV_W - 1)

    @pl.when(n == total)
    def _():
        if total >= 2:
            for cp in y_copies(n - 2, slot):
                cp.wait()
        for cp in y_copies(n - 1, 1 - slot):
            cp.wait()


def _resident(shape):
    zeros = (0,) * len(shape)
    return pl.BlockSpec(shape, lambda *_: zeros, pipeline_mode=pl.Buffered(1))


def _fused_call(x, xs, sp, sc, sf, meta, big_weights, small_weights, tm):
    nb, t, d = x.shape
    ns = xs.shape[0]
    seg = tm // SUBLANES
    assert t % tm == 0 and tm % ROW_BLOCK == 0 and ROW_BLOCK % SUBLANES == 0 and seg >= POOL_BUF
    assert ns <= tm and sf.shape == (2, CAST_ROWS, STAGE_COLS)
    nt = t // tm
    assert nt >= 2
    total = nb * nt
    mix_state = lambda rows, c: pl.BlockSpec(
        (None, rows, c), lambda n: (jnp.minimum(n, total - 1) // nt, 0, 0))
    ffn_state = lambda rows, c: pl.BlockSpec(
        (None, rows, c), lambda n: (jnp.maximum(n - 1, 0) // nt, 0, 0))
    hbm = pl.BlockSpec(memory_space=pl.ANY)
    rows_with_head = lambda depth: (depth + seg) * SUBLANES
    return pl.pallas_call(
        functools.partial(_fused_kernel, tm=tm, nt=nt, total=total),
        grid=(total + 1,),
        in_specs=[hbm, _resident(xs.shape), hbm, hbm, hbm, _resident(meta.shape)]
                 + [hbm] * len(big_weights) + [_resident(a.shape) for a in small_weights],
        out_specs=[hbm, _resident((ns, d)),
                   mix_state(POOL_BUF, POOL_WIDTH), mix_state(CONV_W - 1, CONV_WIDTH),
                   ffn_state(CONV_W - 1, 2 * D_FF),
                   _resident((ns, POOL_WIDTH)), _resident((ns, CONV_WIDTH)),
                   _resident((ns, 2 * D_FF))],
        out_shape=[jax.ShapeDtypeStruct((nb, t, d), _F32),
                   jax.ShapeDtypeStruct((ns, d), _F32),
                   jax.ShapeDtypeStruct((nb, POOL_BUF, POOL_WIDTH), _F32),
                   jax.ShapeDtypeStruct((nb, CONV_W - 1, CONV_WIDTH), _F32),
                   jax.ShapeDtypeStruct((nb, CONV_W - 1, 2 * D_FF), _F32),
                   jax.ShapeDtypeStruct((ns, POOL_WIDTH), _F32),
                   jax.ShapeDtypeStruct((ns, CONV_WIDTH), _F32),
                   jax.ShapeDtypeStruct((ns, 2 * D_FF), _F32)],
        scratch_shapes=[pltpu.VMEM((d, IN_TOTAL), _BF16),
                        pltpu.VMEM((N_POOL_GROUPS, POOL_GC, POOL_OUT_GC), _BF16),
                        pltpu.VMEM((CONV_WIDTH, d), _BF16),
                        pltpu.VMEM((d, d), _BF16),
                        pltpu.VMEM((d, 2 * D_FF), _BF16),
                        pltpu.VMEM((D_FF, d), _BF16),
                        pltpu.VMEM((2, CAST_ROWS, STAGE_COLS), _F32),
                        pltpu.VMEM(sp.shape, _F32),
                        pltpu.VMEM(sc.shape, _F32),
                        pltpu.VMEM((POOL_HALO + N_META, POOL_WIDTH), _F32),
                        pltpu.VMEM((CONV_HALO + N_META, CONV_WIDTH), _F32),
                        pltpu.VMEM((CONV_HALO + N_META, 2 * D_FF), _F32),
                        pltpu.VMEM((2, seg, SUBLANES, d), _F32),
                        pltpu.VMEM((2, seg, SUBLANES, d), _F32),
                        pltpu.SemaphoreType.DMA((2,)),
                        pltpu.SemaphoreType.DMA((2,)),
                        pltpu.SemaphoreType.DMA((2,)),
                        pltpu.SemaphoreType.DMA((3,)),
                        pltpu.VMEM((rows_with_head(POOL_BUF), POOL_WIDTH), _F32),
                        pltpu.VMEM((rows_with_head(CONV_W - 1), CONV_WIDTH), _F32),
                        pltpu.VMEM((rows_with_head(CONV_W - 1), 2 * D_FF), _F32),
                        pltpu.VMEM((POOL_BUF * SUBLANES, POOL_WIDTH), _F32),
                        pltpu.VMEM(((CONV_W - 1) * SUBLANES, CONV_WIDTH), _F32),
                        pltpu.VMEM(((CONV_W - 1) * SUBLANES, 2 * D_FF), _F32),
                        pltpu.VMEM((tm, d), _BF16),
                        pltpu.VMEM((tm, D_FF), _BF16),
                        pltpu.VMEM((tm, d), _BF16),
                        pltpu.VMEM((tm, d), _F32),
                        pltpu.VMEM((tm, d), _F32)],
        compiler_params=pltpu.CompilerParams(
            dimension_semantics=("arbitrary",), vmem_limit_bytes=VMEM_LIMIT_BYTES),
        name="hybrid_step",
    )(x, xs, sp, sc, sf, meta, *big_weights, *small_weights)


def kernel(x_prompt, x_sample, state_pool, state_conv, state_ffn, meta_tokens, w_in, w_pool,
           pool_scale, w_conv, w_conv_out, w_o, w_up, w_ffn_conv, w_down, g_pre_mix,
           g_post_mix, g_pre_ffn, g_post_ffn):
    depth = w_in.shape[0]
    assert depth == 1, "the prompt / sample streams are chained for a single layer only"
    assert x_sample.shape[1] == 1 and meta_tokens.shape[0] == N_META
    assert N_META >= max(POOL_WINDOWS)
    row = lambda a: a.reshape(1, -1)
    l = 0
    big = (w_in[l], w_pool[l], w_conv_out[l], w_o[l], w_up[l], w_down[l])
    small = (row(pool_scale[l]), w_conv[l], w_ffn_conv[l], row(g_pre_mix[l]), row(g_post_mix[l]),
             row(g_pre_ffn[l]), row(g_post_ffn[l]))

    to_rows_first = lambda s: jnp.transpose(s, (1, 0, 2))
    y_prompt, ys, pp, pc, pf, us, cs, fs = _fused_call(
        x_prompt, x_sample[:, 0, :], to_rows_first(state_pool[l]), to_rows_first(state_conv[l]),
        to_rows_first(state_ffn[l]), meta_tokens.astype(_F32), big, small, TM_PROMPT)
    push = lambda state, new: jnp.concatenate([state[:, 1:], new[:, None, :]], axis=1)

    return (y_prompt, ys[:, None, :], pp[None], pc[None], pf[None],
            push(state_pool[l], us)[None], push(state_conv[l], cs)[None],
            push(state_ffn[l], fs)[None])
```

```python
import functools

import jax
import jax.numpy as jnp
from jax import lax
from jax.experimental import pallas as pl
from jax.experimental.pallas import tpu as pltpu

D_MODEL = 1024
N_META = 16
POOL_WINDOWS = (2, 4, 8, 16)
N_POOL_GROUPS = len(POOL_WINDOWS)
POOL_WIDTH = D_MODEL // 2
POOL_GC = POOL_WIDTH // N_POOL_GROUPS
POOL_OUT_GC = D_MODEL // N_POOL_GROUPS
POOL_BUF = max(POOL_WINDOWS) - 1
CONV_WIDTH = D_MODEL // 2
CONV_W = 3
D_FF = 2816
IN_TOTAL = 2 * POOL_WIDTH + 2 * CONV_WIDTH + 2 * D_MODEL
EPS = 1e-6
OFF_U, OFF_V, OFF_B, OFF_C, OFF_GA, OFF_GB = 0, 512, 1024, 1536, 2048, 3072

SUBLANES = 8
FF_CHUNK = 256
POOL_HALO = 16
CONV_HALO = 8
TM_PROMPT = 256
CAST_ROWS = 128
STAGE_COLS = 2 * D_FF
VMEM_LIMIT_BYTES = 58 * 1024 * 1024

_BF16 = jnp.bfloat16
_F32 = jnp.float32


def _dot(a, b):
    return jnp.dot(a, b, preferred_element_type=_F32)


def _rms(x, g):
    ms = jnp.mean(x * x, axis=-1, keepdims=True)
    return x * lax.rsqrt(ms + EPS) * g


def _sigmoid(x):
    return 1.0 / (1.0 + jnp.exp(-x))


def _mixer(x, w, hist, mbuf, pool_div):
    (w_in, w_pool, pscale, wconv, wco, _, _, _, _, g1, _, _, _) = w
    h = _rms(x, g1[...]).astype(_BF16)

    hist.put_u(_dot(h, w_in[:, OFF_U:OFF_U + POOL_WIDTH]))
    v = _dot(h, w_in[:, OFF_V:OFF_V + CONV_WIDTH])
    cg = _dot(h, w_in[:, OFF_C:OFF_C + CONV_WIDTH])
    hist.put_c(cg * v)
    cv = wconv[CONV_W - 1:CONV_W, :] * hist.get_c(0)
    for k in range(1, CONV_W):
        cv = cv + wconv[CONV_W - 1 - k:CONV_W - k, :] * hist.get_c(k)
    bg = _dot(h, w_in[:, OFF_B:OFF_B + CONV_WIDTH])
    b_in = (bg * cv).astype(_BF16)

    for g, win in enumerate(POOL_WINDOWS):
        cs = slice(g * POOL_GC, (g + 1) * POOL_GC)
        os_ = slice(g * POOL_OUT_GC, (g + 1) * POOL_OUT_GC)
        ug = hist.get_u(0, cs)
        s = ug
        for k in range(1, win):
            s = s + hist.get_u(k, cs)
        pooled = (pool_div(s, win) - ug).astype(_BF16)
        a = _dot(pooled, w_pool[g]) * pscale[:, os_]
        ga = _dot(h, w_in[:, OFF_GA + g * POOL_OUT_GC:OFF_GA + (g + 1) * POOL_OUT_GC])
        gb = _dot(h, w_in[:, OFF_GB + g * POOL_OUT_GC:OFF_GB + (g + 1) * POOL_OUT_GC])
        b = _dot(b_in, wco[:, os_])
        mbuf[:, os_] = (_sigmoid(ga) * a + _sigmoid(gb) * b).astype(_BF16)


def _mixer_out(w, mbuf, rows=slice(None)):
    return _dot(mbuf[rows, :], w[5][...])


def _mixer_residual(x, mixed, w):
    return x + _rms(mixed, w[10][...])


def _ffn_norm(x1, w):
    return _rms(x1, w[11][...]).astype(_BF16)


def _ffn_up(h2, w, hist, actbuf):
    wup, wfc = w[6], w[7]
    for c in range(D_FF // FF_CHUNK):
        conv = []
        for col in (c * FF_CHUNK, D_FF + c * FF_CHUNK):
            cols = slice(col, col + FF_CHUNK)
            hist.put_up(_dot(h2, wup[:, cols]), cols)
            y = wfc[CONV_W - 1:CONV_W, cols] * hist.get_up(0, cols)
            for k in range(1, CONV_W):
                y = y + wfc[CONV_W - 1 - k:CONV_W - k, cols] * hist.get_up(k, cols)
            conv.append(y)
        gate, val = conv
        actbuf[:, c * FF_CHUNK:(c + 1) * FF_CHUNK] = (gate * _sigmoid(gate) * val).astype(_BF16)


def _ffn_down(w, actbuf, rows=slice(None)):
    return _dot(actbuf[rows, :], w[8][...])


def _ffn_residual(x1, down, w):
    return x1 + _rms(down, w[12][...])


def _layer(x, w, hist, mbuf, actbuf, pool_div):
    _mixer(x, w, hist, mbuf, pool_div)
    x1 = _mixer_residual(x, _mixer_out(w, mbuf), w)
    _ffn_up(_ffn_norm(x1, w), w, hist, actbuf)
    return _ffn_residual(x1, _ffn_down(w, actbuf), w)


def _full_windows(s, win):
    return s * (1.0 / win)


class _RowHistory:
    def __init__(self, ubuf, cbuf, upbuf, m):
        self.ubuf, self.cbuf, self.upbuf, self.m = ubuf, cbuf, upbuf, m

    def put_u(self, val):
        self.ubuf[POOL_HALO:POOL_HALO + self.m, :] = val

    def get_u(self, k, cs):
        return self.ubuf[POOL_HALO - k:POOL_HALO - k + self.m, cs]

    def put_c(self, val):
        self.cbuf[CONV_HALO:CONV_HALO + self.m, :] = val

    def get_c(self, k):
        return self.cbuf[CONV_HALO - k:CONV_HALO - k + self.m, :]

    def put_up(self, val, cols):
        self.upbuf[CONV_HALO:CONV_HALO + self.m, cols] = val

    def get_up(self, k, cols):
        return self.upbuf[CONV_HALO - k:CONV_HALO - k + self.m, cols]


class _SegmentHistory:
    def __init__(self, ubuf, cbuf, upbuf, utail, ctail, ftail, tm):
        self.ubuf, self.cbuf, self.upbuf, self.tm = ubuf, cbuf, upbuf, tm
        self.utail, self.ctail, self.ftail = utail, ctail, ftail

    def _put(self, buf, tail, val, cols, depth):
        tm = self.tm
        buf[depth * SUBLANES:depth * SUBLANES + tm, cols] = val
        sublane = lax.broadcasted_iota(jnp.int32, (SUBLANES, val.shape[1]), 0)
        for j in range(depth):
            rows = slice(j * SUBLANES, (j + 1) * SUBLANES)
            cur = val[tm - (depth - j) * SUBLANES:tm - (depth - j - 1) * SUBLANES, :]
            mixed = jnp.where(sublane == SUBLANES - 1, tail[rows, cols], cur)
            buf[rows, cols] = pltpu.roll(mixed, 1, axis=0)
            tail[rows, cols] = cur

    def _get(self, buf, k, cols, depth):
        start = (depth - k) * SUBLANES
        return buf[start:start + self.tm, cols]

    def put_u(self, val):
        self._put(self.ubuf, self.utail, val, slice(None), POOL_BUF)

    def get_u(self, k, cs):
        return self._get(self.ubuf, k, cs, POOL_BUF)

    def put_c(self, val):
        self._put(self.cbuf, self.ctail, val, slice(None), CONV_W - 1)

    def get_c(self, k):
        return self._get(self.cbuf, k, slice(None), CONV_W - 1)

    def put_up(self, val, cols):
        self._put(self.upbuf, self.ftail, val, cols, CONV_W - 1)

    def get_up(self, k, cols):
        return self._get(self.upbuf, k, cols, CONV_W - 1)


class _StateHistory:
    def __init__(self, sp_ref, sc_ref, sf_ref, u_ref, c_ref, f_ref):
        self.sp, self.sc, self.sf = sp_ref, sc_ref, sf_ref
        self.u, self.c, self.f = u_ref, c_ref, f_ref

    def put_u(self, val):
        self.u[...] = val

    def get_u(self, k, cs):
        return self.u[:, cs] if k == 0 else self.sp[POOL_BUF - k, :, cs]

    def put_c(self, val):
        self.c[...] = val

    def get_c(self, k):
        return self.c[...] if k == 0 else self.sc[CONV_W - 1 - k]

    def put_up(self, val, cols):
        self.f[:, cols] = val

    def get_up(self, k, cols):
        return self.f[:, cols] if k == 0 else self.sf[CONV_W - 1 - k, :, cols]


def _fill_tail(tail, state_ref, first_row, depth):
    for j in range(depth):
        r = first_row + j
        tail[j * SUBLANES:(j + 1) * SUBLANES, :] = jnp.broadcast_to(
            state_ref[r:r + 1, :], (SUBLANES, tail.shape[1]))


def _read_tail(out_ref, tail, depth):
    for j in range(depth):
        out_ref[j:j + 1, :] = tail[(j + 1) * SUBLANES - 1:(j + 1) * SUBLANES, :]


def _cast_batches(pairs):
    jobs = []
    for src, dst in pairs:
        if len(src.shape) == 3:
            assert src.shape[1] == CAST_ROWS
            jobs += [(src.at[g], dst.at[g], src.shape[2]) for g in range(src.shape[0])]
        else:
            assert src.shape[0] % CAST_ROWS == 0
            jobs += [(src.at[pl.ds(r, CAST_ROWS), :], dst.at[pl.ds(r, CAST_ROWS), :], src.shape[1])
                     for r in range(0, src.shape[0], CAST_ROWS)]
    batches, cur, used = [], [], 0
    for s, d, cols in jobs:
        if used + cols > STAGE_COLS:
            batches.append(cur)
            cur, used = [], 0
        cur.append((s, d, used, cols))
        used += cols
    batches.append(cur)
    return batches


def _cast_weights(pairs, stage, sem):
    batches = _cast_batches(pairs)

    def copies(b, slot):
        return [pltpu.make_async_copy(s, stage.at[slot, :, pl.ds(lane, cols)], sem.at[slot])
                for s, _, lane, cols in batches[b]]

    for b in range(min(2, len(batches))):
        for cp in copies(b, b):
            cp.start()
    for b in range(len(batches)):
        slot = b % 2
        for cp in copies(b, slot):
            cp.wait()
        for _, d, lane, cols in batches[b]:
            d[...] = stage[slot, :, lane:lane + cols].astype(_BF16)
        if b + 2 < len(batches):
            for cp in copies(b + 2, slot):
                cp.start()


def _fused_kernel(x_hbm, xs_ref, sp_hbm, sc_hbm, sf_hbm, meta_ref,
                  w_in_hbm, w_pool_hbm, wco_hbm, wo_hbm, wup_hbm, wdown_hbm,
                  pscale, wconv, wfc, g1, g2, g3, g4,
                  y_hbm, ys_ref, op_ref, oc_ref, of_ref, us_ref, cs_ref, fs_ref,
                  w_in, w_pool, wco, wo, wup, wdown, stage, sp_v, sc_v, m_ubuf, m_cbuf, m_upbuf,
                  s_mbuf, s_actbuf, xbuf, ybuf, in_sem, out_sem, cast_sem, state_sem,
                  ubuf, cbuf, upbuf, utail, ctail, ftail, mbuf, actbuf, h2buf, x1_new, x1_old, down_buf,
                  *, tm, nt, total):
    w = (w_in, w_pool, pscale, wconv, wco, wo, wup, wfc, wdown, g1, g2, g3, g4)
    seg = tm // SUBLANES
    n = pl.program_id(0)
    slot = n % 2
    n_sample = xs_ref.shape[0]

    def x_copies(tile, sl):
        b, t0 = tile // nt, (tile % nt) * tm
        return [pltpu.make_async_copy(x_hbm.at[b, pl.ds(t0 + s * seg, seg), :],
                                      xbuf.at[sl, :, s, :], in_sem.at[sl]) for s in range(SUBLANES)]

    def y_copies(tile, sl):
        b, t0 = tile // nt, (tile % nt) * tm
        return [pltpu.make_async_copy(ybuf.at[sl, :, s, :],
                                      y_hbm.at[b, pl.ds(t0 + s * seg, seg), :], out_sem.at[sl])
                for s in range(SUBLANES)]

    @pl.when(n == 0)
    def _():
        for cp in x_copies(n, slot):
            cp.start()
        pool_cp = pltpu.make_async_copy(sp_hbm, sp_v, state_sem.at[0])
        conv_cp = pltpu.make_async_copy(sc_hbm, sc_v, state_sem.at[1])
        ffn_cp = pltpu.make_async_copy(sf_hbm, stage, state_sem.at[2])
        pool_cp.start()
        conv_cp.start()

        _cast_weights([(w_in_hbm, w_in), (w_pool_hbm, w_pool), (wco_hbm, wco), (wo_hbm, wo),
                       (wup_hbm, wup), (wdown_hbm, wdown)], stage, cast_sem)
        ffn_cp.start()

        m_ubuf[0:POOL_HALO, :] = jnp.zeros((POOL_HALO, POOL_WIDTH), _F32)
        m_cbuf[0:CONV_HALO, :] = jnp.zeros((CONV_HALO, CONV_WIDTH), _F32)
        m_upbuf[0:CONV_HALO, :] = jnp.zeros((CONV_HALO, 2 * D_FF), _F32)
        row1 = lax.broadcasted_iota(jnp.int32, (N_META, POOL_GC), 0) + 1

        def meta_windows(s, win):
            return s / jnp.minimum(win, row1).astype(_F32)

        meta_hist = _RowHistory(m_ubuf, m_cbuf, m_upbuf, N_META)
        meta_mbuf, meta_actbuf = s_mbuf.at[pl.ds(0, N_META)], s_actbuf.at[pl.ds(0, N_META)]
        _mixer(meta_ref[...], w, meta_hist, meta_mbuf, meta_windows)
        meta_x1 = _mixer_residual(meta_ref[...], _mixer_out(w, meta_mbuf), w)
        _ffn_up(_ffn_norm(meta_x1, w), w, meta_hist, meta_actbuf)

        pool_cp.wait()
        conv_cp.wait()
        ffn_cp.wait()
        sample_hist = _StateHistory(sp_v, sc_v, stage, us_ref, cs_ref, fs_ref)
        ys_ref[...] = _layer(xs_ref[...], w, sample_hist, s_mbuf, s_actbuf, _full_windows)

        h2buf[...] = jnp.zeros_like(h2buf)
        x1_new[...] = jnp.zeros_like(x1_new)
        x1_old[...] = jnp.zeros_like(x1_old)
        down_buf[...] = jnp.zeros_like(down_buf)
        ftail[...] = jnp.zeros_like(ftail)

    @pl.when(n + 1 < total)
    def _():
        for cp in x_copies(n + 1, 1 - slot):
            cp.start()

    @pl.when(n < total)
    def _():
        for cp in x_copies(n, slot):
            cp.wait()

    @pl.when(n >= 4)
    def _():
        for cp in y_copies(n - 4, slot):
            cp.wait()

    @pl.when(n % nt == 0)
    def _():
        _fill_tail(utail, m_ubuf, POOL_HALO + N_META - POOL_BUF, POOL_BUF)
        _fill_tail(ctail, m_cbuf, CONV_HALO + N_META - (CONV_W - 1), CONV_W - 1)

    @pl.when(n % nt == 1)
    def _():
        _fill_tail(ftail, m_upbuf, CONV_HALO + N_META - (CONV_W - 1), CONV_W - 1)

    hist = _SegmentHistory(ubuf, cbuf, upbuf, utail, ctail, ftail, tm)

    def finish_tile(sl):
        y = _ffn_residual(x1_old[...], down_buf[...], w)
        ybuf[sl] = y.reshape(seg, SUBLANES, D_MODEL)

    finish_tile(slot)
    x1_old[...] = x1_new[...]
    _ffn_up(h2buf[...], w, hist, actbuf)
    x = xbuf[slot].reshape(tm, D_MODEL)
    _mixer(x, w, hist, mbuf, _full_windows)
    x1 = _mixer_residual(x, _mixer_out(w, mbuf), w)
    x1_new[...] = x1
    h2buf[...] = _ffn_norm(x1, w)
    down_buf[...] = _ffn_down(w, actbuf)

    @pl.when(n >= 2)
    def _():
        for cp in y_copies(n - 2, slot):
            cp.start()

    @pl.when((n % nt == nt - 1) & (n < total))
    def _():
        _read_tail(op_ref, utail, POOL_BUF)
        _read_tail(oc_ref, ctail, CONV_W - 1)

    @pl.when((n % nt == 0) & (n >= 1))
    def _():
        _read_tail(of_ref, ftail, CONV_W - 1)

    @pl.when(n == total)
    def _():
        for cp in y_copies(n - 3, 1 - slot):
            cp.wait()
        finish_tile(1 - slot)
        for cp in y_copies(n - 1, 1 - slot):
            cp.start()
        for cp in y_copies(n - 2, slot):
            cp.wait()
        for cp in y_copies(n - 1, 1 - slot):
            cp.wait()


def _resident(shape):
    zeros = (0,) * len(shape)
    return pl.BlockSpec(shape, lambda *_: zeros, pipeline_mode=pl.Buffered(1))


def _fused_call(x, xs, sp, sc, sf, meta, big_weights, small_weights, tm):
    nb, t, d = x.shape
    ns = xs.shape[0]
    seg = tm // SUBLANES
    assert t % tm == 0 and tm % SUBLANES == 0 and seg >= POOL_BUF
    assert ns <= tm and sf.shape == (2, CAST_ROWS, STAGE_COLS)
    nt = t // tm
    assert nt >= 2 and nb * nt >= 4
    total = nb * nt
    mix_state = lambda rows, c: pl.BlockSpec(
        (None, rows, c), lambda n: (jnp.minimum(n, total - 1) // nt, 0, 0))
    ffn_state = lambda rows, c: pl.BlockSpec(
        (None, rows, c), lambda n: (jnp.maximum(n - 1, 0) // nt, 0, 0))
    hbm = pl.BlockSpec(memory_space=pl.ANY)
    rows_with_head = lambda depth: (depth + seg) * SUBLANES
    return pl.pallas_call(
        functools.partial(_fused_kernel, tm=tm, nt=nt, total=total),
        grid=(total + 1,),
        in_specs=[hbm, _resident(xs.shape), hbm, hbm, hbm, _resident(meta.shape)]
                 + [hbm] * len(big_weights) + [_resident(a.shape) for a in small_weights],
        out_specs=[hbm, _resident((ns, d)),
                   mix_state(POOL_BUF, POOL_WIDTH), mix_state(CONV_W - 1, CONV_WIDTH),
                   ffn_state(CONV_W - 1, 2 * D_FF),
                   _resident((ns, POOL_WIDTH)), _resident((ns, CONV_WIDTH)),
                   _resident((ns, 2 * D_FF))],
        out_shape=[jax.ShapeDtypeStruct((nb, t, d), _F32),
                   jax.ShapeDtypeStruct((ns, d), _F32),
                   jax.ShapeDtypeStruct((nb, POOL_BUF, POOL_WIDTH), _F32),
                   jax.ShapeDtypeStruct((nb, CONV_W - 1, CONV_WIDTH), _F32),
                   jax.ShapeDtypeStruct((nb, CONV_W - 1, 2 * D_FF), _F32),
                   jax.ShapeDtypeStruct((ns, POOL_WIDTH), _F32),
                   jax.ShapeDtypeStruct((ns, CONV_WIDTH), _F32),
                   jax.ShapeDtypeStruct((ns, 2 * D_FF), _F32)],
        scratch_shapes=[pltpu.VMEM((d, IN_TOTAL), _BF16),
                        pltpu.VMEM((N_POOL_GROUPS, POOL_GC, POOL_OUT_GC), _BF16),
                        pltpu.VMEM((CONV_WIDTH, d), _BF16),
                        pltpu.VMEM((d, d), _BF16),
                        pltpu.VMEM((d, 2 * D_FF), _BF16),
                        pltpu.VMEM((D_FF, d), _BF16),
                        pltpu.VMEM((2, CAST_ROWS, STAGE_COLS), _F32),
                        pltpu.VMEM(sp.shape, _F32),
                        pltpu.VMEM(sc.shape, _F32),
                        pltpu.VMEM((POOL_HALO + N_META, POOL_WIDTH), _F32),
                        pltpu.VMEM((CONV_HALO + N_META, CONV_WIDTH), _F32),
                        pltpu.VMEM((CONV_HALO + N_META, 2 * D_FF), _F32),
                        pltpu.VMEM((ns, d), _BF16),
                        pltpu.VMEM((ns, D_FF), _BF16),
                        pltpu.VMEM((2, seg, SUBLANES, d), _F32),
                        pltpu.VMEM((2, seg, SUBLANES, d), _F32),
                        pltpu.SemaphoreType.DMA((2,)),
                        pltpu.SemaphoreType.DMA((2,)),
                        pltpu.SemaphoreType.DMA((2,)),
                        pltpu.SemaphoreType.DMA((3,)),
                        pltpu.VMEM((rows_with_head(POOL_BUF), POOL_WIDTH), _F32),
                        pltpu.VMEM((rows_with_head(CONV_W - 1), CONV_WIDTH), _F32),
                        pltpu.VMEM((rows_with_head(CONV_W - 1), 2 * D_FF), _F32),
                        pltpu.VMEM((POOL_BUF * SUBLANES, POOL_WIDTH), _F32),
                        pltpu.VMEM(((CONV_W - 1) * SUBLANES, CONV_WIDTH), _F32),
                        pltpu.VMEM(((CONV_W - 1) * SUBLANES, 2 * D_FF), _F32),
                        pltpu.VMEM((tm, d), _BF16),
                        pltpu.VMEM((tm, D_FF), _BF16),
                        pltpu.VMEM((tm, d), _BF16),
                        pltpu.VMEM((tm, d), _F32),
                        pltpu.VMEM((tm, d), _F32),
                        pltpu.VMEM((tm, d), _F32)],
        compiler_params=pltpu.CompilerParams(
            dimension_semantics=("arbitrary",), vmem_limit_bytes=VMEM_LIMIT_BYTES),
        name="hybrid_step",
    )(x, xs, sp, sc, sf, meta, *big_weights, *small_weights)


def kernel(x_prompt, x_sample, state_pool, state_conv, state_ffn, meta_tokens, w_in, w_pool,
           pool_scale, w_conv, w_conv_out, w_o, w_up, w_ffn_conv, w_down, g_pre_mix,
           g_post_mix, g_pre_ffn, g_post_ffn):
    depth = w_in.shape[0]
    assert depth == 1, "the prompt / sample streams are chained for a single layer only"
    assert x_sample.shape[1] == 1 and meta_tokens.shape[0] == N_META
    assert N_META >= max(POOL_WINDOWS)
    row = lambda a: a.reshape(1, -1)
    l = 0
    big = (w_in[l], w_pool[l], w_conv_out[l], w_o[l], w_up[l], w_down[l])
    small = (row(pool_scale[l]), w_conv[l], w_ffn_conv[l], row(g_pre_mix[l]), row(g_post_mix[l]),
             row(g_pre_ffn[l]), row(g_post_ffn[l]))

    to_rows_first = lambda s: jnp.transpose(s, (1, 0, 2))
    y_prompt, ys, pp, pc, pf, us, cs, fs = _fused_call(
        x_prompt, x_sample[:, 0, :], to_rows_first(state_pool[l]), to_rows_first(state_conv[l]),
        to_rows_first(state_ffn[l]), meta_tokens.astype(_F32), big, small, TM_PROMPT)
    push = lambda state, new: jnp.concatenate([state[:, 1:], new[:, None, :]], axis=1)

    return (y_prompt, ys[:, None, :], pp[None], pc[None], pf[None],
            push(state_pool[l], us)[None], push(state_conv[l], cs)[None],
            push(state_ffn[l], fs)[None])
```

```python
import functools

import jax
import jax.numpy as jnp
from jax import lax
from jax.experimental import pallas as pl
from jax.experimental.pallas import tpu as pltpu

D_MODEL = 1024
N_META = 16
POOL_WINDOWS = (2, 4, 8, 16)
N_POOL_GROUPS = len(POOL_WINDOWS)
POOL_WIDTH = D_MODEL // 2
POOL_GC = POOL_WIDTH // N_POOL_GROUPS
POOL_OUT_GC = D_MODEL // N_POOL_GROUPS
POOL_BUF = max(POOL_WINDOWS) - 1
CONV_WIDTH = D_MODEL // 2
CONV_W = 3
D_FF = 2816
IN_TOTAL = 2 * POOL_WIDTH + 2 * CONV_WIDTH + 2 * D_MODEL
EPS = 1e-6
LOG2_E = 1.4426950408889634
OFF_U, OFF_V, OFF_B, OFF_C, OFF_GA, OFF_GB = 0, 512, 1024, 1536, 2048, 3072

SUBLANES = 8
FF_CHUNK = 256
POOL_HALO = 16
CONV_HALO = 8
TM_PROMPT = 256
CAST_ROWS = 128
STAGE_COLS = 2 * D_FF
VMEM_LIMIT_BYTES = 58 * 1024 * 1024

_BF16 = jnp.bfloat16
_F32 = jnp.float32


def _dot(a, b):
    return jnp.dot(a, b, preferred_element_type=_F32)


def _rms(x, g):
    ms = jnp.mean(x * x, axis=-1, keepdims=True)
    return x * lax.rsqrt(ms + EPS) * g


def _sigmoid(x):
    return 1.0 / (1.0 + jnp.exp2(x * (-LOG2_E)))


def _mixer(x, w, hist, mbuf, pool_div):
    (w_in, w_pool, pscale, wconv, wco, _, _, _, _, g1, _, _, _) = w
    h = _rms(x, g1[...]).astype(_BF16)

    hist.put_u(_dot(h, w_in[:, OFF_U:OFF_U + POOL_WIDTH]))
    v = _dot(h, w_in[:, OFF_V:OFF_V + CONV_WIDTH])
    cg = _dot(h, w_in[:, OFF_C:OFF_C + CONV_WIDTH])
    hist.put_c(cg * v)
    cv = wconv[CONV_W - 1:CONV_W, :] * hist.get_c(0)
    for k in range(1, CONV_W):
        cv = cv + wconv[CONV_W - 1 - k:CONV_W - k, :] * hist.get_c(k)
    bg = _dot(h, w_in[:, OFF_B:OFF_B + CONV_WIDTH])
    b_in = (bg * cv).astype(_BF16)

    for g, win in enumerate(POOL_WINDOWS):
        cs = slice(g * POOL_GC, (g + 1) * POOL_GC)
        os_ = slice(g * POOL_OUT_GC, (g + 1) * POOL_OUT_GC)
        ug = hist.get_u(0, cs)
        s = ug
        for k in range(1, win):
            s = s + hist.get_u(k, cs)
        pooled = (pool_div(s, win) - ug).astype(_BF16)
        a = _dot(pooled, w_pool[g]) * pscale[:, os_]
        ga = _dot(h, w_in[:, OFF_GA + g * POOL_OUT_GC:OFF_GA + (g + 1) * POOL_OUT_GC])
        gb = _dot(h, w_in[:, OFF_GB + g * POOL_OUT_GC:OFF_GB + (g + 1) * POOL_OUT_GC])
        b = _dot(b_in, wco[:, os_])
        mbuf[:, os_] = (_sigmoid(ga) * a + _sigmoid(gb) * b).astype(_BF16)


def _mixer_out(w, mbuf, rows=slice(None)):
    return _dot(mbuf[rows, :], w[5][...])


def _mixer_residual(x, mixed, w):
    return x + _rms(mixed, w[10][...])


def _ffn_norm(x1, w):
    return _rms(x1, w[11][...]).astype(_BF16)


def _ffn_up(h2, w, hist, actbuf):
    wup, wfc = w[6], w[7]
    for c in range(D_FF // FF_CHUNK):
        conv = []
        for col in (c * FF_CHUNK, D_FF + c * FF_CHUNK):
            cols = slice(col, col + FF_CHUNK)
            hist.put_up(_dot(h2, wup[:, cols]), cols)
            y = wfc[CONV_W - 1:CONV_W, cols] * hist.get_up(0, cols)
            for k in range(1, CONV_W):
                y = y + wfc[CONV_W - 1 - k:CONV_W - k, cols] * hist.get_up(k, cols)
            conv.append(y)
        gate, val = conv
        actbuf[:, c * FF_CHUNK:(c + 1) * FF_CHUNK] = (gate * _sigmoid(gate) * val).astype(_BF16)


def _ffn_down(w, actbuf, rows=slice(None)):
    return _dot(actbuf[rows, :], w[8][...])


def _ffn_residual(x1, down, w):
    return x1 + _rms(down, w[12][...])


def _layer(x, w, hist, mbuf, actbuf, pool_div):
    _mixer(x, w, hist, mbuf, pool_div)
    x1 = _mixer_residual(x, _mixer_out(w, mbuf), w)
    _ffn_up(_ffn_norm(x1, w), w, hist, actbuf)
    return _ffn_residual(x1, _ffn_down(w, actbuf), w)


def _full_windows(s, win):
    return s * (1.0 / win)


class _RowHistory:
    def __init__(self, ubuf, cbuf, upbuf, m):
        self.ubuf, self.cbuf, self.upbuf, self.m = ubuf, cbuf, upbuf, m

    def put_u(self, val):
        self.ubuf[POOL_HALO:POOL_HALO + self.m, :] = val

    def get_u(self, k, cs):
        return self.ubuf[POOL_HALO - k:POOL_HALO - k + self.m, cs]

    def put_c(self, val):
        self.cbuf[CONV_HALO:CONV_HALO + self.m, :] = val

    def get_c(self, k):
        return self.cbuf[CONV_HALO - k:CONV_HALO - k + self.m, :]

    def put_up(self, val, cols):
        self.upbuf[CONV_HALO:CONV_HALO + self.m, cols] = val

    def get_up(self, k, cols):
        return self.upbuf[CONV_HALO - k:CONV_HALO - k + self.m, cols]


class _SegmentHistory:
    def __init__(self, ubuf, cbuf, upbuf, utail, ctail, ftail, tm):
        self.ubuf, self.cbuf, self.upbuf, self.tm = ubuf, cbuf, upbuf, tm
        self.utail, self.ctail, self.ftail = utail, ctail, ftail

    def _put(self, buf, tail, val, cols, depth):
        tm = self.tm
        buf[depth * SUBLANES:depth * SUBLANES + tm, cols] = val
        sublane = lax.broadcasted_iota(jnp.int32, (SUBLANES, val.shape[1]), 0)
        for j in range(depth):
            rows = slice(j * SUBLANES, (j + 1) * SUBLANES)
            cur = val[tm - (depth - j) * SUBLANES:tm - (depth - j - 1) * SUBLANES, :]
            mixed = jnp.where(sublane == SUBLANES - 1, tail[rows, cols], cur)
            buf[rows, cols] = pltpu.roll(mixed, 1, axis=0)
            tail[rows, cols] = cur

    def _get(self, buf, k, cols, depth):
        start = (depth - k) * SUBLANES
        return buf[start:start + self.tm, cols]

    def put_u(self, val):
        self._put(self.ubuf, self.utail, val, slice(None), POOL_BUF)

    def get_u(self, k, cs):
        return self._get(self.ubuf, k, cs, POOL_BUF)

    def put_c(self, val):
        self._put(self.cbuf, self.ctail, val, slice(None), CONV_W - 1)

    def get_c(self, k):
        return self._get(self.cbuf, k, slice(None), CONV_W - 1)

    def put_up(self, val, cols):
        self._put(self.upbuf, self.ftail, val, cols, CONV_W - 1)

    def get_up(self, k, cols):
        return self._get(self.upbuf, k, cols, CONV_W - 1)


class _StateHistory:
    def __init__(self, sp_ref, sc_ref, sf_ref, u_ref, c_ref, f_ref):
        self.sp, self.sc, self.sf = sp_ref, sc_ref, sf_ref
        self.u, self.c, self.f = u_ref, c_ref, f_ref

    def put_u(self, val):
        self.u[...] = val

    def get_u(self, k, cs):
        return self.u[:, cs] if k == 0 else self.sp[POOL_BUF - k, :, cs]

    def put_c(self, val):
        self.c[...] = val

    def get_c(self, k):
        return self.c[...] if k == 0 else self.sc[CONV_W - 1 - k]

    def put_up(self, val, cols):
        self.f[:, cols] = val

    def get_up(self, k, cols):
        return self.f[:, cols] if k == 0 else self.sf[CONV_W - 1 - k, :, cols]


def _fill_tail(tail, state_ref, first_row, depth):
    for j in range(depth):
        r = first_row + j
        tail[j * SUBLANES:(j + 1) * SUBLANES, :] = jnp.broadcast_to(
            state_ref[r:r + 1, :], (SUBLANES, tail.shape[1]))


def _read_tail(out_ref, tail, depth):
    for j in range(depth):
        out_ref[j:j + 1, :] = tail[(j + 1) * SUBLANES - 1:(j + 1) * SUBLANES, :]


def _cast_batches(pairs):
    jobs = []
    for src, dst in pairs:
        if len(src.shape) == 3:
            assert src.shape[1] == CAST_ROWS
            jobs += [(src.at[g], dst.at[g], src.shape[2]) for g in range(src.shape[0])]
        else:
            assert src.shape[0] % CAST_ROWS == 0
            jobs += [(src.at[pl.ds(r, CAST_ROWS), :], dst.at[pl.ds(r, CAST_ROWS), :], src.shape[1])
                     for r in range(0, src.shape[0], CAST_ROWS)]
    batches, cur, used = [], [], 0
    for s, d, cols in jobs:
        if used + cols > STAGE_COLS:
            batches.append(cur)
            cur, used = [], 0
        cur.append((s, d, used, cols))
        used += cols
    batches.append(cur)
    return batches


def _cast_weights(pairs, stage, sem):
    batches = _cast_batches(pairs)

    def copies(b, slot):
        return [pltpu.make_async_copy(s, stage.at[slot, :, pl.ds(lane, cols)], sem.at[slot])
                for s, _, lane, cols in batches[b]]

    for b in range(min(2, len(batches))):
        for cp in copies(b, b):
            cp.start()
    for b in range(len(batches)):
        slot = b % 2
        for cp in copies(b, slot):
            cp.wait()
        for _, d, lane, cols in batches[b]:
            d[...] = stage[slot, :, lane:lane + cols].astype(_BF16)
        if b + 2 < len(batches):
            for cp in copies(b + 2, slot):
                cp.start()


def _fused_kernel(x_hbm, xs_hbm, sp_hbm, sc_hbm, sf_hbm, meta_ref,
                  w_in_hbm, w_pool_hbm, wco_hbm, wo_hbm, wup_hbm, wdown_hbm,
                  pscale, wconv, wfc, g1, g2, g3, g4,
                  y_hbm, ys_hbm, op_ref, oc_ref, of_ref, us_ref, cs_ref, fs_ref,
                  w_in, w_pool, wco, wo, wup, wdown, stage, sp_v, sc_v, m_ubuf, m_cbuf, m_upbuf,
                  s_mbuf, s_actbuf, xs_v, ys_v, xbuf, ybuf, in_sem, out_sem, cast_sem, state_sem,
                  ubuf, cbuf, upbuf, utail, ctail, ftail, mbuf, actbuf, h2buf, x1_new, x1_old,
                  *, tm, nt, total):
    w = (w_in, w_pool, pscale, wconv, wco, wo, wup, wfc, wdown, g1, g2, g3, g4)
    seg = tm // SUBLANES
    n = pl.program_id(0)
    slot = n % 2

    def x_copies(tile, sl):
        b, t0 = tile // nt, (tile % nt) * tm
        return [pltpu.make_async_copy(x_hbm.at[b, pl.ds(t0 + s * seg, seg), :],
                                      xbuf.at[sl, :, s, :], in_sem.at[sl]) for s in range(SUBLANES)]

    def y_copies(tile, sl):
        b, t0 = tile // nt, (tile % nt) * tm
        return [pltpu.make_async_copy(ybuf.at[sl, :, s, :],
                                      y_hbm.at[b, pl.ds(t0 + s * seg, seg), :], out_sem.at[sl])
                for s in range(SUBLANES)]

    @pl.when(n == 0)
    def _():
        for cp in x_copies(n, slot):
            cp.start()
        pool_cp = pltpu.make_async_copy(sp_hbm, sp_v, state_sem.at[0])
        conv_cp = pltpu.make_async_copy(sc_hbm, sc_v, state_sem.at[1])
        ffn_cp = pltpu.make_async_copy(sf_hbm, stage, state_sem.at[2])
        xs_cp = pltpu.make_async_copy(xs_hbm.at[:, 0, :], xs_v, state_sem.at[3])
        ys_cp = pltpu.make_async_copy(ys_v, ys_hbm.at[:, 0, :], state_sem.at[4])
        pool_cp.start()
        conv_cp.start()
        xs_cp.start()

        _cast_weights([(w_in_hbm, w_in), (w_pool_hbm, w_pool), (wco_hbm, wco), (wo_hbm, wo),
                       (wup_hbm, wup), (wdown_hbm, wdown)], stage, cast_sem)
        ffn_cp.start()

        m_ubuf[0:POOL_HALO, :] = jnp.zeros((POOL_HALO, POOL_WIDTH), _F32)
        m_cbuf[0:CONV_HALO, :] = jnp.zeros((CONV_HALO, CONV_WIDTH), _F32)
        m_upbuf[0:CONV_HALO, :] = jnp.zeros((CONV_HALO, 2 * D_FF), _F32)
        row1 = lax.broadcasted_iota(jnp.int32, (N_META, POOL_GC), 0) + 1

        def meta_windows(s, win):
            return s / jnp.minimum(win, row1).astype(_F32)

        meta_hist = _RowHistory(m_ubuf, m_cbuf, m_upbuf, N_META)
        meta_mbuf, meta_actbuf = s_mbuf.at[pl.ds(0, N_META)], s_actbuf.at[pl.ds(0, N_META)]
        _mixer(meta_ref[...], w, meta_hist, meta_mbuf, meta_windows)
        meta_x1 = _mixer_residual(meta_ref[...], _mixer_out(w, meta_mbuf), w)
        _ffn_up(_ffn_norm(meta_x1, w), w, meta_hist, meta_actbuf)

        pool_cp.wait()
        conv_cp.wait()
        ffn_cp.wait()
        xs_cp.wait()
        sample_hist = _StateHistory(sp_v, sc_v, stage, us_ref, cs_ref, fs_ref)
        ys_v[...] = _layer(xs_v[...], w, sample_hist, s_mbuf, s_actbuf, _full_windows)
        ys_cp.start()
        ys_cp.wait()

        h2buf[...] = jnp.zeros_like(h2buf)
        x1_new[...] = jnp.zeros_like(x1_new)
        ftail[...] = jnp.zeros_like(ftail)

    @pl.when(n + 1 < total)
    def _():
        for cp in x_copies(n + 1, 1 - slot):
            cp.start()

    @pl.when(n < total)
    def _():
        for cp in x_copies(n, slot):
            cp.wait()

    @pl.when(n >= 3)
    def _():
        for cp in y_copies(n - 3, 1 - slot):
            cp.wait()

    @pl.when(n % nt == 0)
    def _():
        _fill_tail(utail, m_ubuf, POOL_HALO + N_META - POOL_BUF, POOL_BUF)
        _fill_tail(ctail, m_cbuf, CONV_HALO + N_META - (CONV_W - 1), CONV_W - 1)

    @pl.when(n % nt == 1)
    def _():
        _fill_tail(ftail, m_upbuf, CONV_HALO + N_META - (CONV_W - 1), CONV_W - 1)

    hist = _SegmentHistory(ubuf, cbuf, upbuf, utail, ctail, ftail, tm)

    x1_old[...] = x1_new[...]
    _ffn_up(h2buf[...], w, hist, actbuf)
    x = xbuf[slot].reshape(tm, D_MODEL)
    _mixer(x, w, hist, mbuf, _full_windows)
    x1 = _mixer_residual(x, _mixer_out(w, mbuf), w)
    x1_new[...] = x1
    h2buf[...] = _ffn_norm(x1, w)
    y = _ffn_residual(x1_old[...], _ffn_down(w, actbuf), w)
    ybuf[1 - slot] = y.reshape(seg, SUBLANES, D_MODEL)

    @pl.when(n >= 1)
    def _():
        for cp in y_copies(n - 1, 1 - slot):
            cp.start()

    @pl.when((n % nt == nt - 1) & (n < total))
    def _():
        _read_tail(op_ref, utail, POOL_BUF)
        _read_tail(oc_ref, ctail, CONV_W - 1)

    @pl.when((n % nt == 0) & (n >= 1))
    def _():
        _read_tail(of_ref, ftail, CONV_W - 1)

    @pl.when(n == total)
    def _():
        for cp in y_copies(n - 2, slot):
            cp.wait()
        for cp in y_copies(n - 1, 1 - slot):
            cp.wait()


def _resident(shape):
    zeros = (0,) * len(shape)
    return pl.BlockSpec(shape, lambda *_: zeros, pipeline_mode=pl.Buffered(1))


def _layer_slice(shape, l):
    return pl.BlockSpec((None,) + tuple(shape[1:]), lambda *_: (l, 0, 0), pipeline_mode=pl.Buffered(1))


def _fused_call(x, xs, sp, sc, sf, meta, big_weights, pscale, w_conv, w_ffn_conv, gains, l, tm):
    nb, t, d = x.shape
    ns = xs.shape[0]
    seg = tm // SUBLANES
    assert t % tm == 0 and tm % SUBLANES == 0 and seg >= POOL_BUF
    assert xs.shape == (ns, 1, d) and sf.shape == (2, CAST_ROWS, STAGE_COLS) and ns == CAST_ROWS
    nt = t // tm
    assert nt >= 2
    total = nb * nt
    mix_state = lambda rows, c: pl.BlockSpec(
        (None, None, rows, c), lambda n: (0, jnp.minimum(n, total - 1) // nt, 0, 0))
    ffn_state = lambda rows, c: pl.BlockSpec(
        (None, None, rows, c), lambda n: (0, jnp.maximum(n - 1, 0) // nt, 0, 0))
    hbm = pl.BlockSpec(memory_space=pl.ANY)
    rows_with_head = lambda depth: (depth + seg) * SUBLANES
    return pl.pallas_call(
        functools.partial(_fused_kernel, tm=tm, nt=nt, total=total),
        grid=(total + 1,),
        in_specs=[hbm, hbm, hbm, hbm, hbm, _resident(meta.shape)]
                 + [hbm] * len(big_weights)
                 + [_resident(pscale.shape), _layer_slice(w_conv.shape, l),
                    _layer_slice(w_ffn_conv.shape, l)]
                 + [_resident(g.shape) for g in gains],
        out_specs=[hbm, hbm,
                   mix_state(POOL_BUF, POOL_WIDTH), mix_state(CONV_W - 1, CONV_WIDTH),
                   ffn_state(CONV_W - 1, 2 * D_FF),
                   _resident((ns, POOL_WIDTH)), _resident((ns, CONV_WIDTH)),
                   _resident((ns, 2 * D_FF))],
        out_shape=[jax.ShapeDtypeStruct((nb, t, d), _F32),
                   jax.ShapeDtypeStruct((ns, 1, d), _F32),
                   jax.ShapeDtypeStruct((1, nb, POOL_BUF, POOL_WIDTH), _F32),
                   jax.ShapeDtypeStruct((1, nb, CONV_W - 1, CONV_WIDTH), _F32),
                   jax.ShapeDtypeStruct((1, nb, CONV_W - 1, 2 * D_FF), _F32),
                   jax.ShapeDtypeStruct((ns, POOL_WIDTH), _F32),
                   jax.ShapeDtypeStruct((ns, CONV_WIDTH), _F32),
                   jax.ShapeDtypeStruct((ns, 2 * D_FF), _F32)],
        scratch_shapes=[pltpu.VMEM((d, IN_TOTAL), _BF16),
                        pltpu.VMEM((N_POOL_GROUPS, POOL_GC, POOL_OUT_GC), _BF16),
                        pltpu.VMEM((CONV_WIDTH, d), _BF16),
                        pltpu.VMEM((d, d), _BF16),
                        pltpu.VMEM((d, 2 * D_FF), _BF16),
                        pltpu.VMEM((D_FF, d), _BF16),
                        pltpu.VMEM((2, CAST_ROWS, STAGE_COLS), _F32),
                        pltpu.VMEM(sp.shape, _F32),
                        pltpu.VMEM(sc.shape, _F32),
                        pltpu.VMEM((POOL_HALO + N_META, POOL_WIDTH), _F32),
                        pltpu.VMEM((CONV_HALO + N_META, CONV_WIDTH), _F32),
                        pltpu.VMEM((CONV_HALO + N_META, 2 * D_FF), _F32),
                        pltpu.VMEM((ns, d), _BF16),
                        pltpu.VMEM((ns, D_FF), _BF16),
                        pltpu.VMEM((ns, d), _F32),
                        pltpu.VMEM((ns, d), _F32),
                        pltpu.VMEM((2, seg, SUBLANES, d), _F32),
                        pltpu.VMEM((2, seg, SUBLANES, d), _F32),
                        pltpu.SemaphoreType.DMA((2,)),
                        pltpu.SemaphoreType.DMA((2,)),
                        pltpu.SemaphoreType.DMA((2,)),
                        pltpu.SemaphoreType.DMA((5,)),
                        pltpu.VMEM((rows_with_head(POOL_BUF), POOL_WIDTH), _F32),
                        pltpu.VMEM((rows_with_head(CONV_W - 1), CONV_WIDTH), _F32),
                        pltpu.VMEM((rows_with_head(CONV_W - 1), 2 * D_FF), _F32),
                        pltpu.VMEM((POOL_BUF * SUBLANES, POOL_WIDTH), _F32),
                        pltpu.VMEM(((CONV_W - 1) * SUBLANES, CONV_WIDTH), _F32),
                        pltpu.VMEM(((CONV_W - 1) * SUBLANES, 2 * D_FF), _F32),
                        pltpu.VMEM((tm, d), _BF16),
                        pltpu.VMEM((tm, D_FF), _BF16),
                        pltpu.VMEM((tm, d), _BF16),
                        pltpu.VMEM((tm, d), _F32),
                        pltpu.VMEM((tm, d), _F32)],
        compiler_params=pltpu.CompilerParams(
            dimension_semantics=("arbitrary",), vmem_limit_bytes=VMEM_LIMIT_BYTES),
        name="hybrid_step",
    )(x, xs, sp, sc, sf, meta, *big_weights, pscale, w_conv, w_ffn_conv, *gains)


def kernel(x_prompt, x_sample, state_pool, state_conv, state_ffn, meta_tokens, w_in, w_pool,
           pool_scale, w_conv, w_conv_out, w_o, w_up, w_ffn_conv, w_down, g_pre_mix,
           g_post_mix, g_pre_ffn, g_post_ffn):
    depth = w_in.shape[0]
    assert depth == 1, "the prompt / sample streams are chained for a single layer only"
    assert x_sample.shape[1] == 1 and meta_tokens.shape[0] == N_META
    assert N_META >= max(POOL_WINDOWS)
    row = lambda a: a.reshape(1, -1)
    l = 0
    big = (w_in[l], w_pool[l], w_conv_out[l], w_o[l], w_up[l], w_down[l])
    gains = (row(g_pre_mix[l]), row(g_post_mix[l]), row(g_pre_ffn[l]), row(g_post_ffn[l]))

    to_rows_first = lambda s: jnp.transpose(s, (1, 0, 2))
    y_prompt, y_sample, pp, pc, pf, us, cs, fs = _fused_call(
        x_prompt, x_sample, to_rows_first(state_pool[l]), to_rows_first(state_conv[l]),
        to_rows_first(state_ffn[l]), meta_tokens.astype(_F32), big, row(pool_scale[l]), w_conv,
        w_ffn_conv, gains, l, TM_PROMPT)
    push = lambda state, new: jnp.concatenate([state[:, 1:], new[:, None, :]], axis=1)

    return (y_prompt, y_sample, pp, pc, pf,
            push(state_pool[l], us)[None], push(state_conv[l], cs)[None],
            push(state_ffn[l], fs)[None])
```

```python
import functools

import jax
import jax.numpy as jnp
from jax import lax
from jax.experimental import pallas as pl
from jax.experimental.pallas import tpu as pltpu

D_MODEL = 1024
N_META = 16
POOL_WINDOWS = (2, 4, 8, 16)
N_POOL_GROUPS = len(POOL_WINDOWS)
POOL_WIDTH = D_MODEL // 2
POOL_GC = POOL_WIDTH // N_POOL_GROUPS
POOL_OUT_GC = D_MODEL // N_POOL_GROUPS
POOL_BUF = max(POOL_WINDOWS) - 1
CONV_WIDTH = D_MODEL // 2
CONV_W = 3
D_FF = 2816
IN_TOTAL = 2 * POOL_WIDTH + 2 * CONV_WIDTH + 2 * D_MODEL
EPS = 1e-6
LOG2_E = 1.4426950408889634
OFF_U, OFF_V, OFF_B, OFF_C, OFF_GA, OFF_GB = 0, 512, 1024, 1536, 2048, 3072

SUBLANES = 8
FF_CHUNK = 256
POOL_HALO = 16
CONV_HALO = 8
TM_PROMPT = 256
CAST_ROWS = 128
STAGE_COLS = 2 * D_FF
VMEM_LIMIT_BYTES = 58 * 1024 * 1024

_BF16 = jnp.bfloat16
_F32 = jnp.float32


def _dot(a, b):
    return jnp.dot(a, b, preferred_element_type=_F32)


def _rms(x, g):
    ms = jnp.mean(x * x, axis=-1, keepdims=True)
    return x * lax.rsqrt(ms + EPS) * g


def _sigmoid(x):
    return 1.0 / (1.0 + jnp.exp2(x * (-LOG2_E)))


def _mixer(x, w, hist, mbuf, pool_div):
    (w_in, w_pool, pscale, wconv, wco, _, _, _, _, g1, _, _, _) = w
    h = _rms(x, g1[...]).astype(_BF16)

    hist.put_u(_dot(h, w_in[:, OFF_U:OFF_U + POOL_WIDTH]))
    v = _dot(h, w_in[:, OFF_V:OFF_V + CONV_WIDTH])
    cg = _dot(h, w_in[:, OFF_C:OFF_C + CONV_WIDTH])
    hist.put_c(cg * v)
    cv = wconv[CONV_W - 1:CONV_W, :] * hist.get_c(0)
    for k in range(1, CONV_W):
        cv = cv + wconv[CONV_W - 1 - k:CONV_W - k, :] * hist.get_c(k)
    bg = _dot(h, w_in[:, OFF_B:OFF_B + CONV_WIDTH])
    b_in = (bg * cv).astype(_BF16)

    for g, win in enumerate(POOL_WINDOWS):
        cs = slice(g * POOL_GC, (g + 1) * POOL_GC)
        os_ = slice(g * POOL_OUT_GC, (g + 1) * POOL_OUT_GC)
        ug = hist.get_u(0, cs)
        s = ug
        for k in range(1, win):
            s = s + hist.get_u(k, cs)
        pooled = (pool_div(s, win) - ug).astype(_BF16)
        a = _dot(pooled, w_pool[g]) * pscale[:, os_]
        ga = _dot(h, w_in[:, OFF_GA + g * POOL_OUT_GC:OFF_GA + (g + 1) * POOL_OUT_GC])
        gb = _dot(h, w_in[:, OFF_GB + g * POOL_OUT_GC:OFF_GB + (g + 1) * POOL_OUT_GC])
        b = _dot(b_in, wco[:, os_])
        mbuf[:, os_] = (_sigmoid(ga) * a + _sigmoid(gb) * b).astype(_BF16)


def _mixer_out(w, mbuf, rows=slice(None)):
    return _dot(mbuf[rows, :], w[5][...])


def _mixer_residual(x, mixed, w):
    return x + _rms(mixed, w[10][...])


def _ffn_norm(x1, w):
    return _rms(x1, w[11][...]).astype(_BF16)


def _ffn_up(h2, w, hist, actbuf):
    wup, wfc = w[6], w[7]
    for c in range(D_FF // FF_CHUNK):
        conv = []
        for col in (c * FF_CHUNK, D_FF + c * FF_CHUNK):
            cols = slice(col, col + FF_CHUNK)
            hist.put_up(_dot(h2, wup[:, cols]), cols)
            y = wfc[CONV_W - 1:CONV_W, cols] * hist.get_up(0, cols)
            for k in range(1, CONV_W):
                y = y + wfc[CONV_W - 1 - k:CONV_W - k, cols] * hist.get_up(k, cols)
            conv.append(y)
        gate, val = conv
        actbuf[:, c * FF_CHUNK:(c + 1) * FF_CHUNK] = (gate * _sigmoid(gate) * val).astype(_BF16)


def _ffn_down(w, actbuf, rows=slice(None)):
    return _dot(actbuf[rows, :], w[8][...])


def _ffn_residual(x1, down, w):
    return x1 + _rms(down, w[12][...])


def _layer(x, w, hist, mbuf, actbuf, pool_div):
    _mixer(x, w, hist, mbuf, pool_div)
    x1 = _mixer_residual(x, _mixer_out(w, mbuf), w)
    _ffn_up(_ffn_norm(x1, w), w, hist, actbuf)
    return _ffn_residual(x1, _ffn_down(w, actbuf), w)


def _full_windows(s, win):
    return s * (1.0 / win)


class _RowHistory:
    def __init__(self, ubuf, cbuf, upbuf, m):
        self.ubuf, self.cbuf, self.upbuf, self.m = ubuf, cbuf, upbuf, m

    def put_u(self, val):
        self.ubuf[POOL_HALO:POOL_HALO + self.m, :] = val

    def get_u(self, k, cs):
        return self.ubuf[POOL_HALO - k:POOL_HALO - k + self.m, cs]

    def put_c(self, val):
        self.cbuf[CONV_HALO:CONV_HALO + self.m, :] = val

    def get_c(self, k):
        return self.cbuf[CONV_HALO - k:CONV_HALO - k + self.m, :]

    def put_up(self, val, cols):
        self.upbuf[CONV_HALO:CONV_HALO + self.m, cols] = val

    def get_up(self, k, cols):
        return self.upbuf[CONV_HALO - k:CONV_HALO - k + self.m, cols]


class _SegmentHistory:
    def __init__(self, ubuf, cbuf, upbuf, utail, ctail, ftail, tm):
        self.ubuf, self.cbuf, self.upbuf, self.tm = ubuf, cbuf, upbuf, tm
        self.utail, self.ctail, self.ftail = utail, ctail, ftail

    def _put(self, buf, tail, val, cols, depth):
        tm = self.tm
        buf[depth * SUBLANES:depth * SUBLANES + tm, cols] = val
        sublane = lax.broadcasted_iota(jnp.int32, (SUBLANES, val.shape[1]), 0)
        for j in range(depth):
            rows = slice(j * SUBLANES, (j + 1) * SUBLANES)
            cur = val[tm - (depth - j) * SUBLANES:tm - (depth - j - 1) * SUBLANES, :]
            mixed = jnp.where(sublane == SUBLANES - 1, tail[rows, cols], cur)
            buf[rows, cols] = pltpu.roll(mixed, 1, axis=0)
            tail[rows, cols] = cur

    def _get(self, buf, k, cols, depth):
        start = (depth - k) * SUBLANES
        return buf[start:start + self.tm, cols]

    def put_u(self, val):
        self._put(self.ubuf, self.utail, val, slice(None), POOL_BUF)

    def get_u(self, k, cs):
        return self._get(self.ubuf, k, cs, POOL_BUF)

    def put_c(self, val):
        self._put(self.cbuf, self.ctail, val, slice(None), CONV_W - 1)

    def get_c(self, k):
        return self._get(self.cbuf, k, slice(None), CONV_W - 1)

    def put_up(self, val, cols):
        self._put(self.upbuf, self.ftail, val, cols, CONV_W - 1)

    def get_up(self, k, cols):
        return self._get(self.upbuf, k, cols, CONV_W - 1)


class _StateHistory:
    def __init__(self, sp_ref, sc_ref, sf_ref, u_ref, c_ref, f_ref):
        self.sp, self.sc, self.sf = sp_ref, sc_ref, sf_ref
        self.u, self.c, self.f = u_ref, c_ref, f_ref

    def put_u(self, val):
        self.u[...] = val

    def get_u(self, k, cs):
        return self.u[:, cs] if k == 0 else self.sp[POOL_BUF - k, :, cs]

    def put_c(self, val):
        self.c[...] = val

    def get_c(self, k):
        return self.c[...] if k == 0 else self.sc[CONV_W - 1 - k]

    def put_up(self, val, cols):
        self.f[:, cols] = val

    def get_up(self, k, cols):
        return self.f[:, cols] if k == 0 else self.sf[CONV_W - 1 - k, :, cols]


def _fill_tail(tail, state_ref, first_row, depth):
    for j in range(depth):
        r = first_row + j
        tail[j * SUBLANES:(j + 1) * SUBLANES, :] = jnp.broadcast_to(
            state_ref[r:r + 1, :], (SUBLANES, tail.shape[1]))


def _read_tail(out_ref, tail, depth):
    for j in range(depth):
        out_ref[j:j + 1, :] = tail[(j + 1) * SUBLANES - 1:(j + 1) * SUBLANES, :]


def _cast_batches(pairs):
    jobs = []
    for src, dst in pairs:
        if len(src.shape) == 3:
            assert src.shape[1] == CAST_ROWS
            jobs += [(src.at[g], dst.at[g], src.shape[2]) for g in range(src.shape[0])]
        else:
            assert src.shape[0] % CAST_ROWS == 0
            jobs += [(src.at[pl.ds(r, CAST_ROWS), :], dst.at[pl.ds(r, CAST_ROWS), :], src.shape[1])
                     for r in range(0, src.shape[0], CAST_ROWS)]
    batches, cur, used = [], [], 0
    for s, d, cols in jobs:
        if used + cols > STAGE_COLS:
            batches.append(cur)
            cur, used = [], 0
        cur.append((s, d, used, cols))
        used += cols
    batches.append(cur)
    return batches


def _cast_weights(pairs, stage, sem):
    batches = _cast_batches(pairs)

    def copies(b, slot):
        return [pltpu.make_async_copy(s, stage.at[slot, :, pl.ds(lane, cols)], sem.at[slot])
                for s, _, lane, cols in batches[b]]

    for b in range(min(2, len(batches))):
        for cp in copies(b, b):
            cp.start()
    for b in range(len(batches)):
        slot = b % 2
        for cp in copies(b, slot):
            cp.wait()
        for _, d, lane, cols in batches[b]:
            d[...] = stage[slot, :, lane:lane + cols].astype(_BF16)
        if b + 2 < len(batches):
            for cp in copies(b + 2, slot):
                cp.start()


def _fused_kernel(x_hbm, xs_hbm, sp_hbm, sc_hbm, sf_hbm, meta_ref,
                  w_in_hbm, w_pool_hbm, wco_hbm, wo_hbm, wup_hbm, wdown_hbm,
                  pscale, wconv, wfc, g1, g2, g3, g4,
                  y_hbm, ys_hbm, op_ref, oc_ref, of_ref, us_ref, cs_ref, fs_ref,
                  w_in, w_pool, wco, wo, wup, wdown, stage, sp_v, sc_v, m_ubuf, m_cbuf, m_upbuf,
                  s_mbuf, s_actbuf, xs_v, ys_v, xbuf, ybuf, in_sem, out_sem, cast_sem, state_sem,
                  ubuf, cbuf, upbuf, utail, ctail, ftail, mbuf, actbuf, h2buf, x1_new, x1_old,
                  ubuf_first, cbuf_first, mbuf_first, upbuf_last, actbuf_last,
                  *, tm, nt, total):
    w = (w_in, w_pool, pscale, wconv, wco, wo, wup, wfc, wdown, g1, g2, g3, g4)
    seg = tm // SUBLANES
    n = pl.program_id(0)
    slot = n % 2

    def x_copies(tile, sl):
        b, t0 = tile // nt, (tile % nt) * tm
        return [pltpu.make_async_copy(x_hbm.at[b, pl.ds(t0 + s * seg, seg), :],
                                      xbuf.at[sl, :, s, :], in_sem.at[sl]) for s in range(SUBLANES)]

    def y_copies(tile, sl):
        b, t0 = tile // nt, (tile % nt) * tm
        return [pltpu.make_async_copy(ybuf.at[sl, :, s, :],
                                      y_hbm.at[b, pl.ds(t0 + s * seg, seg), :], out_sem.at[sl])
                for s in range(SUBLANES)]

    @pl.when(n == 0)
    def _():
        for cp in x_copies(n, slot):
            cp.start()
        pool_cp = pltpu.make_async_copy(sp_hbm, sp_v, state_sem.at[0])
        conv_cp = pltpu.make_async_copy(sc_hbm, sc_v, state_sem.at[1])
        ffn_cp = pltpu.make_async_copy(sf_hbm, stage, state_sem.at[2])
        xs_cp = pltpu.make_async_copy(xs_hbm.at[:, 0, :], xs_v, state_sem.at[3])
        ys_cp = pltpu.make_async_copy(ys_v, ys_hbm.at[:, 0, :], state_sem.at[4])
        pool_cp.start()
        conv_cp.start()
        xs_cp.start()

        _cast_weights([(w_in_hbm, w_in), (w_pool_hbm, w_pool), (wco_hbm, wco), (wo_hbm, wo),
                       (wup_hbm, wup), (wdown_hbm, wdown)], stage, cast_sem)
        ffn_cp.start()

        m_ubuf[0:POOL_HALO, :] = jnp.zeros((POOL_HALO, POOL_WIDTH), _F32)
        m_cbuf[0:CONV_HALO, :] = jnp.zeros((CONV_HALO, CONV_WIDTH), _F32)
        m_upbuf[0:CONV_HALO, :] = jnp.zeros((CONV_HALO, 2 * D_FF), _F32)
        row1 = lax.broadcasted_iota(jnp.int32, (N_META, POOL_GC), 0) + 1

        def meta_windows(s, win):
            return s / jnp.minimum(win, row1).astype(_F32)

        meta_hist = _RowHistory(m_ubuf, m_cbuf, m_upbuf, N_META)
        meta_mbuf, meta_actbuf = s_mbuf.at[pl.ds(0, N_META)], s_actbuf.at[pl.ds(0, N_META)]
        _mixer(meta_ref[...], w, meta_hist, meta_mbuf, meta_windows)
        meta_x1 = _mixer_residual(meta_ref[...], _mixer_out(w, meta_mbuf), w)
        _ffn_up(_ffn_norm(meta_x1, w), w, meta_hist, meta_actbuf)

        pool_cp.wait()
        conv_cp.wait()
        ffn_cp.wait()
        xs_cp.wait()
        sample_hist = _StateHistory(sp_v, sc_v, stage, us_ref, cs_ref, fs_ref)
        ys_v[...] = _layer(xs_v[...], w, sample_hist, s_mbuf, s_actbuf, _full_windows)
        ys_cp.start()
        ys_cp.wait()

    @pl.when(n + 1 < total)
    def _():
        for cp in x_copies(n + 1, 1 - slot):
            cp.start()

    def wait_tile(buf, sem, sl):
        pltpu.make_async_copy(buf.at[sl], buf.at[sl], sem.at[sl]).wait()

    @pl.when(n < total)
    def _():
        wait_tile(xbuf, in_sem, slot)

    @pl.when(n >= 3)
    def _():
        wait_tile(ybuf, out_sem, 1 - slot)

    @pl.when(n % nt == 0)
    def _():
        _fill_tail(utail, m_ubuf, POOL_HALO + N_META - POOL_BUF, POOL_BUF)
        _fill_tail(ctail, m_cbuf, CONV_HALO + N_META - (CONV_W - 1), CONV_W - 1)

    @pl.when(n % nt == 1)
    def _():
        _fill_tail(ftail, m_upbuf, CONV_HALO + N_META - (CONV_W - 1), CONV_W - 1)

    def token_mixer(work):
        ub, cb, mb = work
        hist = _SegmentHistory(ub, cb, None, utail, ctail, None, tm)
        x = xbuf[slot].reshape(tm, D_MODEL)
        _mixer(x, w, hist, mb, _full_windows)
        x1 = _mixer_residual(x, _mixer_out(w, mb), w)
        x1_new[...] = x1
        h2buf[...] = _ffn_norm(x1, w)

    def channel_mixer_up(work):
        upb, ab = work
        _ffn_up(h2buf[...], w, _SegmentHistory(None, None, upb, None, None, ftail, tm), ab)

    def channel_mixer_out(work, x1_ref):
        y = _ffn_residual(x1_ref[...], _ffn_down(w, work[1]), w)
        ybuf[1 - slot] = y.reshape(seg, SUBLANES, D_MODEL)

    @pl.when(n == 0)
    def _():
        token_mixer((ubuf_first, cbuf_first, mbuf_first))

    @pl.when((n > 0) & (n < total))
    def _():
        x1_old[...] = x1_new[...]
        channel_mixer_up((upbuf, actbuf))
        token_mixer((ubuf, cbuf, mbuf))
        channel_mixer_out((upbuf, actbuf), x1_old)

    @pl.when(n == total)
    def _():
        channel_mixer_up((upbuf_last, actbuf_last))
        channel_mixer_out((upbuf_last, actbuf_last), x1_new)

    @pl.when(n >= 1)
    def _():
        for cp in y_copies(n - 1, 1 - slot):
            cp.start()

    @pl.when((n % nt == nt - 1) & (n < total))
    def _():
        _read_tail(op_ref, utail, POOL_BUF)
        _read_tail(oc_ref, ctail, CONV_W - 1)

    @pl.when((n % nt == 0) & (n >= 1))
    def _():
        _read_tail(of_ref, ftail, CONV_W - 1)

    @pl.when(n == total)
    def _():
        wait_tile(ybuf, out_sem, slot)
        wait_tile(ybuf, out_sem, 1 - slot)


def _resident(shape):
    zeros = (0,) * len(shape)
    return pl.BlockSpec(shape, lambda *_: zeros, pipeline_mode=pl.Buffered(1))


def _layer_slice(shape, l):
    return pl.BlockSpec((None,) + tuple(shape[1:]), lambda *_: (l, 0, 0), pipeline_mode=pl.Buffered(1))


def _fused_call(x, xs, sp, sc, sf, meta, big_weights, pscale, w_conv, w_ffn_conv, gains, l, tm):
    nb, t, d = x.shape
    ns = xs.shape[0]
    seg = tm // SUBLANES
    assert t % tm == 0 and tm % SUBLANES == 0 and seg >= POOL_BUF
    assert xs.shape == (ns, 1, d) and sf.shape == (2, CAST_ROWS, STAGE_COLS) and ns == CAST_ROWS
    nt = t // tm
    assert nt >= 2
    total = nb * nt
    mix_state = lambda rows, c: pl.BlockSpec(
        (None, None, rows, c), lambda n: (0, jnp.minimum(n, total - 1) // nt, 0, 0))
    ffn_state = lambda rows, c: pl.BlockSpec(
        (None, None, rows, c), lambda n: (0, jnp.maximum(n - 1, 0) // nt, 0, 0))
    hbm = pl.BlockSpec(memory_space=pl.ANY)
    rows_with_head = lambda depth: (depth + seg) * SUBLANES
    return pl.pallas_call(
        functools.partial(_fused_kernel, tm=tm, nt=nt, total=total),
        grid=(total + 1,),
        in_specs=[hbm, hbm, hbm, hbm, hbm, _resident(meta.shape)]
                 + [hbm] * len(big_weights)
                 + [_resident(pscale.shape), _layer_slice(w_conv.shape, l),
                    _layer_slice(w_ffn_conv.shape, l)]
                 + [_resident(g.shape) for g in gains],
        out_specs=[hbm, hbm,
                   mix_state(POOL_BUF, POOL_WIDTH), mix_state(CONV_W - 1, CONV_WIDTH),
                   ffn_state(CONV_W - 1, 2 * D_FF),
                   _resident((ns, POOL_WIDTH)), _resident((ns, CONV_WIDTH)),
                   _resident((ns, 2 * D_FF))],
        out_shape=[jax.ShapeDtypeStruct((nb, t, d), _F32),
                   jax.ShapeDtypeStruct((ns, 1, d), _F32),
                   jax.ShapeDtypeStruct((1, nb, POOL_BUF, POOL_WIDTH), _F32),
                   jax.ShapeDtypeStruct((1, nb, CONV_W - 1, CONV_WIDTH), _F32),
                   jax.ShapeDtypeStruct((1, nb, CONV_W - 1, 2 * D_FF), _F32),
                   jax.ShapeDtypeStruct((ns, POOL_WIDTH), _F32),
                   jax.ShapeDtypeStruct((ns, CONV_WIDTH), _F32),
                   jax.ShapeDtypeStruct((ns, 2 * D_FF), _F32)],
        scratch_shapes=[pltpu.VMEM((d, IN_TOTAL), _BF16),
                        pltpu.VMEM((N_POOL_GROUPS, POOL_GC, POOL_OUT_GC), _BF16),
                        pltpu.VMEM((CONV_WIDTH, d), _BF16),
                        pltpu.VMEM((d, d), _BF16),
                        pltpu.VMEM((d, 2 * D_FF), _BF16),
                        pltpu.VMEM((D_FF, d), _BF16),
                        pltpu.VMEM((2, CAST_ROWS, STAGE_COLS), _F32),
                        pltpu.VMEM(sp.shape, _F32),
                        pltpu.VMEM(sc.shape, _F32),
                        pltpu.VMEM((POOL_HALO + N_META, POOL_WIDTH), _F32),
                        pltpu.VMEM((CONV_HALO + N_META, CONV_WIDTH), _F32),
                        pltpu.VMEM((CONV_HALO + N_META, 2 * D_FF), _F32),
                        pltpu.VMEM((ns, d), _BF16),
                        pltpu.VMEM((ns, D_FF), _BF16),
                        pltpu.VMEM((ns, d), _F32),
                        pltpu.VMEM((ns, d), _F32),
                        pltpu.VMEM((2, seg, SUBLANES, d), _F32),
                        pltpu.VMEM((2, seg, SUBLANES, d), _F32),
                        pltpu.SemaphoreType.DMA((2,)),
                        pltpu.SemaphoreType.DMA((2,)),
                        pltpu.SemaphoreType.DMA((2,)),
                        pltpu.SemaphoreType.DMA((5,)),
                        pltpu.VMEM((rows_with_head(POOL_BUF), POOL_WIDTH), _F32),
                        pltpu.VMEM((rows_with_head(CONV_W - 1), CONV_WIDTH), _F32),
                        pltpu.VMEM((rows_with_head(CONV_W - 1), 2 * D_FF), _F32),
                        pltpu.VMEM((POOL_BUF * SUBLANES, POOL_WIDTH), _F32),
                        pltpu.VMEM(((CONV_W - 1) * SUBLANES, CONV_WIDTH), _F32),
                        pltpu.VMEM(((CONV_W - 1) * SUBLANES, 2 * D_FF), _F32),
                        pltpu.VMEM((tm, d), _BF16),
                        pltpu.VMEM((tm, D_FF), _BF16),
                        pltpu.VMEM((tm, d), _BF16),
                        pltpu.VMEM((tm, d), _F32),
                        pltpu.VMEM((tm, d), _F32),
                        pltpu.VMEM((rows_with_head(POOL_BUF), POOL_WIDTH), _F32),
                        pltpu.VMEM((rows_with_head(CONV_W - 1), CONV_WIDTH), _F32),
                        pltpu.VMEM((tm, d), _BF16),
                        pltpu.VMEM((rows_with_head(CONV_W - 1), 2 * D_FF), _F32),
                        pltpu.VMEM((tm, D_FF), _BF16)],
        compiler_params=pltpu.CompilerParams(
            dimension_semantics=("arbitrary",), vmem_limit_bytes=VMEM_LIMIT_BYTES),
        name="hybrid_step",
    )(x, xs, sp, sc, sf, meta, *big_weights, pscale, w_conv, w_ffn_conv, *gains)


def kernel(x_prompt, x_sample, state_pool, state_conv, state_ffn, meta_tokens, w_in, w_pool,
           pool_scale, w_conv, w_conv_out, w_o, w_up, w_ffn_conv, w_down, g_pre_mix,
           g_post_mix, g_pre_ffn, g_post_ffn):
    depth = w_in.shape[0]
    assert depth == 1, "the prompt / sample streams are chained for a single layer only"
    assert x_sample.shape[1] == 1 and meta_tokens.shape[0] == N_META
    assert N_META >= max(POOL_WINDOWS)
    row = lambda a: a.reshape(1, -1)
    l = 0
    big = (w_in[l], w_pool[l], w_conv_out[l], w_o[l], w_up[l], w_down[l])
    gains = (row(g_pre_mix[l]), row(g_post_mix[l]), row(g_pre_ffn[l]), row(g_post_ffn[l]))

    to_rows_first = lambda s: jnp.transpose(s, (1, 0, 2))
    y_prompt, y_sample, pp, pc, pf, us, cs, fs = _fused_call(
        x_prompt, x_sample, to_rows_first(state_pool[l]), to_rows_first(state_conv[l]),
        to_rows_first(state_ffn[l]), meta_tokens.astype(_F32), big, row(pool_scale[l]), w_conv,
        w_ffn_conv, gains, l, TM_PROMPT)
    push = lambda state, new: jnp.concatenate([state[:, 1:], new[:, None, :]], axis=1)

    return (y_prompt, y_sample, pp, pc, pf,
            push(state_pool[l], us)[None], push(state_conv[l], cs)[None],
            push(state_ffn[l], fs)[None])
```

```python
import functools

import jax
import jax.numpy as jnp
from jax import lax
from jax.experimental import pallas as pl
from jax.experimental.pallas import tpu as pltpu

D_MODEL = 1024
N_META = 16
POOL_WINDOWS = (2, 4, 8, 16)
N_POOL_GROUPS = len(POOL_WINDOWS)
POOL_WIDTH = D_MODEL // 2
POOL_GC = POOL_WIDTH // N_POOL_GROUPS
POOL_OUT_GC = D_MODEL // N_POOL_GROUPS
POOL_BUF = max(POOL_WINDOWS) - 1
CONV_WIDTH = D_MODEL // 2
CONV_W = 3
D_FF = 2816
IN_TOTAL = 2 * POOL_WIDTH + 2 * CONV_WIDTH + 2 * D_MODEL
EPS = 1e-6
LOG2_E = 1.4426950408889634
OFF_U, OFF_V, OFF_B, OFF_C, OFF_GA, OFF_GB = 0, 512, 1024, 1536, 2048, 3072

SUBLANES = 8
FF_CHUNK = 256
POOL_HALO = 16
CONV_HALO = 8
TM_PROMPT = 256
CAST_ROWS = 128
STAGE_COLS = 2 * D_FF
VMEM_LIMIT_BYTES = 58 * 1024 * 1024

_BF16 = jnp.bfloat16
_F32 = jnp.float32


def _dot(a, b):
    return jnp.dot(a, b, preferred_element_type=_F32)


def _rms(x, g):
    ms = jnp.mean(x * x, axis=-1, keepdims=True)
    return x * lax.rsqrt(ms + EPS) * g


def _sigmoid(x):
    return 1.0 / (1.0 + jnp.exp2(x * (-LOG2_E)))


def _mixer(x, w, hist, mbuf, pool_div):
    (w_in, w_pool, pscale, wconv, wco, _, _, _, _, g1, _, _, _) = w
    h = _rms(x, g1[...]).astype(_BF16)

    hist.put_u(_dot(h, w_in[:, OFF_U:OFF_U + POOL_WIDTH]))
    v = _dot(h, w_in[:, OFF_V:OFF_V + CONV_WIDTH])
    cg = _dot(h, w_in[:, OFF_C:OFF_C + CONV_WIDTH])
    hist.put_c(cg * v)
    cv = wconv[CONV_W - 1:CONV_W, :] * hist.get_c(0)
    for k in range(1, CONV_W):
        cv = cv + wconv[CONV_W - 1 - k:CONV_W - k, :] * hist.get_c(k)
    bg = _dot(h, w_in[:, OFF_B:OFF_B + CONV_WIDTH])
    b_in = (bg * cv).astype(_BF16)

    for g, win in enumerate(POOL_WINDOWS):
        cs = slice(g * POOL_GC, (g + 1) * POOL_GC)
        os_ = slice(g * POOL_OUT_GC, (g + 1) * POOL_OUT_GC)
        ug = hist.get_u(0, cs)
        s = ug
        for k in range(1, win):
            s = s + hist.get_u(k, cs)
        pooled = (pool_div(s, win) - ug).astype(_BF16)
        a = _dot(pooled, w_pool[g]) * pscale[:, os_]
        ga = _dot(h, w_in[:, OFF_GA + g * POOL_OUT_GC:OFF_GA + (g + 1) * POOL_OUT_GC])
        gb = _dot(h, w_in[:, OFF_GB + g * POOL_OUT_GC:OFF_GB + (g + 1) * POOL_OUT_GC])
        b = _dot(b_in, wco[:, os_])
        mbuf[:, os_] = (_sigmoid(ga) * a + _sigmoid(gb) * b).astype(_BF16)


def _mixer_out(w, mbuf, rows=slice(None)):
    return _dot(mbuf[rows, :], w[5][...])


def _mixer_residual(x, mixed, w):
    return x + _rms(mixed, w[10][...])


def _ffn_norm(x1, w):
    return _rms(x1, w[11][...]).astype(_BF16)


def _ffn_up(h2, w, hist, actbuf):
    wup, wfc = w[6], w[7]
    for c in range(D_FF // FF_CHUNK):
        conv = []
        for col in (c * FF_CHUNK, D_FF + c * FF_CHUNK):
            cols = slice(col, col + FF_CHUNK)
            hist.put_up(_dot(h2, wup[:, cols]), cols)
            y = wfc[CONV_W - 1:CONV_W, cols] * hist.get_up(0, cols)
            for k in range(1, CONV_W):
                y = y + wfc[CONV_W - 1 - k:CONV_W - k, cols] * hist.get_up(k, cols)
            conv.append(y)
        gate, val = conv
        actbuf[:, c * FF_CHUNK:(c + 1) * FF_CHUNK] = (gate * _sigmoid(gate) * val).astype(_BF16)


def _ffn_down(w, actbuf, rows=slice(None)):
    return _dot(actbuf[rows, :], w[8][...])


def _ffn_residual(x1, down, w):
    return x1 + _rms(down, w[12][...])


def _full_windows(s, win):
    return s * (1.0 / win)


class _RowHistory:
    def __init__(self, ubuf, cbuf, upbuf, m):
        self.ubuf, self.cbuf, self.upbuf, self.m = ubuf, cbuf, upbuf, m

    def put_u(self, val):
        self.ubuf[POOL_HALO:POOL_HALO + self.m, :] = val

    def get_u(self, k, cs):
        return self.ubuf[POOL_HALO - k:POOL_HALO - k + self.m, cs]

    def put_c(self, val):
        self.cbuf[CONV_HALO:CONV_HALO + self.m, :] = val

    def get_c(self, k):
        return self.cbuf[CONV_HALO - k:CONV_HALO - k + self.m, :]

    def put_up(self, val, cols):
        self.upbuf[CONV_HALO:CONV_HALO + self.m, cols] = val

    def get_up(self, k, cols):
        return self.upbuf[CONV_HALO - k:CONV_HALO - k + self.m, cols]


class _SegmentHistory:
    def __init__(self, ubuf, cbuf, upbuf, utail, ctail, ftail, tm):
        self.ubuf, self.cbuf, self.upbuf, self.tm = ubuf, cbuf, upbuf, tm
        self.utail, self.ctail, self.ftail = utail, ctail, ftail

    def _put(self, buf, tail, val, cols, depth):
        tm = self.tm
        buf[depth * SUBLANES:depth * SUBLANES + tm, cols] = val
        sublane = lax.broadcasted_iota(jnp.int32, (SUBLANES, val.shape[1]), 0)
        for j in range(depth):
            rows = slice(j * SUBLANES, (j + 1) * SUBLANES)
            cur = val[tm - (depth - j) * SUBLANES:tm - (depth - j - 1) * SUBLANES, :]
            mixed = jnp.where(sublane == SUBLANES - 1, tail[rows, cols], cur)
            buf[rows, cols] = pltpu.roll(mixed, 1, axis=0)
            tail[rows, cols] = cur

    def _get(self, buf, k, cols, depth):
        start = (depth - k) * SUBLANES
        return buf[start:start + self.tm, cols]

    def put_u(self, val):
        self._put(self.ubuf, self.utail, val, slice(None), POOL_BUF)

    def get_u(self, k, cs):
        return self._get(self.ubuf, k, cs, POOL_BUF)

    def put_c(self, val):
        self._put(self.cbuf, self.ctail, val, slice(None), CONV_W - 1)

    def get_c(self, k):
        return self._get(self.cbuf, k, slice(None), CONV_W - 1)

    def put_up(self, val, cols):
        self._put(self.upbuf, self.ftail, val, cols, CONV_W - 1)

    def get_up(self, k, cols):
        return self._get(self.upbuf, k, cols, CONV_W - 1)


class _StateHistory:
    def __init__(self, sp_ref, sc_ref, sf_ref, u_ref, c_ref):
        self.sp, self.sc, self.sf = sp_ref, sc_ref, sf_ref
        self.u, self.c = u_ref, c_ref

    def _slab(self, cols):
        slab, off = divmod(cols.start, POOL_WIDTH)
        return slab, slice(off, off + cols.stop - cols.start)

    def put_u(self, val):
        self.u[...] = val

    def get_u(self, k, cs):
        return self.u[:, cs] if k == 0 else self.sp[POOL_BUF - k, :, cs]

    def put_c(self, val):
        self.c[...] = val

    def get_c(self, k):
        return self.c[...] if k == 0 else self.sc[CONV_W - 1 - k]

    def put_up(self, val, cols):
        slab, c = self._slab(cols)
        self.sp[slab, :, c] = val

    def get_up(self, k, cols):
        if k == 0:
            slab, c = self._slab(cols)
            return self.sp[slab, :, c]
        return self.sf[CONV_W - 1 - k, :, cols]


def _fill_tail(tail, state_ref, first_row, depth):
    for j in range(depth):
        r = first_row + j
        tail[j * SUBLANES:(j + 1) * SUBLANES, :] = jnp.broadcast_to(
            state_ref[r:r + 1, :], (SUBLANES, tail.shape[1]))


def _read_tail(out_ref, tail, depth):
    for j in range(depth):
        out_ref[j:j + 1, :] = tail[(j + 1) * SUBLANES - 1:(j + 1) * SUBLANES, :]


def _cast_batches(pairs):
    jobs = []
    for src, dst in pairs:
        if len(src.shape) == 3:
            assert src.shape[1] == CAST_ROWS
            jobs += [(src.at[g], dst.at[g], src.shape[2]) for g in range(src.shape[0])]
        else:
            assert src.shape[0] % CAST_ROWS == 0
            jobs += [(src.at[pl.ds(r, CAST_ROWS), :], dst.at[pl.ds(r, CAST_ROWS), :], src.shape[1])
                     for r in range(0, src.shape[0], CAST_ROWS)]
    batches, cur, used = [], [], 0
    for s, d, cols in jobs:
        if used + cols > STAGE_COLS:
            batches.append(cur)
            cur, used = [], 0
        cur.append((s, d, used, cols))
        used += cols
    batches.append(cur)
    return batches


def _cast_weights(first_pairs, later_pairs, between, stage, sem):
    first = _cast_batches(first_pairs)
    batches = first + _cast_batches(later_pairs)
    hooks = {len(first) + 2 * i: fn for i, fn in enumerate(between)}
    assert not hooks or max(hooks) < len(batches)

    def copies(b, slot):
        return [pltpu.make_async_copy(s, stage.at[slot, :, pl.ds(lane, cols)], sem.at[slot])
                for s, _, lane, cols in batches[b]]

    for b in range(min(2, len(batches))):
        for cp in copies(b, b):
            cp.start()
    for b in range(len(batches)):
        slot = b % 2
        for cp in copies(b, slot):
            cp.wait()
        for _, d, lane, cols in batches[b]:
            d[...] = stage[slot, :, lane:lane + cols].astype(_BF16)
        if b + 2 < len(batches):
            for cp in copies(b + 2, slot):
                cp.start()
        if b in hooks:
            hooks[b]()


def _fused_kernel(x_hbm, xs_hbm, sp_hbm, sc_hbm, sf_hbm, meta_ref,
                  w_in_hbm, w_pool_hbm, wco_hbm, wo_hbm, wup_hbm, wdown_hbm,
                  pscale, wconv, wfc, g1, g2, g3, g4,
                  y_hbm, ys_hbm, op_ref, oc_ref, of_ref, us_ref, cs_ref, fs_hbm,
                  w_in, w_pool, wco, wo, wup, wdown, stage, sp_v, sc_v, m_ubuf, m_cbuf, m_upbuf,
                  s_mbuf, s_actbuf, s_h2, m_h2, xs_v, ys_v, xbuf, ybuf, in_sem, out_sem, cast_sem, state_sem,
                  ubuf, cbuf, upbuf, utail, ctail, ftail, mbuf, actbuf, h2buf, x1_new, x1_old,
                  ubuf_first, cbuf_first, mbuf_first, upbuf_last, actbuf_last,
                  *, tm, nt, total):
    w = (w_in, w_pool, pscale, wconv, wco, wo, wup, wfc, wdown, g1, g2, g3, g4)
    seg = tm // SUBLANES
    n = pl.program_id(0)
    slot = n % 2

    def x_copies(tile, sl):
        b, t0 = tile // nt, (tile % nt) * tm
        return [pltpu.make_async_copy(x_hbm.at[b, pl.ds(t0 + s * seg, seg), :],
                                      xbuf.at[sl, :, s, :], in_sem.at[sl]) for s in range(SUBLANES)]

    def y_copies(tile):
        b, t0 = tile // nt, (tile % nt) * tm
        return [pltpu.make_async_copy(ybuf.at[:, s, :],
                                      y_hbm.at[b, pl.ds(t0 + s * seg, seg), :], out_sem.at[0])
                for s in range(SUBLANES)]

    def sample_up_copies():
        return [pltpu.make_async_copy(sp_v.at[j], fs_hbm.at[:, pl.ds(j * POOL_WIDTH, POOL_WIDTH)],
                                      state_sem.at[5]) for j in range(2 * D_FF // POOL_WIDTH)]

    def wait_tile(buf, sem):
        pltpu.make_async_copy(buf, buf, sem).wait()

    def start_sequence_mixer_history():
        _fill_tail(utail, m_ubuf, POOL_HALO + N_META - POOL_BUF, POOL_BUF)
        _fill_tail(ctail, m_cbuf, CONV_HALO + N_META - (CONV_W - 1), CONV_W - 1)

    def token_mixer(work):
        ub, cb, mb = work
        hist = _SegmentHistory(ub, cb, None, utail, ctail, None, tm)
        x = xbuf[slot].reshape(tm, D_MODEL)
        _mixer(x, w, hist, mb, _full_windows)
        x1 = _mixer_residual(x, _mixer_out(w, mb), w)
        x1_new[...] = x1
        h2buf[...] = _ffn_norm(x1, w)

    def channel_mixer_up(work):
        upb, ab = work
        _ffn_up(h2buf[...], w, _SegmentHistory(None, None, upb, None, None, ftail, tm), ab)

    def channel_mixer_out(work, x1_ref):
        y = _ffn_residual(x1_ref[...], _ffn_down(w, work[1]), w)
        ybuf[...] = y.reshape(seg, SUBLANES, D_MODEL)

    def prefetch_next_tile():
        for cp in x_copies(jnp.minimum(n + 1, total - 1), 1 - slot):
            cp.start()

    def send_previous_tile():
        for cp in y_copies(n - 1):
            cp.start()

    @pl.when(n == 0)
    def _():
        for cp in x_copies(n, slot):
            cp.start()
        pool_cp = pltpu.make_async_copy(sp_hbm, sp_v, state_sem.at[0])
        conv_cp = pltpu.make_async_copy(sc_hbm, sc_v, state_sem.at[1])
        ffn_cp = pltpu.make_async_copy(sf_hbm, stage, state_sem.at[2])
        xs_cp = pltpu.make_async_copy(xs_hbm.at[:, 0, :], xs_v, state_sem.at[3])
        ys_cp = pltpu.make_async_copy(ys_v, ys_hbm.at[:, 0, :], state_sem.at[4])
        pool_cp.start()
        conv_cp.start()
        xs_cp.start()

        m_ubuf[0:POOL_HALO, :] = jnp.zeros((POOL_HALO, POOL_WIDTH), _F32)
        m_cbuf[0:CONV_HALO, :] = jnp.zeros((CONV_HALO, CONV_WIDTH), _F32)
        m_upbuf[0:CONV_HALO, :] = jnp.zeros((CONV_HALO, 2 * D_FF), _F32)
        row1 = lax.broadcasted_iota(jnp.int32, (N_META, POOL_GC), 0) + 1

        def meta_windows(s, win):
            return s / jnp.minimum(win, row1).astype(_F32)

        meta_hist = _RowHistory(m_ubuf, m_cbuf, m_upbuf, N_META)
        meta_rows = pl.ds(0, N_META)

        def meta_token_mixer():
            _mixer(meta_ref[...], w, meta_hist, s_mbuf.at[meta_rows], meta_windows)
            x1 = _mixer_residual(meta_ref[...], _mixer_out(w, s_mbuf.at[meta_rows]), w)
            m_h2[...] = _ffn_norm(x1, w)

        sample_hist = _StateHistory(sp_v, sc_v, stage, us_ref, cs_ref)

        def sample_token_mixer():
            pool_cp.wait()
            conv_cp.wait()
            xs_cp.wait()
            _mixer(xs_v[...], w, sample_hist, s_mbuf, _full_windows)
            x1 = _mixer_residual(xs_v[...], _mixer_out(w, s_mbuf), w)
            ys_v[...] = x1
            s_h2[...] = _ffn_norm(x1, w)

        def first_tile_token_mixer():
            wait_tile(xbuf.at[slot], in_sem.at[slot])
            prefetch_next_tile()
            start_sequence_mixer_history()
            token_mixer((ubuf_first, cbuf_first, mbuf_first))

        _cast_weights([(w_in_hbm, w_in), (w_pool_hbm, w_pool), (wco_hbm, wco), (wo_hbm, wo)],
                      [(wup_hbm, wup), (wdown_hbm, wdown)],
                      [meta_token_mixer, sample_token_mixer, first_tile_token_mixer],
                      stage, cast_sem)
        ffn_cp.start()
        _ffn_up(m_h2[...], w, meta_hist, s_actbuf.at[meta_rows])
        ffn_cp.wait()
        _ffn_up(s_h2[...], w, sample_hist, s_actbuf)
        ys_v[...] = _ffn_residual(ys_v[...], _ffn_down(w, s_actbuf), w)
        ys_cp.start()
        for cp in sample_up_copies():
            cp.start()
        ys_cp.wait()

    @pl.when((n > 0) & (n < total))
    def _():
        wait_tile(xbuf.at[slot], in_sem.at[slot])

    @pl.when(n >= 2)
    def _():
        wait_tile(ybuf, out_sem.at[0])

    @pl.when((n % nt == 0) & (n > 0))
    def _():
        start_sequence_mixer_history()

    @pl.when(n % nt == 1)
    def _():
        _fill_tail(ftail, m_upbuf, CONV_HALO + N_META - (CONV_W - 1), CONV_W - 1)

    @pl.when((n > 0) & (n < total))
    def _():
        prefetch_next_tile()
        x1_old[...] = x1_new[...]
        channel_mixer_up((upbuf, actbuf))
        token_mixer((ubuf, cbuf, mbuf))
        channel_mixer_out((upbuf, actbuf), x1_old)
        send_previous_tile()

    @pl.when(n == total)
    def _():
        channel_mixer_up((upbuf_last, actbuf_last))
        channel_mixer_out((upbuf_last, actbuf_last), x1_new)
        send_previous_tile()

    @pl.when((n % nt == nt - 1) & (n < total))
    def _():
        _read_tail(op_ref, utail, POOL_BUF)
        _read_tail(oc_ref, ctail, CONV_W - 1)

    @pl.when((n % nt == 0) & (n >= 1))
    def _():
        _read_tail(of_ref, ftail, CONV_W - 1)

    @pl.when(n == total)
    def _():
        wait_tile(ybuf, out_sem.at[0])
        wait_tile(xbuf.at[slot], in_sem.at[slot])
        for cp in sample_up_copies():
            cp.wait()


def _resident(shape):
    zeros = (0,) * len(shape)
    return pl.BlockSpec(shape, lambda *_: zeros, pipeline_mode=pl.Buffered(1))


def _layer_slice(shape, l):
    return pl.BlockSpec((None,) + tuple(shape[1:]), lambda *_: (l, 0, 0), pipeline_mode=pl.Buffered(1))


def _fused_call(x, xs, sp, sc, sf, meta, big_weights, pscale, w_conv, w_ffn_conv, gains, l, tm):
    nb, t, d = x.shape
    ns = xs.shape[0]
    seg = tm // SUBLANES
    assert t % tm == 0 and tm % SUBLANES == 0 and seg >= POOL_BUF
    assert xs.shape == (ns, 1, d) and sf.shape == (2, CAST_ROWS, STAGE_COLS) and ns == CAST_ROWS
    assert sp.shape == (POOL_BUF, ns, POOL_WIDTH) and 2 * D_FF <= POOL_BUF * POOL_WIDTH
    assert (2 * D_FF) % POOL_WIDTH == 0 and POOL_WIDTH % FF_CHUNK == 0
    nt = t // tm
    assert nt >= 2
    total = nb * nt
    mix_state = lambda rows, c: pl.BlockSpec(
        (None, None, rows, c), lambda n: (0, jnp.minimum(n, total - 1) // nt, 0, 0))
    ffn_state = lambda rows, c: pl.BlockSpec(
        (None, None, rows, c), lambda n: (0, jnp.maximum(n - 1, 0) // nt, 0, 0))
    hbm = pl.BlockSpec(memory_space=pl.ANY)
    rows_with_head = lambda depth: (depth + seg) * SUBLANES
    return pl.pallas_call(
        functools.partial(_fused_kernel, tm=tm, nt=nt, total=total),
        grid=(total + 1,),
        in_specs=[hbm, hbm, hbm, hbm, hbm, _resident(meta.shape)]
                 + [hbm] * len(big_weights)
                 + [_resident(pscale.shape), _layer_slice(w_conv.shape, l),
                    _layer_slice(w_ffn_conv.shape, l)]
                 + [_resident(g.shape) for g in gains],
        out_specs=[hbm, hbm,
                   mix_state(POOL_BUF, POOL_WIDTH), mix_state(CONV_W - 1, CONV_WIDTH),
                   ffn_state(CONV_W - 1, 2 * D_FF),
                   _resident((ns, POOL_WIDTH)), _resident((ns, CONV_WIDTH)), hbm],
        out_shape=[jax.ShapeDtypeStruct((nb, t, d), _F32),
                   jax.ShapeDtypeStruct((ns, 1, d), _F32),
                   jax.ShapeDtypeStruct((1, nb, POOL_BUF, POOL_WIDTH), _F32),
                   jax.ShapeDtypeStruct((1, nb, CONV_W - 1, CONV_WIDTH), _F32),
                   jax.ShapeDtypeStruct((1, nb, CONV_W - 1, 2 * D_FF), _F32),
                   jax.ShapeDtypeStruct((ns, POOL_WIDTH), _F32),
                   jax.ShapeDtypeStruct((ns, CONV_WIDTH), _F32),
                   jax.ShapeDtypeStruct((ns, 2 * D_FF), _F32)],
        scratch_shapes=[pltpu.VMEM((d, IN_TOTAL), _BF16),
                        pltpu.VMEM((N_POOL_GROUPS, POOL_GC, POOL_OUT_GC), _BF16),
                        pltpu.VMEM((CONV_WIDTH, d), _BF16),
                        pltpu.VMEM((d, d), _BF16),
                        pltpu.VMEM((d, 2 * D_FF), _BF16),
                        pltpu.VMEM((D_FF, d), _BF16),
                        pltpu.VMEM((2, CAST_ROWS, STAGE_COLS), _F32),
                        pltpu.VMEM(sp.shape, _F32),
                        pltpu.VMEM(sc.shape, _F32),
                        pltpu.VMEM((POOL_HALO + N_META, POOL_WIDTH), _F32),
                        pltpu.VMEM((CONV_HALO + N_META, CONV_WIDTH), _F32),
                        pltpu.VMEM((CONV_HALO + N_META, 2 * D_FF), _F32),
                        pltpu.VMEM((ns, d), _BF16),
                        pltpu.VMEM((ns, D_FF), _BF16),
                        pltpu.VMEM((ns, d), _BF16),
                        pltpu.VMEM((N_META, d), _BF16),
                        pltpu.VMEM((ns, d), _F32),
                        pltpu.VMEM((ns, d), _F32),
                        pltpu.VMEM((2, seg, SUBLANES, d), _F32),
                        pltpu.VMEM((seg, SUBLANES, d), _F32),
                        pltpu.SemaphoreType.DMA((2,)),
                        pltpu.SemaphoreType.DMA((1,)),
                        pltpu.SemaphoreType.DMA((2,)),
                        pltpu.SemaphoreType.DMA((6,)),
                        pltpu.VMEM((rows_with_head(POOL_BUF), POOL_WIDTH), _F32),
                        pltpu.VMEM((rows_with_head(CONV_W - 1), CONV_WIDTH), _F32),
                        pltpu.VMEM((rows_with_head(CONV_W - 1), 2 * D_FF), _F32),
                        pltpu.VMEM((POOL_BUF * SUBLANES, POOL_WIDTH), _F32),
                        pltpu.VMEM(((CONV_W - 1) * SUBLANES, CONV_WIDTH), _F32),
                        pltpu.VMEM(((CONV_W - 1) * SUBLANES, 2 * D_FF), _F32),
                        pltpu.VMEM((tm, d), _BF16),
                        pltpu.VMEM((tm, D_FF), _BF16),
                        pltpu.VMEM((tm, d), _BF16),
                        pltpu.VMEM((tm, d), _F32),
                        pltpu.VMEM((tm, d), _F32),
                        pltpu.VMEM((rows_with_head(POOL_BUF), POOL_WIDTH), _F32),
                        pltpu.VMEM((rows_with_head(CONV_W - 1), CONV_WIDTH), _F32),
                        pltpu.VMEM((tm, d), _BF16),
                        pltpu.VMEM((rows_with_head(CONV_W - 1), 2 * D_FF), _F32),
                        pltpu.VMEM((tm, D_FF), _BF16)],
        compiler_params=pltpu.CompilerParams(
            dimension_semantics=("arbitrary",), vmem_limit_bytes=VMEM_LIMIT_BYTES),
        name="hybrid_step",
    )(x, xs, sp, sc, sf, meta, *big_weights, pscale, w_conv, w_ffn_conv, *gains)


def kernel(x_prompt, x_sample, state_pool, state_conv, state_ffn, meta_tokens, w_in, w_pool,
           pool_scale, w_conv, w_conv_out, w_o, w_up, w_ffn_conv, w_down, g_pre_mix,
           g_post_mix, g_pre_ffn, g_post_ffn):
    depth = w_in.shape[0]
    assert depth == 1, "the prompt / sample streams are chained for a single layer only"
    assert x_sample.shape[1] == 1 and meta_tokens.shape[0] == N_META
    assert N_META >= max(POOL_WINDOWS)
    row = lambda a: a.reshape(1, -1)
    l = 0
    big = (w_in[l], w_pool[l], w_conv_out[l], w_o[l], w_up[l], w_down[l])
    gains = (row(g_pre_mix[l]), row(g_post_mix[l]), row(g_pre_ffn[l]), row(g_post_ffn[l]))

    to_rows_first = lambda s: jnp.transpose(s, (1, 0, 2))
    y_prompt, y_sample, pp, pc, pf, us, cs, fs = _fused_call(
        x_prompt, x_sample, to_rows_first(state_pool[l]), to_rows_first(state_conv[l]),
        to_rows_first(state_ffn[l]), meta_tokens.astype(_F32), big, row(pool_scale[l]), w_conv,
        w_ffn_conv, gains, l, TM_PROMPT)
    push = lambda state, new: jnp.concatenate([state[:, 1:], new[:, None, :]], axis=1)

    return (y_prompt, y_sample, pp, pc, pf,
            push(state_pool[l], us)[None], push(state_conv[l], cs)[None],
            push(state_ffn[l], fs)[None])
```

```python
import functools

import jax
import jax.numpy as jnp
from jax import lax
from jax.experimental import pallas as pl
from jax.experimental.pallas import tpu as pltpu

D_MODEL = 1024
N_META = 16
POOL_WINDOWS = (2, 4, 8, 16)
N_POOL_GROUPS = len(POOL_WINDOWS)
POOL_WIDTH = D_MODEL // 2
POOL_GC = POOL_WIDTH // N_POOL_GROUPS
POOL_OUT_GC = D_MODEL // N_POOL_GROUPS
POOL_BUF = max(POOL_WINDOWS) - 1
CONV_WIDTH = D_MODEL // 2
CONV_W = 3
D_FF = 2816
IN_TOTAL = 2 * POOL_WIDTH + 2 * CONV_WIDTH + 2 * D_MODEL
EPS = 1e-6
LOG2_E = 1.4426950408889634
OFF_U, OFF_V, OFF_B, OFF_C, OFF_GA, OFF_GB = 0, 512, 1024, 1536, 2048, 3072

SUBLANES = 8
FF_CHUNK = 256
POOL_HALO = 16
CONV_HALO = 8
TM_PROMPT = 256
CAST_ROWS = 128
STAGE_COLS = 2 * D_FF
VMEM_LIMIT_BYTES = 58 * 1024 * 1024

_BF16 = jnp.bfloat16
_F32 = jnp.float32


def _dot(a, b):
    return jnp.dot(a, b, preferred_element_type=_F32)


def _rms(x, g):
    ms = jnp.mean(x * x, axis=-1, keepdims=True)
    return x * lax.rsqrt(ms + EPS) * g


def _sigmoid(x):
    return 1.0 / (1.0 + jnp.exp2(x * (-LOG2_E)))


def _mixer(x, w, hist, mbuf, pool_div):
    (w_in, w_pool, pscale, wconv, wco, _, _, _, _, g1, _, _, _) = w
    h = _rms(x, g1[...]).astype(_BF16)

    hist.put_u(_dot(h, w_in[:, OFF_U:OFF_U + POOL_WIDTH]))
    v = _dot(h, w_in[:, OFF_V:OFF_V + CONV_WIDTH])
    cg = _dot(h, w_in[:, OFF_C:OFF_C + CONV_WIDTH])
    hist.put_c(cg * v)
    cv = wconv[CONV_W - 1:CONV_W, :] * hist.get_c(0)
    for k in range(1, CONV_W):
        cv = cv + wconv[CONV_W - 1 - k:CONV_W - k, :] * hist.get_c(k)
    bg = _dot(h, w_in[:, OFF_B:OFF_B + CONV_WIDTH])
    b_in = (bg * cv).astype(_BF16)

    for g, win in enumerate(POOL_WINDOWS):
        cs = slice(g * POOL_GC, (g + 1) * POOL_GC)
        os_ = slice(g * POOL_OUT_GC, (g + 1) * POOL_OUT_GC)
        ug = hist.get_u(0, cs)
        s = ug
        for k in range(1, win):
            s = s + hist.get_u(k, cs)
        pooled = (pool_div(s, win) - ug).astype(_BF16)
        a = _dot(pooled, w_pool[g]) * pscale[:, os_]
        ga = _dot(h, w_in[:, OFF_GA + g * POOL_OUT_GC:OFF_GA + (g + 1) * POOL_OUT_GC])
        gb = _dot(h, w_in[:, OFF_GB + g * POOL_OUT_GC:OFF_GB + (g + 1) * POOL_OUT_GC])
        b = _dot(b_in, wco[:, os_])
        mbuf[:, os_] = (_sigmoid(ga) * a + _sigmoid(gb) * b).astype(_BF16)


def _mixer_out(w, mbuf, rows=slice(None)):
    return _dot(mbuf[rows, :], w[5][...])


def _mixer_residual(x, mixed, w):
    return x + _rms(mixed, w[10][...])


def _ffn_norm(x1, w):
    return _rms(x1, w[11][...]).astype(_BF16)


def _ffn_up(h2, w, hist, actbuf):
    wup, wfc = w[6], w[7]
    for c in range(D_FF // FF_CHUNK):
        conv = []
        for col in (c * FF_CHUNK, D_FF + c * FF_CHUNK):
            cols = slice(col, col + FF_CHUNK)
            hist.put_up(_dot(h2, wup[:, cols]), cols)
            y = wfc[CONV_W - 1:CONV_W, cols] * hist.get_up(0, cols)
            for k in range(1, CONV_W):
                y = y + wfc[CONV_W - 1 - k:CONV_W - k, cols] * hist.get_up(k, cols)
            conv.append(y)
        gate, val = conv
        actbuf[:, c * FF_CHUNK:(c + 1) * FF_CHUNK] = (gate * _sigmoid(gate) * val).astype(_BF16)


def _ffn_down(w, actbuf, rows=slice(None)):
    return _dot(actbuf[rows, :], w[8][...])


def _ffn_residual(x1, down, w):
    return x1 + _rms(down, w[12][...])


def _full_windows(s, win):
    return s * (1.0 / win)


class _RowHistory:
    def __init__(self, ubuf, cbuf, upbuf, m):
        self.ubuf, self.cbuf, self.upbuf, self.m = ubuf, cbuf, upbuf, m

    def put_u(self, val):
        self.ubuf[POOL_HALO:POOL_HALO + self.m, :] = val

    def get_u(self, k, cs):
        return self.ubuf[POOL_HALO - k:POOL_HALO - k + self.m, cs]

    def put_c(self, val):
        self.cbuf[CONV_HALO:CONV_HALO + self.m, :] = val

    def get_c(self, k):
        return self.cbuf[CONV_HALO - k:CONV_HALO - k + self.m, :]

    def put_up(self, val, cols):
        self.upbuf[CONV_HALO:CONV_HALO + self.m, cols] = val

    def get_up(self, k, cols):
        return self.upbuf[CONV_HALO - k:CONV_HALO - k + self.m, cols]


class _SegmentHistory:
    def __init__(self, ubuf, cbuf, upbuf, utail, ctail, ftail, tm):
        self.ubuf, self.cbuf, self.upbuf, self.tm = ubuf, cbuf, upbuf, tm
        self.utail, self.ctail, self.ftail = utail, ctail, ftail

    def _put(self, buf, tail, val, cols, depth):
        tm = self.tm
        buf[depth * SUBLANES:depth * SUBLANES + tm, cols] = val
        sublane = lax.broadcasted_iota(jnp.int32, (SUBLANES, val.shape[1]), 0)
        for j in range(depth):
            rows = slice(j * SUBLANES, (j + 1) * SUBLANES)
            cur = val[tm - (depth - j) * SUBLANES:tm - (depth - j - 1) * SUBLANES, :]
            mixed = jnp.where(sublane == SUBLANES - 1, tail[rows, cols], cur)
            buf[rows, cols] = pltpu.roll(mixed, 1, axis=0)
            tail[rows, cols] = cur

    def _get(self, buf, k, cols, depth):
        start = (depth - k) * SUBLANES
        return buf[start:start + self.tm, cols]

    def put_u(self, val):
        self._put(self.ubuf, self.utail, val, slice(None), POOL_BUF)

    def get_u(self, k, cs):
        return self._get(self.ubuf, k, cs, POOL_BUF)

    def put_c(self, val):
        self._put(self.cbuf, self.ctail, val, slice(None), CONV_W - 1)

    def get_c(self, k):
        return self._get(self.cbuf, k, slice(None), CONV_W - 1)

    def put_up(self, val, cols):
        self._put(self.upbuf, self.ftail, val, cols, CONV_W - 1)

    def get_up(self, k, cols):
        return self._get(self.upbuf, k, cols, CONV_W - 1)


class _StateHistory:
    def __init__(self, sp_ref, sc_ref, sf_ref, u_ref, c_ref):
        self.sp, self.sc, self.sf = sp_ref, sc_ref, sf_ref
        self.u, self.c = u_ref, c_ref

    def _slab(self, cols):
        slab, off = divmod(cols.start, POOL_WIDTH)
        return slab, slice(off, off + cols.stop - cols.start)

    def put_u(self, val):
        self.u[...] = val

    def get_u(self, k, cs):
        return self.u[:, cs] if k == 0 else self.sp[POOL_BUF - k, :, cs]

    def put_c(self, val):
        self.c[...] = val

    def get_c(self, k):
        return self.c[...] if k == 0 else self.sc[CONV_W - 1 - k]

    def put_up(self, val, cols):
        slab, c = self._slab(cols)
        self.sp[slab, :, c] = val

    def get_up(self, k, cols):
        if k == 0:
            slab, c = self._slab(cols)
            return self.sp[slab, :, c]
        return self.sf[CONV_W - 1 - k, :, cols]


def _fill_tail(tail, state_ref, first_row, depth):
    for j in range(depth):
        r = first_row + j
        tail[j * SUBLANES:(j + 1) * SUBLANES, :] = jnp.broadcast_to(
            state_ref[r:r + 1, :], (SUBLANES, tail.shape[1]))


def _read_tail(out_ref, tail, depth):
    for j in range(depth):
        out_ref[j:j + 1, :] = tail[(j + 1) * SUBLANES - 1:(j + 1) * SUBLANES, :]


def _cast_batches(pairs):
    jobs = []
    for src, dst in pairs:
        if len(src.shape) == 3:
            assert src.shape[1] == CAST_ROWS
            jobs += [(src.at[g], dst.at[g], src.shape[2]) for g in range(src.shape[0])]
        else:
            assert src.shape[0] % CAST_ROWS == 0
            jobs += [(src.at[pl.ds(r, CAST_ROWS), :], dst.at[pl.ds(r, CAST_ROWS), :], src.shape[1])
                     for r in range(0, src.shape[0], CAST_ROWS)]
    batches, cur, used = [], [], 0
    for s, d, cols in jobs:
        if used + cols > STAGE_COLS:
            batches.append(cur)
            cur, used = [], 0
        cur.append((s, d, used, cols))
        used += cols
    batches.append(cur)
    return batches


def _cast_weights(first_pairs, later_pairs, between, stage, sem):
    first = _cast_batches(first_pairs)
    batches = first + _cast_batches(later_pairs)
    hooks = {len(first) + 2 * i: fn for i, fn in enumerate(between)}
    assert not hooks or max(hooks) < len(batches)

    def copies(b, slot):
        return [pltpu.make_async_copy(s, stage.at[slot, :, pl.ds(lane, cols)], sem.at[slot])
                for s, _, lane, cols in batches[b]]

    for b in range(min(2, len(batches))):
        for cp in copies(b, b):
            cp.start()
    for b in range(len(batches)):
        slot = b % 2
        for cp in copies(b, slot):
            cp.wait()
        for _, d, lane, cols in batches[b]:
            d[...] = stage[slot, :, lane:lane + cols].astype(_BF16)
        if b + 2 < len(batches):
            for cp in copies(b + 2, slot):
                cp.start()
        if b in hooks:
            hooks[b]()


def _fused_kernel(x_hbm, xs_hbm, sp_hbm, sc_hbm, sf_hbm, meta_ref,
                  w_in_hbm, w_pool_hbm, wco_hbm, wo_hbm, wup_hbm, wdown_hbm,
                  pscale, wconv, wfc, g1, g2, g3, g4,
                  y_hbm, ys_hbm, op_ref, oc_ref, of_ref, us_ref, cs_ref, fs_hbm,
                  w_in, w_pool, wco, wo, wup, wdown, stage, sp_v, sc_v, m_ubuf, m_cbuf, m_upbuf,
                  s_mbuf, s_actbuf, s_h2, m_h2, xs_v, ys_v, xbuf, ybuf, in_sem, out_sem, cast_sem, state_sem,
                  ubuf, cbuf, upbuf, utail, ctail, ftail, mbuf, actbuf, h2buf, x1_new, x1_old,
                  ubuf_first, cbuf_first, mbuf_first, upbuf_last, actbuf_last,
                  *, tm, nt, total):
    w = (w_in, w_pool, pscale, wconv, wco, wo, wup, wfc, wdown, g1, g2, g3, g4)
    seg = tm // SUBLANES
    n = pl.program_id(0)
    slot = n % 2

    def x_copies(tile, sl):
        b, t0 = tile // nt, (tile % nt) * tm
        return [pltpu.make_async_copy(x_hbm.at[b, pl.ds(t0 + s * seg, seg), :],
                                      xbuf.at[sl, :, s, :], in_sem.at[sl]) for s in range(SUBLANES)]

    def y_copies(tile, sl):
        b, t0 = tile // nt, (tile % nt) * tm
        return [pltpu.make_async_copy(ybuf.at[sl, :, s, :],
                                      y_hbm.at[b, pl.ds(t0 + s * seg, seg), :], out_sem.at[sl])
                for s in range(SUBLANES)]

    def sample_up_copies():
        return [pltpu.make_async_copy(sp_v.at[j], fs_hbm.at[:, pl.ds(j * POOL_WIDTH, POOL_WIDTH)],
                                      state_sem.at[5]) for j in range(2 * D_FF // POOL_WIDTH)]

    def wait_tile(buf, sem):
        pltpu.make_async_copy(buf, buf, sem).wait()

    def start_sequence_mixer_history():
        _fill_tail(utail, m_ubuf, POOL_HALO + N_META - POOL_BUF, POOL_BUF)
        _fill_tail(ctail, m_cbuf, CONV_HALO + N_META - (CONV_W - 1), CONV_W - 1)

    def token_mixer(work):
        ub, cb, mb = work
        hist = _SegmentHistory(ub, cb, None, utail, ctail, None, tm)
        x = xbuf[slot].reshape(tm, D_MODEL)
        _mixer(x, w, hist, mb, _full_windows)
        x1 = _mixer_residual(x, _mixer_out(w, mb), w)
        x1_new[...] = x1
        h2buf[...] = _ffn_norm(x1, w)

    def channel_mixer_up(work):
        upb, ab = work
        _ffn_up(h2buf[...], w, _SegmentHistory(None, None, upb, None, None, ftail, tm), ab)

    def channel_mixer_out(work, x1_ref):
        y = _ffn_residual(x1_ref[...], _ffn_down(w, work[1]), w)
        ybuf[1 - slot] = y.reshape(seg, SUBLANES, D_MODEL)

    def prefetch_next_tile():
        for cp in x_copies(jnp.minimum(n + 1, total - 1), 1 - slot):
            cp.start()

    def send_previous_tile():
        for cp in y_copies(n - 1, 1 - slot):
            cp.start()

    @pl.when(n == 0)
    def _():
        for cp in x_copies(n, slot):
            cp.start()
        pool_cp = pltpu.make_async_copy(sp_hbm, sp_v, state_sem.at[0])
        conv_cp = pltpu.make_async_copy(sc_hbm, sc_v, state_sem.at[1])
        ffn_cp = pltpu.make_async_copy(sf_hbm, stage, state_sem.at[2])
        xs_cp = pltpu.make_async_copy(xs_hbm.at[:, 0, :], xs_v, state_sem.at[3])
        ys_cp = pltpu.make_async_copy(ys_v, ys_hbm.at[:, 0, :], state_sem.at[4])
        pool_cp.start()
        conv_cp.start()
        xs_cp.start()

        m_ubuf[0:POOL_HALO, :] = jnp.zeros((POOL_HALO, POOL_WIDTH), _F32)
        m_cbuf[0:CONV_HALO, :] = jnp.zeros((CONV_HALO, CONV_WIDTH), _F32)
        m_upbuf[0:CONV_HALO, :] = jnp.zeros((CONV_HALO, 2 * D_FF), _F32)
        row1 = lax.broadcasted_iota(jnp.int32, (N_META, POOL_GC), 0) + 1

        def meta_windows(s, win):
            return s / jnp.minimum(win, row1).astype(_F32)

        meta_hist = _RowHistory(m_ubuf, m_cbuf, m_upbuf, N_META)
        meta_rows = pl.ds(0, N_META)

        def meta_token_mixer():
            _mixer(meta_ref[...], w, meta_hist, s_mbuf.at[meta_rows], meta_windows)
            x1 = _mixer_residual(meta_ref[...], _mixer_out(w, s_mbuf.at[meta_rows]), w)
            m_h2[...] = _ffn_norm(x1, w)

        sample_hist = _StateHistory(sp_v, sc_v, stage, us_ref, cs_ref)

        def sample_token_mixer():
            pool_cp.wait()
            conv_cp.wait()
            xs_cp.wait()
            _mixer(xs_v[...], w, sample_hist, s_mbuf, _full_windows)
            x1 = _mixer_residual(xs_v[...], _mixer_out(w, s_mbuf), w)
            ys_v[...] = x1
            s_h2[...] = _ffn_norm(x1, w)

        def first_tile_token_mixer():
            wait_tile(xbuf.at[slot], in_sem.at[slot])
            prefetch_next_tile()
            start_sequence_mixer_history()
            token_mixer((ubuf_first, cbuf_first, mbuf_first))

        _cast_weights([(w_in_hbm, w_in), (w_pool_hbm, w_pool), (wco_hbm, wco), (wo_hbm, wo)],
                      [(wup_hbm, wup), (wdown_hbm, wdown)],
                      [meta_token_mixer, sample_token_mixer, first_tile_token_mixer],
                      stage, cast_sem)
        ffn_cp.start()
        _ffn_up(m_h2[...], w, meta_hist, s_actbuf.at[meta_rows])
        ffn_cp.wait()
        _ffn_up(s_h2[...], w, sample_hist, s_actbuf)
        ys_v[...] = _ffn_residual(ys_v[...], _ffn_down(w, s_actbuf), w)
        ys_cp.start()
        for cp in sample_up_copies():
            cp.start()
        ys_cp.wait()

    @pl.when((n > 0) & (n < total))
    def _():
        wait_tile(xbuf.at[slot], in_sem.at[slot])

    @pl.when(n >= 3)
    def _():
        wait_tile(ybuf.at[1 - slot], out_sem.at[1 - slot])

    @pl.when((n % nt == 0) & (n > 0))
    def _():
        start_sequence_mixer_history()

    @pl.when(n % nt == 1)
    def _():
        _fill_tail(ftail, m_upbuf, CONV_HALO + N_META - (CONV_W - 1), CONV_W - 1)

    @pl.when((n > 0) & (n < total))
    def _():
        prefetch_next_tile()
        x1_old[...] = x1_new[...]
        channel_mixer_up((upbuf, actbuf))
        token_mixer((ubuf, cbuf, mbuf))
        channel_mixer_out((upbuf, actbuf), x1_old)
        send_previous_tile()

    @pl.when(n == total)
    def _():
        channel_mixer_up((upbuf_last, actbuf_last))
        channel_mixer_out((upbuf_last, actbuf_last), x1_new)
        send_previous_tile()

    @pl.when((n % nt == nt - 1) & (n < total))
    def _():
        _read_tail(op_ref, utail, POOL_BUF)
        _read_tail(oc_ref, ctail, CONV_W - 1)

    @pl.when((n % nt == 0) & (n >= 1))
    def _():
        _read_tail(of_ref, ftail, CONV_W - 1)

    @pl.when(n == total)
    def _():
        wait_tile(ybuf.at[slot], out_sem.at[slot])
        wait_tile(ybuf.at[1 - slot], out_sem.at[1 - slot])
        wait_tile(xbuf.at[slot], in_sem.at[slot])
        for cp in sample_up_copies():
            cp.wait()


def _resident(shape):
    zeros = (0,) * len(shape)
    return pl.BlockSpec(shape, lambda *_: zeros, pipeline_mode=pl.Buffered(1))


def _layer_slice(shape, l):
    return pl.BlockSpec((None,) + tuple(shape[1:]), lambda *_: (l, 0, 0), pipeline_mode=pl.Buffered(1))


def _fused_call(x, xs, sp, sc, sf, meta, big_weights, pscale, w_conv, w_ffn_conv, gains, l, tm):
    nb, t, d = x.shape
    ns = xs.shape[0]
    seg = tm // SUBLANES
    assert t % tm == 0 and tm % SUBLANES == 0 and seg >= POOL_BUF
    assert xs.shape == (ns, 1, d) and sf.shape == (2, CAST_ROWS, STAGE_COLS) and ns == CAST_ROWS
    assert sp.shape == (POOL_BUF, ns, POOL_WIDTH) and 2 * D_FF <= POOL_BUF * POOL_WIDTH
    assert (2 * D_FF) % POOL_WIDTH == 0 and POOL_WIDTH % FF_CHUNK == 0
    nt = t // tm
    assert nt >= 2
    total = nb * nt
    mix_state = lambda rows, c: pl.BlockSpec(
        (None, None, rows, c), lambda n: (0, jnp.minimum(n, total - 1) // nt, 0, 0))
    ffn_state = lambda rows, c: pl.BlockSpec(
        (None, None, rows, c), lambda n: (0, jnp.maximum(n - 1, 0) // nt, 0, 0))
    hbm = pl.BlockSpec(memory_space=pl.ANY)
    rows_with_head = lambda depth: (depth + seg) * SUBLANES
    return pl.pallas_call(
        functools.partial(_fused_kernel, tm=tm, nt=nt, total=total),
        grid=(total + 1,),
        in_specs=[hbm, hbm, hbm, hbm, hbm, _resident(meta.shape)]
                 + [hbm] * len(big_weights)
                 + [_resident(pscale.shape), _layer_slice(w_conv.shape, l),
                    _layer_slice(w_ffn_conv.shape, l)]
                 + [_resident(g.shape) for g in gains],
        out_specs=[hbm, hbm,
                   mix_state(POOL_BUF, POOL_WIDTH), mix_state(CONV_W - 1, CONV_WIDTH),
                   ffn_state(CONV_W - 1, 2 * D_FF),
                   _resident((ns, POOL_WIDTH)), _resident((ns, CONV_WIDTH)), hbm],
        out_shape=[jax.ShapeDtypeStruct((nb, t, d), _F32),
                   jax.ShapeDtypeStruct((ns, 1, d), _F32),
                   jax.ShapeDtypeStruct((1, nb, POOL_BUF, POOL_WIDTH), _F32),
                   jax.ShapeDtypeStruct((1, nb, CONV_W - 1, CONV_WIDTH), _F32),
                   jax.ShapeDtypeStruct((1, nb, CONV_W - 1, 2 * D_FF), _F32),
                   jax.ShapeDtypeStruct((ns, POOL_WIDTH), _F32),
                   jax.ShapeDtypeStruct((ns, CONV_WIDTH), _F32),
                   jax.ShapeDtypeStruct((ns, 2 * D_FF), _F32)],
        scratch_shapes=[pltpu.VMEM((d, IN_TOTAL), _BF16),
                        pltpu.VMEM((N_POOL_GROUPS, POOL_GC, POOL_OUT_GC), _BF16),
                        pltpu.VMEM((CONV_WIDTH, d), _BF16),
                        pltpu.VMEM((d, d), _BF16),
                        pltpu.VMEM((d, 2 * D_FF), _BF16),
                        pltpu.VMEM((D_FF, d), _BF16),
                        pltpu.VMEM((2, CAST_ROWS, STAGE_COLS), _F32),
                        pltpu.VMEM(sp.shape, _F32),
                        pltpu.VMEM(sc.shape, _F32),
                        pltpu.VMEM((POOL_HALO + N_META, POOL_WIDTH), _F32),
                        pltpu.VMEM((CONV_HALO + N_META, CONV_WIDTH), _F32),
                        pltpu.VMEM((CONV_HALO + N_META, 2 * D_FF), _F32),
                        pltpu.VMEM((ns, d), _BF16),
                        pltpu.VMEM((ns, D_FF), _BF16),
                        pltpu.VMEM((ns, d), _BF16),
                        pltpu.VMEM((N_META, d), _BF16),
                        pltpu.VMEM((ns, d), _F32),
                        pltpu.VMEM((ns, d), _F32),
                        pltpu.VMEM((2, seg, SUBLANES, d), _F32),
                        pltpu.VMEM((2, seg, SUBLANES, d), _F32),
                        pltpu.SemaphoreType.DMA((2,)),
                        pltpu.SemaphoreType.DMA((2,)),
                        pltpu.SemaphoreType.DMA((2,)),
                        pltpu.SemaphoreType.DMA((6,)),
                        pltpu.VMEM((rows_with_head(POOL_BUF), POOL_WIDTH), _F32),
                        pltpu.VMEM((rows_with_head(CONV_W - 1), CONV_WIDTH), _F32),
                        pltpu.VMEM((rows_with_head(CONV_W - 1), 2 * D_FF), _F32),
                        pltpu.VMEM((POOL_BUF * SUBLANES, POOL_WIDTH), _F32),
                        pltpu.VMEM(((CONV_W - 1) * SUBLANES, CONV_WIDTH), _F32),
                        pltpu.VMEM(((CONV_W - 1) * SUBLANES, 2 * D_FF), _F32),
                        pltpu.VMEM((tm, d), _BF16),
                        pltpu.VMEM((tm, D_FF), _BF16),
                        pltpu.VMEM((tm, d), _BF16),
                        pltpu.VMEM((tm, d), _F32),
                        pltpu.VMEM((tm, d), _F32),
                        pltpu.VMEM((rows_with_head(POOL_BUF), POOL_WIDTH), _F32),
                        pltpu.VMEM((rows_with_head(CONV_W - 1), CONV_WIDTH), _F32),
                        pltpu.VMEM((tm, d), _BF16),
                        pltpu.VMEM((rows_with_head(CONV_W - 1), 2 * D_FF), _F32),
                        pltpu.VMEM((tm, D_FF), _BF16)],
        compiler_params=pltpu.CompilerParams(
            dimension_semantics=("arbitrary",), vmem_limit_bytes=VMEM_LIMIT_BYTES),
        name="hybrid_step",
    )(x, xs, sp, sc, sf, meta, *big_weights, pscale, w_conv, w_ffn_conv, *gains)


def kernel(x_prompt, x_sample, state_pool, state_conv, state_ffn, meta_tokens, w_in, w_pool,
           pool_scale, w_conv, w_conv_out, w_o, w_up, w_ffn_conv, w_down, g_pre_mix,
           g_post_mix, g_pre_ffn, g_post_ffn):
    depth = w_in.shape[0]
    assert depth == 1, "the prompt / sample streams are chained for a single layer only"
    assert x_sample.shape[1] == 1 and meta_tokens.shape[0] == N_META
    assert N_META >= max(POOL_WINDOWS)
    row = lambda a: a.reshape(1, -1)
    l = 0
    big = (w_in[l], w_pool[l], w_conv_out[l], w_o[l], w_up[l], w_down[l])
    gains = (row(g_pre_mix[l]), row(g_post_mix[l]), row(g_pre_ffn[l]), row(g_post_ffn[l]))

    to_rows_first = lambda s: jnp.transpose(s, (1, 0, 2))
    y_prompt, y_sample, pp, pc, pf, us, cs, fs = _fused_call(
        x_prompt, x_sample, to_rows_first(state_pool[l]), to_rows_first(state_conv[l]),
        to_rows_first(state_ffn[l]), meta_tokens.astype(_F32), big, row(pool_scale[l]), w_conv,
        w_ffn_conv, gains, l, TM_PROMPT)
    push = lambda state, new: jnp.concatenate([state[:, 1:], new[:, None, :]], axis=1)

    return (y_prompt, y_sample, pp, pc, pf,
            push(state_pool[l], us)[None], push(state_conv[l], cs)[None],
            push(state_ffn[l], fs)[None])
```

```python
import functools

import jax
import jax.numpy as jnp
from jax import lax
from jax.experimental import pallas as pl
from jax.experimental.pallas import tpu as pltpu

D_MODEL = 1024
N_META = 16
POOL_WINDOWS = (2, 4, 8, 16)
N_POOL_GROUPS = len(POOL_WINDOWS)
POOL_WIDTH = D_MODEL // 2
POOL_GC = POOL_WIDTH // N_POOL_GROUPS
POOL_OUT_GC = D_MODEL // N_POOL_GROUPS
POOL_BUF = max(POOL_WINDOWS) - 1
CONV_WIDTH = D_MODEL // 2
CONV_W = 3
D_FF = 2816
IN_TOTAL = 2 * POOL_WIDTH + 2 * CONV_WIDTH + 2 * D_MODEL
EPS = 1e-6
LOG2_E = 1.4426950408889634
OFF_U, OFF_V, OFF_B, OFF_C, OFF_GA, OFF_GB = 0, 512, 1024, 1536, 2048, 3072

SUBLANES = 8
FF_CHUNK = 256
POOL_HALO = 16
CONV_HALO = 8
TM_PROMPT = 256
CAST_ROWS = 128
SF_GROUP = 8
STAGE_COLS = 2 * D_FF
VMEM_LIMIT_BYTES = 58 * 1024 * 1024

_BF16 = jnp.bfloat16
_F32 = jnp.float32


def _dot(a, b):
    return jnp.dot(a, b, preferred_element_type=_F32)


def _rms(x, g):
    ms = jnp.mean(x * x, axis=-1, keepdims=True)
    return x * lax.rsqrt(ms + EPS) * g


def _sigmoid(x):
    return 1.0 / (1.0 + jnp.exp2(x * (-LOG2_E)))


def _mixer(x, w, hist, mbuf, pool_div):
    (w_in, w_pool, pscale, wconv, wco, _, _, _, _, g1, _, _, _) = w
    h = _rms(x, g1[...]).astype(_BF16)

    hist.put_u(_dot(h, w_in[:, OFF_U:OFF_U + POOL_WIDTH]))
    v = _dot(h, w_in[:, OFF_V:OFF_V + CONV_WIDTH])
    cg = _dot(h, w_in[:, OFF_C:OFF_C + CONV_WIDTH])
    hist.put_c(cg * v)
    cv = wconv[CONV_W - 1:CONV_W, :] * hist.get_c(0)
    for k in range(1, CONV_W):
        cv = cv + wconv[CONV_W - 1 - k:CONV_W - k, :] * hist.get_c(k)
    bg = _dot(h, w_in[:, OFF_B:OFF_B + CONV_WIDTH])
    b_in = (bg * cv).astype(_BF16)

    for g, win in enumerate(POOL_WINDOWS):
        cs = slice(g * POOL_GC, (g + 1) * POOL_GC)
        os_ = slice(g * POOL_OUT_GC, (g + 1) * POOL_OUT_GC)
        ug = hist.get_u(0, cs)
        s = ug
        for k in range(1, win):
            s = s + hist.get_u(k, cs)
        pooled = (pool_div(s, win) - ug).astype(_BF16)
        a = _dot(pooled, w_pool[g]) * pscale[:, os_]
        ga = _dot(h, w_in[:, OFF_GA + g * POOL_OUT_GC:OFF_GA + (g + 1) * POOL_OUT_GC])
        gb = _dot(h, w_in[:, OFF_GB + g * POOL_OUT_GC:OFF_GB + (g + 1) * POOL_OUT_GC])
        b = _dot(b_in, wco[:, os_])
        mbuf[:, os_] = (_sigmoid(ga) * a + _sigmoid(gb) * b).astype(_BF16)


def _mixer_out(w, mbuf, rows=slice(None)):
    return _dot(mbuf[rows, :], w[5][...])


def _mixer_residual(x, mixed, w):
    return x + _rms(mixed, w[10][...])


def _ffn_norm(x1, w):
    return _rms(x1, w[11][...]).astype(_BF16)


def _ffn_up(h2, w, hist, actbuf):
    wup, wfc = w[6], w[7]
    for c in range(D_FF // FF_CHUNK):
        conv = []
        for col in (c * FF_CHUNK, D_FF + c * FF_CHUNK):
            cols = slice(col, col + FF_CHUNK)
            hist.put_up(_dot(h2, wup[:, cols]), cols)
            y = wfc[CONV_W - 1:CONV_W, cols] * hist.get_up(0, cols)
            for k in range(1, CONV_W):
                y = y + wfc[CONV_W - 1 - k:CONV_W - k, cols] * hist.get_up(k, cols)
            conv.append(y)
        gate, val = conv
        actbuf[:, c * FF_CHUNK:(c + 1) * FF_CHUNK] = (gate * _sigmoid(gate) * val).astype(_BF16)


def _ffn_down(w, actbuf, rows=slice(None)):
    return _dot(actbuf[rows, :], w[8][...])


def _ffn_residual(x1, down, w):
    return x1 + _rms(down, w[12][...])


def _full_windows(s, win):
    return s * (1.0 / win)


class _RowHistory:
    def __init__(self, ubuf, cbuf, upbuf, m):
        self.ubuf, self.cbuf, self.upbuf, self.m = ubuf, cbuf, upbuf, m

    def put_u(self, val):
        self.ubuf[POOL_HALO:POOL_HALO + self.m, :] = val

    def get_u(self, k, cs):
        return self.ubuf[POOL_HALO - k:POOL_HALO - k + self.m, cs]

    def put_c(self, val):
        self.cbuf[CONV_HALO:CONV_HALO + self.m, :] = val

    def get_c(self, k):
        return self.cbuf[CONV_HALO - k:CONV_HALO - k + self.m, :]

    def put_up(self, val, cols):
        self.upbuf[CONV_HALO:CONV_HALO + self.m, cols] = val

    def get_up(self, k, cols):
        return self.upbuf[CONV_HALO - k:CONV_HALO - k + self.m, cols]


class _SegmentHistory:
    def __init__(self, ubuf, cbuf, upbuf, utail, ctail, ftail, tm):
        self.ubuf, self.cbuf, self.upbuf, self.tm = ubuf, cbuf, upbuf, tm
        self.utail, self.ctail, self.ftail = utail, ctail, ftail

    def _put(self, buf, tail, val, cols, depth):
        tm = self.tm
        buf[depth * SUBLANES:depth * SUBLANES + tm, cols] = val
        sublane = lax.broadcasted_iota(jnp.int32, (SUBLANES, val.shape[1]), 0)
        for j in range(depth):
            rows = slice(j * SUBLANES, (j + 1) * SUBLANES)
            cur = val[tm - (depth - j) * SUBLANES:tm - (depth - j - 1) * SUBLANES, :]
            mixed = jnp.where(sublane == SUBLANES - 1, tail[rows, cols], cur)
            buf[rows, cols] = pltpu.roll(mixed, 1, axis=0)
            tail[rows, cols] = cur

    def _get(self, buf, k, cols, depth):
        start = (depth - k) * SUBLANES
        return buf[start:start + self.tm, cols]

    def put_u(self, val):
        self._put(self.ubuf, self.utail, val, slice(None), POOL_BUF)

    def get_u(self, k, cs):
        return self._get(self.ubuf, k, cs, POOL_BUF)

    def put_c(self, val):
        self._put(self.cbuf, self.ctail, val, slice(None), CONV_W - 1)

    def get_c(self, k):
        return self._get(self.cbuf, k, slice(None), CONV_W - 1)

    def put_up(self, val, cols):
        self._put(self.upbuf, self.ftail, val, cols, CONV_W - 1)

    def get_up(self, k, cols):
        return self._get(self.upbuf, k, cols, CONV_W - 1)


class _StateHistory:
    def __init__(self, sp_ref, sc_ref, sf_ref, u_ref, c_ref, f_ref):
        self.sp, self.sc, self.sf = sp_ref, sc_ref, sf_ref
        self.u, self.c, self.f = u_ref, c_ref, f_ref

    def put_u(self, val):
        self.u[...] = val

    def get_u(self, k, cs):
        return self.u[:, cs] if k == 0 else self.sp[:, POOL_BUF - k, cs]

    def put_c(self, val):
        self.c[...] = val

    def get_c(self, k):
        return self.c[...] if k == 0 else self.sc[CONV_W - 1 - k]

    def put_up(self, val, cols):
        self.f[:, cols] = val

    def get_up(self, k, cols):
        return self.f[:, cols] if k == 0 else self.sf[CONV_W - 1 - k, :, cols]


def _fill_tail(tail, state_ref, first_row, depth):
    for j in range(depth):
        r = first_row + j
        tail[j * SUBLANES:(j + 1) * SUBLANES, :] = jnp.broadcast_to(
            state_ref[r:r + 1, :], (SUBLANES, tail.shape[1]))


def _read_tail(out_ref, tail, depth):
    for j in range(depth):
        out_ref[j:j + 1, :] = tail[(j + 1) * SUBLANES - 1:(j + 1) * SUBLANES, :]


def _cast_batches(pairs):
    jobs = []
    for src, dst in pairs:
        if len(src.shape) == 3:
            assert src.shape[1] == CAST_ROWS
            jobs += [(src.at[g], dst.at[g], src.shape[2]) for g in range(src.shape[0])]
        else:
            assert src.shape[0] % CAST_ROWS == 0
            jobs += [(src.at[pl.ds(r, CAST_ROWS), :], dst.at[pl.ds(r, CAST_ROWS), :], src.shape[1])
                     for r in range(0, src.shape[0], CAST_ROWS)]
    batches, cur, used = [], [], 0
    for s, d, cols in jobs:
        if used + cols > STAGE_COLS:
            batches.append(cur)
            cur, used = [], 0
        cur.append((s, d, used, cols))
        used += cols
    batches.append(cur)
    return batches


def _cast_weights(first_pairs, later_pairs, between, stage, sem):
    first = _cast_batches(first_pairs)
    batches = first + _cast_batches(later_pairs)
    hooks = {len(first) + 2 * i: fn for i, fn in enumerate(between)}
    assert not hooks or max(hooks) < len(batches)

    def copies(b, slot):
        return [pltpu.make_async_copy(s, stage.at[slot, :, pl.ds(lane, cols)], sem.at[slot])
                for s, _, lane, cols in batches[b]]

    for b in range(min(2, len(batches))):
        for cp in copies(b, b):
            cp.start()
    for b in range(len(batches)):
        slot = b % 2
        for cp in copies(b, slot):
            cp.wait()
        for _, d, lane, cols in batches[b]:
            d[...] = stage[slot, :, lane:lane + cols].astype(_BF16)
        if b + 2 < len(batches):
            for cp in copies(b + 2, slot):
                cp.start()
        if b in hooks:
            hooks[b]()


def _fused_kernel(x_hbm, xs_hbm, sp_hbm, sc_hbm, sf_hbm, meta_ref,
                  w_in_hbm, w_pool_hbm, wco_hbm, wo_hbm, wup_hbm, wdown_hbm,
                  pscale, wconv, wfc, g1, g2, g3, g4,
                  y_hbm, ys_hbm, op_ref, oc_ref, of_ref, spo_hbm, cs_ref, sfo_hbm,
                  w_in, w_pool, wco, wo, wup, wdown, stage, sp_v, sc_v, m_ubuf, m_cbuf, m_upbuf,
                  s_mbuf, s_actbuf, s_h2, m_h2, xs_v, ys_v, us_v, s_up, sf_ring, xbuf, ybuf, in_sem, out_sem, cast_sem, state_sem,
                  ubuf, cbuf, upbuf, utail, ctail, ftail, mbuf, actbuf, h2buf, x1_new, x1_old,
                  ubuf_first, cbuf_first, mbuf_first, upbuf_last, actbuf_last,
                  *, tm, nt, total):
    w = (w_in, w_pool, pscale, wconv, wco, wo, wup, wfc, wdown, g1, g2, g3, g4)
    seg = tm // SUBLANES
    n = pl.program_id(0)
    slot = n % 2

    def x_copies(tile, sl):
        b, t0 = tile // nt, (tile % nt) * tm
        return [pltpu.make_async_copy(x_hbm.at[b, pl.ds(t0 + s * seg, seg), :],
                                      xbuf.at[sl, :, s, :], in_sem.at[sl]) for s in range(SUBLANES)]

    def y_copies(tile, sl):
        b, t0 = tile // nt, (tile % nt) * tm
        return [pltpu.make_async_copy(ybuf.at[sl, :, s, :],
                                      y_hbm.at[b, pl.ds(t0 + s * seg, seg), :], out_sem.at[sl])
                for s in range(SUBLANES)]

    group = sf_ring.shape[1]
    n_groups = sp_v.shape[0] // group
    pool_out_cp = pltpu.make_async_copy(sp_v, spo_hbm.at[0], state_sem.at[6])

    def ffn_in_copy(g):
        return pltpu.make_async_copy(sf_hbm.at[0, pl.ds(g * group, group)], sf_ring.at[g % 2],
                                     state_sem.at[2 + g % 2])

    def ffn_out_copy(g):
        return pltpu.make_async_copy(sf_ring.at[g % 2], sfo_hbm.at[0, pl.ds(g * group, group)],
                                     state_sem.at[7 + g % 2])

    def wait_tile(buf, sem):
        pltpu.make_async_copy(buf, buf, sem).wait()

    def start_sequence_mixer_history():
        _fill_tail(utail, m_ubuf, POOL_HALO + N_META - POOL_BUF, POOL_BUF)
        _fill_tail(ctail, m_cbuf, CONV_HALO + N_META - (CONV_W - 1), CONV_W - 1)

    def token_mixer(work):
        ub, cb, mb = work
        hist = _SegmentHistory(ub, cb, None, utail, ctail, None, tm)
        x = xbuf[slot].reshape(tm, D_MODEL)
        _mixer(x, w, hist, mb, _full_windows)
        x1 = _mixer_residual(x, _mixer_out(w, mb), w)
        x1_new[...] = x1
        h2buf[...] = _ffn_norm(x1, w)

    def channel_mixer_up(work):
        upb, ab = work
        _ffn_up(h2buf[...], w, _SegmentHistory(None, None, upb, None, None, ftail, tm), ab)

    def channel_mixer_out(work, x1_ref):
        y = _ffn_residual(x1_ref[...], _ffn_down(w, work[1]), w)
        ybuf[1 - slot] = y.reshape(seg, SUBLANES, D_MODEL)

    def prefetch_next_tile():
        for cp in x_copies(jnp.minimum(n + 1, total - 1), 1 - slot):
            cp.start()

    def send_previous_tile():
        for cp in y_copies(n - 1, 1 - slot):
            cp.start()

    @pl.when(n == 0)
    def _():
        for cp in x_copies(n, slot):
            cp.start()
        pool_cp = pltpu.make_async_copy(sp_hbm.at[0], sp_v, state_sem.at[0])
        conv_cp = pltpu.make_async_copy(sc_hbm, sc_v, state_sem.at[1])
        xs_cp = pltpu.make_async_copy(xs_hbm.at[:, 0, :], xs_v, state_sem.at[4])
        ys_cp = pltpu.make_async_copy(ys_v, ys_hbm.at[:, 0, :], state_sem.at[5])
        pool_cp.start()
        conv_cp.start()
        xs_cp.start()

        m_ubuf[0:POOL_HALO, :] = jnp.zeros((POOL_HALO, POOL_WIDTH), _F32)
        m_cbuf[0:CONV_HALO, :] = jnp.zeros((CONV_HALO, CONV_WIDTH), _F32)
        m_upbuf[0:CONV_HALO, :] = jnp.zeros((CONV_HALO, 2 * D_FF), _F32)
        row1 = lax.broadcasted_iota(jnp.int32, (N_META, POOL_GC), 0) + 1

        def meta_windows(s, win):
            return s / jnp.minimum(win, row1).astype(_F32)

        meta_hist = _RowHistory(m_ubuf, m_cbuf, m_upbuf, N_META)
        meta_rows = pl.ds(0, N_META)

        def meta_token_mixer():
            _mixer(meta_ref[...], w, meta_hist, s_mbuf.at[meta_rows], meta_windows)
            x1 = _mixer_residual(meta_ref[...], _mixer_out(w, s_mbuf.at[meta_rows]), w)
            m_h2[...] = _ffn_norm(x1, w)

        sample_hist = _StateHistory(sp_v, sc_v, stage, us_v, cs_ref, s_up)

        def sample_token_mixer():
            pool_cp.wait()
            conv_cp.wait()
            xs_cp.wait()
            _mixer(xs_v[...], w, sample_hist, s_mbuf, _full_windows)
            x1 = _mixer_residual(xs_v[...], _mixer_out(w, s_mbuf), w)
            ys_v[...] = x1
            s_h2[...] = _ffn_norm(x1, w)
            for j in range(POOL_BUF - 1):
                sp_v[:, j, :] = sp_v[:, j + 1, :]
            sp_v[:, POOL_BUF - 1, :] = us_v[...]
            pool_out_cp.start()

        def first_tile_token_mixer():
            wait_tile(xbuf.at[slot], in_sem.at[slot])
            prefetch_next_tile()
            start_sequence_mixer_history()
            token_mixer((ubuf_first, cbuf_first, mbuf_first))

        _cast_weights([(w_in_hbm, w_in), (w_pool_hbm, w_pool), (wco_hbm, wco), (wo_hbm, wo)],
                      [(wup_hbm, wup), (wdown_hbm, wdown)],
                      [meta_token_mixer, sample_token_mixer, first_tile_token_mixer],
                      stage, cast_sem)
        ffn_in_copy(0).start()
        for g in range(n_groups):
            if g + 1 < n_groups:
                ffn_in_copy(g + 1).start()
            ffn_in_copy(g).wait()
            for k in range(CONV_W - 1):
                stage[k, g * group:(g + 1) * group, :] = sf_ring[g % 2, :, k, :]

        _ffn_up(m_h2[...], w, meta_hist, s_actbuf.at[meta_rows])
        _ffn_up(s_h2[...], w, sample_hist, s_actbuf)
        ys_v[...] = _ffn_residual(ys_v[...], _ffn_down(w, s_actbuf), w)
        ys_cp.start()

        for g in range(n_groups):
            if g >= 2:
                ffn_out_copy(g - 2).wait()
            rows = slice(g * group, (g + 1) * group)
            for k in range(1, CONV_W - 1):
                sf_ring[g % 2, :, k - 1, :] = stage[k, rows, :]
            sf_ring[g % 2, :, CONV_W - 2, :] = s_up[rows, :]
            ffn_out_copy(g).start()
        ys_cp.wait()

    @pl.when((n > 0) & (n < total))
    def _():
        wait_tile(xbuf.at[slot], in_sem.at[slot])

    @pl.when(n >= 3)
    def _():
        wait_tile(ybuf.at[1 - slot], out_sem.at[1 - slot])

    @pl.when((n % nt == 0) & (n > 0))
    def _():
        start_sequence_mixer_history()

    @pl.when(n % nt == 1)
    def _():
        _fill_tail(ftail, m_upbuf, CONV_HALO + N_META - (CONV_W - 1), CONV_W - 1)

    @pl.when((n > 0) & (n < total))
    def _():
        prefetch_next_tile()
        x1_old[...] = x1_new[...]
        channel_mixer_up((upbuf, actbuf))
        token_mixer((ubuf, cbuf, mbuf))
        channel_mixer_out((upbuf, actbuf), x1_old)
        send_previous_tile()

    @pl.when(n == total)
    def _():
        channel_mixer_up((upbuf_last, actbuf_last))
        channel_mixer_out((upbuf_last, actbuf_last), x1_new)
        send_previous_tile()

    @pl.when((n % nt == nt - 1) & (n < total))
    def _():
        _read_tail(op_ref, utail, POOL_BUF)
        _read_tail(oc_ref, ctail, CONV_W - 1)

    @pl.when((n % nt == 0) & (n >= 1))
    def _():
        _read_tail(of_ref, ftail, CONV_W - 1)

    @pl.when(n == total)
    def _():
        wait_tile(ybuf.at[slot], out_sem.at[slot])
        wait_tile(ybuf.at[1 - slot], out_sem.at[1 - slot])
        wait_tile(xbuf.at[slot], in_sem.at[slot])
        pool_out_cp.wait()
        for g in range(max(n_groups - 2, 0), n_groups):
            ffn_out_copy(g).wait()


def _resident(shape):
    zeros = (0,) * len(shape)
    return pl.BlockSpec(shape, lambda *_: zeros, pipeline_mode=pl.Buffered(1))


def _layer_slice(shape, l):
    return pl.BlockSpec((None,) + tuple(shape[1:]), lambda *_: (l, 0, 0), pipeline_mode=pl.Buffered(1))


def _fused_call(x, xs, sp, sc, sf, meta, big_weights, pscale, w_conv, w_ffn_conv, gains, l, tm):
    nb, t, d = x.shape
    ns = xs.shape[0]
    seg = tm // SUBLANES
    assert t % tm == 0 and tm % SUBLANES == 0 and seg >= POOL_BUF
    assert xs.shape == (ns, 1, d) and ns == CAST_ROWS and ns % SF_GROUP == 0
    assert sf.shape == (1, ns, CONV_W - 1, STAGE_COLS) and CONV_W - 1 == 2
    assert sp.shape == (1, ns, POOL_BUF, POOL_WIDTH) and sc.shape == (CONV_W - 1, ns, CONV_WIDTH)
    nt = t // tm
    assert nt >= 2
    total = nb * nt
    mix_state = lambda rows, c: pl.BlockSpec(
        (None, None, rows, c), lambda n: (0, jnp.minimum(n, total - 1) // nt, 0, 0))
    ffn_state = lambda rows, c: pl.BlockSpec(
        (None, None, rows, c), lambda n: (0, jnp.maximum(n - 1, 0) // nt, 0, 0))
    hbm = pl.BlockSpec(memory_space=pl.ANY)
    rows_with_head = lambda depth: (depth + seg) * SUBLANES
    return pl.pallas_call(
        functools.partial(_fused_kernel, tm=tm, nt=nt, total=total),
        grid=(total + 1,),
        in_specs=[hbm, hbm, hbm, hbm, hbm, _resident(meta.shape)]
                 + [hbm] * len(big_weights)
                 + [_resident(pscale.shape), _layer_slice(w_conv.shape, l),
                    _layer_slice(w_ffn_conv.shape, l)]
                 + [_resident(g.shape) for g in gains],
        out_specs=[hbm, hbm,
                   mix_state(POOL_BUF, POOL_WIDTH), mix_state(CONV_W - 1, CONV_WIDTH),
                   ffn_state(CONV_W - 1, 2 * D_FF),
                   hbm, _resident((ns, CONV_WIDTH)), hbm],
        out_shape=[jax.ShapeDtypeStruct((nb, t, d), _F32),
                   jax.ShapeDtypeStruct((ns, 1, d), _F32),
                   jax.ShapeDtypeStruct((1, nb, POOL_BUF, POOL_WIDTH), _F32),
                   jax.ShapeDtypeStruct((1, nb, CONV_W - 1, CONV_WIDTH), _F32),
                   jax.ShapeDtypeStruct((1, nb, CONV_W - 1, 2 * D_FF), _F32),
                   jax.ShapeDtypeStruct(sp.shape, _F32),
                   jax.ShapeDtypeStruct((ns, CONV_WIDTH), _F32),
                   jax.ShapeDtypeStruct(sf.shape, _F32)],
        scratch_shapes=[pltpu.VMEM((d, IN_TOTAL), _BF16),
                        pltpu.VMEM((N_POOL_GROUPS, POOL_GC, POOL_OUT_GC), _BF16),
                        pltpu.VMEM((CONV_WIDTH, d), _BF16),
                        pltpu.VMEM((d, d), _BF16),
                        pltpu.VMEM((d, 2 * D_FF), _BF16),
                        pltpu.VMEM((D_FF, d), _BF16),
                        pltpu.VMEM((2, CAST_ROWS, STAGE_COLS), _F32),
                        pltpu.VMEM(sp.shape[1:], _F32),
                        pltpu.VMEM(sc.shape, _F32),
                        pltpu.VMEM((POOL_HALO + N_META, POOL_WIDTH), _F32),
                        pltpu.VMEM((CONV_HALO + N_META, CONV_WIDTH), _F32),
                        pltpu.VMEM((CONV_HALO + N_META, 2 * D_FF), _F32),
                        pltpu.VMEM((ns, d), _BF16),
                        pltpu.VMEM((ns, D_FF), _BF16),
                        pltpu.VMEM((ns, d), _BF16),
                        pltpu.VMEM((N_META, d), _BF16),
                        pltpu.VMEM((ns, d), _F32),
                        pltpu.VMEM((ns, d), _F32),
                        pltpu.VMEM((ns, POOL_WIDTH), _F32),
                        pltpu.VMEM((ns, 2 * D_FF), _F32),
                        pltpu.VMEM((2, SF_GROUP, CONV_W - 1, 2 * D_FF), _F32),
                        pltpu.VMEM((2, seg, SUBLANES, d), _F32),
                        pltpu.VMEM((2, seg, SUBLANES, d), _F32),
                        pltpu.SemaphoreType.DMA((2,)),
                        pltpu.SemaphoreType.DMA((2,)),
                        pltpu.SemaphoreType.DMA((2,)),
                        pltpu.SemaphoreType.DMA((9,)),
                        pltpu.VMEM((rows_with_head(POOL_BUF), POOL_WIDTH), _F32),
                        pltpu.VMEM((rows_with_head(CONV_W - 1), CONV_WIDTH), _F32),
                        pltpu.VMEM((rows_with_head(CONV_W - 1), 2 * D_FF), _F32),
                        pltpu.VMEM((POOL_BUF * SUBLANES, POOL_WIDTH), _F32),
                        pltpu.VMEM(((CONV_W - 1) * SUBLANES, CONV_WIDTH), _F32),
                        pltpu.VMEM(((CONV_W - 1) * SUBLANES, 2 * D_FF), _F32),
                        pltpu.VMEM((tm, d), _BF16),
                        pltpu.VMEM((tm, D_FF), _BF16),
                        pltpu.VMEM((tm, d), _BF16),
                        pltpu.VMEM((tm, d), _F32),
                        pltpu.VMEM((tm, d), _F32),
                        pltpu.VMEM((rows_with_head(POOL_BUF), POOL_WIDTH), _F32),
                        pltpu.VMEM((rows_with_head(CONV_W - 1), CONV_WIDTH), _F32),
                        pltpu.VMEM((tm, d), _BF16),
                        pltpu.VMEM((rows_with_head(CONV_W - 1), 2 * D_FF), _F32),
                        pltpu.VMEM((tm, D_FF), _BF16)],
        compiler_params=pltpu.CompilerParams(
            dimension_semantics=("arbitrary",), vmem_limit_bytes=VMEM_LIMIT_BYTES),
        name="hybrid_step",
    )(x, xs, sp, sc, sf, meta, *big_weights, pscale, w_conv, w_ffn_conv, *gains)


def kernel(x_prompt, x_sample, state_pool, state_conv, state_ffn, meta_tokens, w_in, w_pool,
           pool_scale, w_conv, w_conv_out, w_o, w_up, w_ffn_conv, w_down, g_pre_mix,
           g_post_mix, g_pre_ffn, g_post_ffn):
    depth = w_in.shape[0]
    assert depth == 1, "the prompt / sample streams are chained for a single layer only"
    assert x_sample.shape[1] == 1 and meta_tokens.shape[0] == N_META
    assert N_META >= max(POOL_WINDOWS)
    row = lambda a: a.reshape(1, -1)
    l = 0
    big = (w_in[l], w_pool[l], w_conv_out[l], w_o[l], w_up[l], w_down[l])
    gains = (row(g_pre_mix[l]), row(g_post_mix[l]), row(g_pre_ffn[l]), row(g_post_ffn[l]))

    y_prompt, y_sample, pp, pc, pf, sp_new, cs, sf_new = _fused_call(
        x_prompt, x_sample, state_pool, jnp.transpose(state_conv[l], (1, 0, 2)), state_ffn,
        meta_tokens.astype(_F32), big, row(pool_scale[l]), w_conv, w_ffn_conv, gains, l, TM_PROMPT)
    sc_new = jnp.concatenate([state_conv[l][:, 1:], cs[:, None, :]], axis=1)[None]

    return (y_prompt, y_sample, pp, pc, pf, sp_new, sc_new, sf_new)
```

```python
import functools

import jax
import jax.numpy as jnp
from jax import lax
from jax.experimental import pallas as pl
from jax.experimental.pallas import tpu as pltpu

D_MODEL = 1024
N_META = 16
POOL_WINDOWS = (2, 4, 8, 16)
N_POOL_GROUPS = len(POOL_WINDOWS)
POOL_WIDTH = D_MODEL // 2
POOL_GC = POOL_WIDTH // N_POOL_GROUPS
POOL_OUT_GC = D_MODEL // N_POOL_GROUPS
POOL_BUF = max(POOL_WINDOWS) - 1
CONV_WIDTH = D_MODEL // 2
CONV_W = 3
D_FF = 2816
IN_TOTAL = 2 * POOL_WIDTH + 2 * CONV_WIDTH + 2 * D_MODEL
EPS = 1e-6
LOG2_E = 1.4426950408889634
OFF_U, OFF_V, OFF_B, OFF_C, OFF_GA, OFF_GB = 0, 512, 1024, 1536, 2048, 3072

SUBLANES = 8
FF_CHUNK = 256
POOL_HALO = 16
CONV_HALO = 8
TM_PROMPT = 256
CAST_ROWS = 128
SF_GROUP = 8
STAGE_COLS = 2 * D_FF
VMEM_LIMIT_BYTES = 58 * 1024 * 1024

_BF16 = jnp.bfloat16
_F32 = jnp.float32


def _dot(a, b):
    return jnp.dot(a, b, preferred_element_type=_F32)


def _rms(x, g):
    ms = jnp.mean(x * x, axis=-1, keepdims=True)
    return x * lax.rsqrt(ms + EPS) * g


def _sigmoid(x):
    return 1.0 / (1.0 + jnp.exp2(x * (-LOG2_E)))


def _mixer(x, w, hist, mbuf, pool_div):
    (w_in, w_pool, pscale, wconv, wco, _, _, _, _, g1, _, _, _) = w
    h = _rms(x, g1[...]).astype(_BF16)

    hist.put_u(_dot(h, w_in[:, OFF_U:OFF_U + POOL_WIDTH]))
    v = _dot(h, w_in[:, OFF_V:OFF_V + CONV_WIDTH])
    cg = _dot(h, w_in[:, OFF_C:OFF_C + CONV_WIDTH])
    hist.put_c(cg * v)
    cv = wconv[CONV_W - 1:CONV_W, :] * hist.get_c(0)
    for k in range(1, CONV_W):
        cv = cv + wconv[CONV_W - 1 - k:CONV_W - k, :] * hist.get_c(k)
    bg = _dot(h, w_in[:, OFF_B:OFF_B + CONV_WIDTH])
    b_in = (bg * cv).astype(_BF16)

    for g, win in enumerate(POOL_WINDOWS):
        cs = slice(g * POOL_GC, (g + 1) * POOL_GC)
        os_ = slice(g * POOL_OUT_GC, (g + 1) * POOL_OUT_GC)
        ug = hist.get_u(0, cs)
        s = ug
        for k in range(1, win):
            s = s + hist.get_u(k, cs)
        pooled = (pool_div(s, win) - ug).astype(_BF16)
        a = _dot(pooled, w_pool[g]) * pscale[:, os_]
        ga = _dot(h, w_in[:, OFF_GA + g * POOL_OUT_GC:OFF_GA + (g + 1) * POOL_OUT_GC])
        gb = _dot(h, w_in[:, OFF_GB + g * POOL_OUT_GC:OFF_GB + (g + 1) * POOL_OUT_GC])
        b = _dot(b_in, wco[:, os_])
        mbuf[:, os_] = (_sigmoid(ga) * a + _sigmoid(gb) * b).astype(_BF16)


def _mixer_out(w, mbuf, rows=slice(None)):
    return _dot(mbuf[rows, :], w[5][...])


def _mixer_residual(x, mixed, w):
    return x + _rms(mixed, w[10][...])


def _ffn_norm(x1, w):
    return _rms(x1, w[11][...]).astype(_BF16)


def _ffn_up(h2, w, hist, actbuf):
    wup, wfc = w[6], w[7]
    for c in range(D_FF // FF_CHUNK):
        conv = []
        for col in (c * FF_CHUNK, D_FF + c * FF_CHUNK):
            cols = slice(col, col + FF_CHUNK)
            hist.put_up(_dot(h2, wup[:, cols]), cols)
            y = wfc[CONV_W - 1:CONV_W, cols] * hist.get_up(0, cols)
            for k in range(1, CONV_W):
                y = y + wfc[CONV_W - 1 - k:CONV_W - k, cols] * hist.get_up(k, cols)
            conv.append(y)
        gate, val = conv
        actbuf[:, c * FF_CHUNK:(c + 1) * FF_CHUNK] = (gate * _sigmoid(gate) * val).astype(_BF16)


def _ffn_down(w, actbuf, rows=slice(None)):
    return _dot(actbuf[rows, :], w[8][...])


def _ffn_residual(x1, down, w):
    return x1 + _rms(down, w[12][...])


def _full_windows(s, win):
    return s * (1.0 / win)


class _RowHistory:
    def __init__(self, ubuf, cbuf, upbuf, m):
        self.ubuf, self.cbuf, self.upbuf, self.m = ubuf, cbuf, upbuf, m

    def put_u(self, val):
        self.ubuf[POOL_HALO:POOL_HALO + self.m, :] = val

    def get_u(self, k, cs):
        return self.ubuf[POOL_HALO - k:POOL_HALO - k + self.m, cs]

    def put_c(self, val):
        self.cbuf[CONV_HALO:CONV_HALO + self.m, :] = val

    def get_c(self, k):
        return self.cbuf[CONV_HALO - k:CONV_HALO - k + self.m, :]

    def put_up(self, val, cols):
        self.upbuf[CONV_HALO:CONV_HALO + self.m, cols] = val

    def get_up(self, k, cols):
        return self.upbuf[CONV_HALO - k:CONV_HALO - k + self.m, cols]


class _SegmentHistory:
    def __init__(self, ubuf, cbuf, upbuf, utail, ctail, ftail, tm):
        self.ubuf, self.cbuf, self.upbuf, self.tm = ubuf, cbuf, upbuf, tm
        self.utail, self.ctail, self.ftail = utail, ctail, ftail

    def _put(self, buf, tail, val, cols, depth):
        tm = self.tm
        buf[depth * SUBLANES:depth * SUBLANES + tm, cols] = val
        sublane = lax.broadcasted_iota(jnp.int32, (SUBLANES, val.shape[1]), 0)
        for j in range(depth):
            rows = slice(j * SUBLANES, (j + 1) * SUBLANES)
            cur = val[tm - (depth - j) * SUBLANES:tm - (depth - j - 1) * SUBLANES, :]
            mixed = jnp.where(sublane == SUBLANES - 1, tail[rows, cols], cur)
            buf[rows, cols] = pltpu.roll(mixed, 1, axis=0)
            tail[rows, cols] = cur

    def _get(self, buf, k, cols, depth):
        start = (depth - k) * SUBLANES
        return buf[start:start + self.tm, cols]

    def put_u(self, val):
        self._put(self.ubuf, self.utail, val, slice(None), POOL_BUF)

    def get_u(self, k, cs):
        return self._get(self.ubuf, k, cs, POOL_BUF)

    def put_c(self, val):
        self._put(self.cbuf, self.ctail, val, slice(None), CONV_W - 1)

    def get_c(self, k):
        return self._get(self.cbuf, k, slice(None), CONV_W - 1)

    def put_up(self, val, cols):
        self._put(self.upbuf, self.ftail, val, cols, CONV_W - 1)

    def get_up(self, k, cols):
        return self._get(self.upbuf, k, cols, CONV_W - 1)


class _StateHistory:
    def __init__(self, sp_ref, sc_ref, sf_ref, u_ref, c_ref, f_ref):
        self.sp, self.sc, self.sf = sp_ref, sc_ref, sf_ref
        self.u, self.c, self.f = u_ref, c_ref, f_ref

    def put_u(self, val):
        self.u[...] = val

    def get_u(self, k, cs):
        return self.u[:, cs] if k == 0 else self.sp[POOL_BUF - k, :, cs]

    def put_c(self, val):
        self.c[...] = val

    def get_c(self, k):
        return self.c[...] if k == 0 else self.sc[:, CONV_W - 1 - k, :]

    def put_up(self, val, cols):
        self.f[:, cols] = val

    def get_up(self, k, cols):
        return self.f[:, cols] if k == 0 else self.sf[CONV_W - 1 - k, :, cols]


def _fill_tail(tail, state_ref, first_row, depth):
    for j in range(depth):
        r = first_row + j
        tail[j * SUBLANES:(j + 1) * SUBLANES, :] = jnp.broadcast_to(
            state_ref[r:r + 1, :], (SUBLANES, tail.shape[1]))


def _read_tail(out_ref, tail, depth):
    for j in range(depth):
        out_ref[j:j + 1, :] = tail[(j + 1) * SUBLANES - 1:(j + 1) * SUBLANES, :]


def _cast_batches(pairs):
    jobs = []
    for src, dst in pairs:
        if len(src.shape) == 3:
            assert src.shape[1] == CAST_ROWS
            jobs += [(src.at[g], dst.at[g], src.shape[2]) for g in range(src.shape[0])]
        else:
            assert src.shape[0] % CAST_ROWS == 0
            jobs += [(src.at[pl.ds(r, CAST_ROWS), :], dst.at[pl.ds(r, CAST_ROWS), :], src.shape[1])
                     for r in range(0, src.shape[0], CAST_ROWS)]
    batches, cur, used = [], [], 0
    for s, d, cols in jobs:
        if used + cols > STAGE_COLS:
            batches.append(cur)
            cur, used = [], 0
        cur.append((s, d, used, cols))
        used += cols
    batches.append(cur)
    return batches


def _cast_weights(first_pairs, later_pairs, between, stage, sem):
    first = _cast_batches(first_pairs)
    batches = first + _cast_batches(later_pairs)
    hooks = {len(first) + 2 * i: fn for i, fn in enumerate(between)}
    assert not hooks or max(hooks) < len(batches)

    def copies(b, slot):
        return [pltpu.make_async_copy(s, stage.at[slot, :, pl.ds(lane, cols)], sem.at[slot])
                for s, _, lane, cols in batches[b]]

    for b in range(min(2, len(batches))):
        for cp in copies(b, b):
            cp.start()
    for b in range(len(batches)):
        slot = b % 2
        for cp in copies(b, slot):
            cp.wait()
        for _, d, lane, cols in batches[b]:
            d[...] = stage[slot, :, lane:lane + cols].astype(_BF16)
        if b + 2 < len(batches):
            for cp in copies(b + 2, slot):
                cp.start()
        if b in hooks:
            hooks[b]()


def _fused_kernel(x_hbm, xs_hbm, sp_hbm, sc_hbm, sf_hbm, meta_ref,
                  w_in_hbm, w_pool_hbm, wco_hbm, wo_hbm, wup_hbm, wdown_hbm,
                  pscale, wconv_hbm, wfc_hbm, g1, g2, g3, g4,
                  y_hbm, ys_hbm, op_ref, oc_ref, of_ref, spo_hbm, sco_hbm, sfo_hbm,
                  w_in, w_pool, wco, wo, wup, wdown, wconv, wfc, stage, sp_v, sc_v,
                  m_ubuf, m_cbuf, m_upbuf,
                  s_mbuf, s_actbuf, s_h2, m_h2, xs_v, ys_v, us_v, cs_v, s_up, sf_ring, xbuf, ybuf, in_sem, out_sem, cast_sem, state_sem,
                  ubuf, cbuf, upbuf, utail, ctail, ftail, mbuf, actbuf, h2buf, x1_new, x1_old,
                  ubuf_first, cbuf_first, mbuf_first, upbuf_last, actbuf_last,
                  *, tm, nt, total, layer):
    w = (w_in, w_pool, pscale, wconv, wco, wo, wup, wfc, wdown, g1, g2, g3, g4)
    seg = tm // SUBLANES
    n = pl.program_id(0)
    slot = n % 2

    def x_copies(tile, sl):
        b, t0 = tile // nt, (tile % nt) * tm
        return [pltpu.make_async_copy(x_hbm.at[b, pl.ds(t0 + s * seg, seg), :],
                                      xbuf.at[sl, :, s, :], in_sem.at[sl]) for s in range(SUBLANES)]

    def y_copies(tile, sl):
        b, t0 = tile // nt, (tile % nt) * tm
        return [pltpu.make_async_copy(ybuf.at[sl, :, s, :],
                                      y_hbm.at[b, pl.ds(t0 + s * seg, seg), :], out_sem.at[sl])
                for s in range(SUBLANES)]

    group = sf_ring.shape[1]
    n_groups = xs_v.shape[0] // group
    pool_out_cps = [
        pltpu.make_async_copy(sp_hbm.at[pl.ds(1, POOL_BUF - 1)], spo_hbm.at[pl.ds(0, POOL_BUF - 1)],
                              state_sem.at[6]),
        pltpu.make_async_copy(us_v, spo_hbm.at[POOL_BUF - 1], state_sem.at[6])]
    conv_out_cp = pltpu.make_async_copy(sc_v, sco_hbm.at[layer], state_sem.at[9])

    def ffn_in_copy(g):
        return pltpu.make_async_copy(sf_hbm.at[layer, pl.ds(g * group, group)], sf_ring.at[g % 2],
                                     state_sem.at[2 + g % 2])

    def ffn_out_copy(g):
        return pltpu.make_async_copy(sf_ring.at[g % 2], sfo_hbm.at[layer, pl.ds(g * group, group)],
                                     state_sem.at[7 + g % 2])

    def wait_tile(buf, sem):
        pltpu.make_async_copy(buf, buf, sem).wait()

    def start_sequence_mixer_history():
        _fill_tail(utail, m_ubuf, POOL_HALO + N_META - POOL_BUF, POOL_BUF)
        _fill_tail(ctail, m_cbuf, CONV_HALO + N_META - (CONV_W - 1), CONV_W - 1)

    def token_mixer(work):
        ub, cb, mb = work
        hist = _SegmentHistory(ub, cb, None, utail, ctail, None, tm)
        x = xbuf[slot].reshape(tm, D_MODEL)
        _mixer(x, w, hist, mb, _full_windows)
        x1 = _mixer_residual(x, _mixer_out(w, mb), w)
        x1_new[...] = x1
        h2buf[...] = _ffn_norm(x1, w)

    def channel_mixer_up(work):
        upb, ab = work
        _ffn_up(h2buf[...], w, _SegmentHistory(None, None, upb, None, None, ftail, tm), ab)

    def channel_mixer_out(work, x1_ref):
        y = _ffn_residual(x1_ref[...], _ffn_down(w, work[1]), w)
        ybuf[1 - slot] = y.reshape(seg, SUBLANES, D_MODEL)

    def prefetch_next_tile():
        for cp in x_copies(jnp.minimum(n + 1, total - 1), 1 - slot):
            cp.start()

    def send_previous_tile():
        for cp in y_copies(n - 1, 1 - slot):
            cp.start()

    @pl.when(n == 0)
    def _():
        for cp in x_copies(n, slot):
            cp.start()
        pool_cp = pltpu.make_async_copy(sp_hbm, sp_v, state_sem.at[0])
        conv_cp = pltpu.make_async_copy(sc_hbm.at[layer], sc_v, state_sem.at[1])
        wconv_cp = pltpu.make_async_copy(wconv_hbm.at[:, layer, :], wconv, state_sem.at[10])
        wfc_cp = pltpu.make_async_copy(wfc_hbm.at[:, layer, :], wfc, state_sem.at[10])
        wconv_cp.start()
        wfc_cp.start()
        xs_cp = pltpu.make_async_copy(xs_hbm.at[:, 0, :], xs_v, state_sem.at[4])
        ys_cp = pltpu.make_async_copy(ys_v, ys_hbm.at[:, 0, :], state_sem.at[5])
        pool_cp.start()
        conv_cp.start()
        xs_cp.start()

        m_ubuf[0:POOL_HALO, :] = jnp.zeros((POOL_HALO, POOL_WIDTH), _F32)
        m_cbuf[0:CONV_HALO, :] = jnp.zeros((CONV_HALO, CONV_WIDTH), _F32)
        m_upbuf[0:CONV_HALO, :] = jnp.zeros((CONV_HALO, 2 * D_FF), _F32)
        row1 = lax.broadcasted_iota(jnp.int32, (N_META, POOL_GC), 0) + 1

        def meta_windows(s, win):
            return s / jnp.minimum(win, row1).astype(_F32)

        meta_hist = _RowHistory(m_ubuf, m_cbuf, m_upbuf, N_META)
        meta_rows = pl.ds(0, N_META)

        def meta_token_mixer():
            wconv_cp.wait()
            wfc_cp.wait()
            _mixer(meta_ref[...], w, meta_hist, s_mbuf.at[meta_rows], meta_windows)
            x1 = _mixer_residual(meta_ref[...], _mixer_out(w, s_mbuf.at[meta_rows]), w)
            m_h2[...] = _ffn_norm(x1, w)

        sample_hist = _StateHistory(sp_v, sc_v, stage, us_v, cs_v, s_up)

        def sample_token_mixer():
            pool_cp.wait()
            conv_cp.wait()
            xs_cp.wait()
            _mixer(xs_v[...], w, sample_hist, s_mbuf, _full_windows)
            x1 = _mixer_residual(xs_v[...], _mixer_out(w, s_mbuf), w)
            ys_v[...] = x1
            s_h2[...] = _ffn_norm(x1, w)
            for cp in pool_out_cps:
                cp.start()
            for j in range(CONV_W - 2):
                sc_v[:, j, :] = sc_v[:, j + 1, :]
            sc_v[:, CONV_W - 2, :] = cs_v[...]
            conv_out_cp.start()

        def first_tile_token_mixer():
            wait_tile(xbuf.at[slot], in_sem.at[slot])
            prefetch_next_tile()
            start_sequence_mixer_history()
            token_mixer((ubuf_first, cbuf_first, mbuf_first))

        _cast_weights([(w_in_hbm, w_in), (w_pool_hbm, w_pool), (wco_hbm, wco), (wo_hbm, wo)],
                      [(wup_hbm, wup), (wdown_hbm, wdown)],
                      [meta_token_mixer, sample_token_mixer, first_tile_token_mixer],
                      stage, cast_sem)
        ffn_in_copy(0).start()
        for g in range(n_groups):
            if g + 1 < n_groups:
                ffn_in_copy(g + 1).start()
            ffn_in_copy(g).wait()
            for k in range(CONV_W - 1):
                stage[k, g * group:(g + 1) * group, :] = sf_ring[g % 2, :, k, :]

        _ffn_up(m_h2[...], w, meta_hist, s_actbuf.at[meta_rows])
        _ffn_up(s_h2[...], w, sample_hist, s_actbuf)
        ys_v[...] = _ffn_residual(ys_v[...], _ffn_down(w, s_actbuf), w)
        ys_cp.start()

        for g in range(n_groups):
            if g >= 2:
                ffn_out_copy(g - 2).wait()
            rows = slice(g * group, (g + 1) * group)
            for k in range(1, CONV_W - 1):
                sf_ring[g % 2, :, k - 1, :] = stage[k, rows, :]
            sf_ring[g % 2, :, CONV_W - 2, :] = s_up[rows, :]
            ffn_out_copy(g).start()
        ys_cp.wait()

    @pl.when((n > 0) & (n < total))
    def _():
        wait_tile(xbuf.at[slot], in_sem.at[slot])

    @pl.when(n >= 3)
    def _():
        wait_tile(ybuf.at[1 - slot], out_sem.at[1 - slot])

    @pl.when((n % nt == 0) & (n > 0))
    def _():
        start_sequence_mixer_history()

    @pl.when(n % nt == 1)
    def _():
        _fill_tail(ftail, m_upbuf, CONV_HALO + N_META - (CONV_W - 1), CONV_W - 1)

    @pl.when((n > 0) & (n < total))
    def _():
        prefetch_next_tile()
        x1_old[...] = x1_new[...]
        channel_mixer_up((upbuf, actbuf))
        token_mixer((ubuf, cbuf, mbuf))
        channel_mixer_out((upbuf, actbuf), x1_old)
        send_previous_tile()

    @pl.when(n == total)
    def _():
        channel_mixer_up((upbuf_last, actbuf_last))
        channel_mixer_out((upbuf_last, actbuf_last), x1_new)
        send_previous_tile()

    @pl.when((n % nt == nt - 1) & (n < total))
    def _():
        for j in range(POOL_BUF):
            op_ref[j, pl.ds(n // nt, 1), :] = utail[(j + 1) * SUBLANES - 1:(j + 1) * SUBLANES, :]
        _read_tail(oc_ref, ctail, CONV_W - 1)

    @pl.when((n % nt == 0) & (n >= 1))
    def _():
        _read_tail(of_ref, ftail, CONV_W - 1)

    @pl.when(n == total)
    def _():
        wait_tile(ybuf.at[slot], out_sem.at[slot])
        wait_tile(ybuf.at[1 - slot], out_sem.at[1 - slot])
        wait_tile(xbuf.at[slot], in_sem.at[slot])
        for cp in pool_out_cps:
            cp.wait()
        conv_out_cp.wait()
        for g in range(max(n_groups - 2, 0), n_groups):
            ffn_out_copy(g).wait()


def _resident(shape):
    zeros = (0,) * len(shape)
    return pl.BlockSpec(shape, lambda *_: zeros, pipeline_mode=pl.Buffered(1))


def _fused_call(x, xs, sp, sc, sf, meta, big_weights, pscale, w_conv, w_ffn_conv, gains, l, tm):
    nb, t, d = x.shape
    ns = xs.shape[0]
    seg = tm // SUBLANES
    assert t % tm == 0 and tm % SUBLANES == 0 and seg >= POOL_BUF
    assert xs.shape == (ns, 1, d) and ns == CAST_ROWS and ns % SF_GROUP == 0
    assert sf.shape[1:] == (ns, CONV_W - 1, STAGE_COLS) and CONV_W - 1 == 2
    assert sp.shape == (POOL_BUF, ns, POOL_WIDTH) and sc.shape[1:] == (ns, CONV_W - 1, CONV_WIDTH)
    nt = t // tm
    assert nt >= 2
    total = nb * nt
    mix_state = lambda rows, c: pl.BlockSpec(
        (None, None, rows, c), lambda n: (0, jnp.minimum(n, total - 1) // nt, 0, 0))
    ffn_state = lambda rows, c: pl.BlockSpec(
        (None, None, rows, c), lambda n: (0, jnp.maximum(n - 1, 0) // nt, 0, 0))
    hbm = pl.BlockSpec(memory_space=pl.ANY)
    rows_with_head = lambda depth: (depth + seg) * SUBLANES
    return pl.pallas_call(
        functools.partial(_fused_kernel, tm=tm, nt=nt, total=total, layer=l),
        grid=(total + 1,),
        in_specs=[hbm, hbm, hbm, hbm, hbm, _resident(meta.shape)]
                 + [hbm] * len(big_weights)
                 + [_resident(pscale.shape), hbm, hbm]
                 + [_resident(g.shape) for g in gains],
        out_specs=[hbm, hbm,
                   _resident((POOL_BUF, nb, POOL_WIDTH)), mix_state(CONV_W - 1, CONV_WIDTH),
                   ffn_state(CONV_W - 1, 2 * D_FF),
                   hbm, hbm, hbm],
        out_shape=[jax.ShapeDtypeStruct((nb, t, d), _F32),
                   jax.ShapeDtypeStruct((ns, 1, d), _F32),
                   jax.ShapeDtypeStruct((POOL_BUF, nb, POOL_WIDTH), _F32),
                   jax.ShapeDtypeStruct((1, nb, CONV_W - 1, CONV_WIDTH), _F32),
                   jax.ShapeDtypeStruct((1, nb, CONV_W - 1, 2 * D_FF), _F32),
                   jax.ShapeDtypeStruct(sp.shape, _F32),
                   jax.ShapeDtypeStruct(sc.shape, _F32),
                   jax.ShapeDtypeStruct(sf.shape, _F32)],
        scratch_shapes=[pltpu.VMEM((d, IN_TOTAL), _BF16),
                        pltpu.VMEM((N_POOL_GROUPS, POOL_GC, POOL_OUT_GC), _BF16),
                        pltpu.VMEM((CONV_WIDTH, d), _BF16),
                        pltpu.VMEM((d, d), _BF16),
                        pltpu.VMEM((d, 2 * D_FF), _BF16),
                        pltpu.VMEM((D_FF, d), _BF16),
                        pltpu.VMEM((CONV_W, CONV_WIDTH), _F32),
                        pltpu.VMEM((CONV_W, 2 * D_FF), _F32),
                        pltpu.VMEM((2, CAST_ROWS, STAGE_COLS), _F32),
                        pltpu.VMEM(sp.shape, _F32),
                        pltpu.VMEM(sc.shape[1:], _F32),
                        pltpu.VMEM((POOL_HALO + N_META, POOL_WIDTH), _F32),
                        pltpu.VMEM((CONV_HALO + N_META, CONV_WIDTH), _F32),
                        pltpu.VMEM((CONV_HALO + N_META, 2 * D_FF), _F32),
                        pltpu.VMEM((ns, d), _BF16),
                        pltpu.VMEM((ns, D_FF), _BF16),
                        pltpu.VMEM((ns, d), _BF16),
                        pltpu.VMEM((N_META, d), _BF16),
                        pltpu.VMEM((ns, d), _F32),
                        pltpu.VMEM((ns, d), _F32),
                        pltpu.VMEM((ns, POOL_WIDTH), _F32),
                        pltpu.VMEM((ns, CONV_WIDTH), _F32),
                        pltpu.VMEM((ns, 2 * D_FF), _F32),
                        pltpu.VMEM((2, SF_GROUP, CONV_W - 1, 2 * D_FF), _F32),
                        pltpu.VMEM((2, seg, SUBLANES, d), _F32),
                        pltpu.VMEM((2, seg, SUBLANES, d), _F32),
                        pltpu.SemaphoreType.DMA((2,)),
                        pltpu.SemaphoreType.DMA((2,)),
                        pltpu.SemaphoreType.DMA((2,)),
                        pltpu.SemaphoreType.DMA((11,)),
                        pltpu.VMEM((rows_with_head(POOL_BUF), POOL_WIDTH), _F32),
                        pltpu.VMEM((rows_with_head(CONV_W - 1), CONV_WIDTH), _F32),
                        pltpu.VMEM((rows_with_head(CONV_W - 1), 2 * D_FF), _F32),
                        pltpu.VMEM((POOL_BUF * SUBLANES, POOL_WIDTH), _F32),
                        pltpu.VMEM(((CONV_W - 1) * SUBLANES, CONV_WIDTH), _F32),
                        pltpu.VMEM(((CONV_W - 1) * SUBLANES, 2 * D_FF), _F32),
                        pltpu.VMEM((tm, d), _BF16),
                        pltpu.VMEM((tm, D_FF), _BF16),
                        pltpu.VMEM((tm, d), _BF16),
                        pltpu.VMEM((tm, d), _F32),
                        pltpu.VMEM((tm, d), _F32),
                        pltpu.VMEM((rows_with_head(POOL_BUF), POOL_WIDTH), _F32),
                        pltpu.VMEM((rows_with_head(CONV_W - 1), CONV_WIDTH), _F32),
                        pltpu.VMEM((tm, d), _BF16),
                        pltpu.VMEM((rows_with_head(CONV_W - 1), 2 * D_FF), _F32),
                        pltpu.VMEM((tm, D_FF), _BF16)],
        compiler_params=pltpu.CompilerParams(
            dimension_semantics=("arbitrary",), vmem_limit_bytes=VMEM_LIMIT_BYTES),
        name="hybrid_step",
    )(x, xs, sp, sc, sf, meta, *big_weights, pscale, w_conv, w_ffn_conv, *gains)


def kernel(x_prompt, x_sample, state_pool, state_conv, state_ffn, meta_tokens, w_in, w_pool,
           pool_scale, w_conv, w_conv_out, w_o, w_up, w_ffn_conv, w_down, g_pre_mix,
           g_post_mix, g_pre_ffn, g_post_ffn):
    depth = w_in.shape[0]
    assert depth == 1, "the prompt / sample streams are chained for a single layer only"
    assert x_sample.shape[1] == 1 and meta_tokens.shape[0] == N_META
    assert N_META >= max(POOL_WINDOWS)
    row = lambda a: a.reshape(1, -1)
    l = 0
    big = (w_in[l], w_pool[l], w_conv_out[l], w_o[l], w_up[l], w_down[l])
    gains = (row(g_pre_mix[l]), row(g_post_mix[l]), row(g_pre_ffn[l]), row(g_post_ffn[l]))

    rows_first = lambda s: jnp.transpose(s, (1, 0, 2))
    y_prompt, y_sample, pp, pc, pf, sp_new, sc_new, sf_new = _fused_call(
        x_prompt, x_sample, rows_first(state_pool[l]), state_conv, state_ffn,
        meta_tokens.astype(_F32), big, row(pool_scale[l]), rows_first(w_conv),
        rows_first(w_ffn_conv), gains, l, TM_PROMPT)

    return (y_prompt, y_sample, rows_first(pp)[None], pc, pf, rows_first(sp_new)[None],
            sc_new, sf_new)
```

```python
import functools

import jax
import jax.numpy as jnp
from jax import lax
from jax.experimental import pallas as pl
from jax.experimental.pallas import tpu as pltpu

D_MODEL = 1024
N_META = 16
POOL_WINDOWS = (2, 4, 8, 16)
N_POOL_GROUPS = len(POOL_WINDOWS)
POOL_WIDTH = D_MODEL // 2
POOL_GC = POOL_WIDTH // N_POOL_GROUPS
POOL_OUT_GC = D_MODEL // N_POOL_GROUPS
POOL_BUF = max(POOL_WINDOWS) - 1
CONV_WIDTH = D_MODEL // 2
CONV_W = 3
D_FF = 2816
IN_TOTAL = 2 * POOL_WIDTH + 2 * CONV_WIDTH + 2 * D_MODEL
EPS = 1e-6
LOG2_E = 1.4426950408889634
OFF_U, OFF_V, OFF_B, OFF_C, OFF_GA, OFF_GB = 0, 512, 1024, 1536, 2048, 3072

SUBLANES = 8
FF_CHUNK = 256
POOL_HALO = 16
CONV_HALO = 8
TM_PROMPT = 256
CAST_ROWS = 128
SF_GROUP = 8
SF_SLOTS = 4
STAGE_COLS = 2 * D_FF
VMEM_LIMIT_BYTES = 58 * 1024 * 1024

_BF16 = jnp.bfloat16
_F32 = jnp.float32


def _dot(a, b):
    return jnp.dot(a, b, preferred_element_type=_F32)


def _rms(x, g):
    ms = jnp.mean(x * x, axis=-1, keepdims=True)
    return x * lax.rsqrt(ms + EPS) * g


def _sigmoid(x):
    return 1.0 / (1.0 + jnp.exp2(x * (-LOG2_E)))


def _mixer(x, w, hist, mbuf, pool_div):
    (w_in, w_pool, pscale, wconv, wco, _, _, _, _, g1, _, _, _) = w
    h = _rms(x, g1[...]).astype(_BF16)

    hist.put_u(_dot(h, w_in[:, OFF_U:OFF_U + POOL_WIDTH]))
    v = _dot(h, w_in[:, OFF_V:OFF_V + CONV_WIDTH])
    cg = _dot(h, w_in[:, OFF_C:OFF_C + CONV_WIDTH])
    hist.put_c(cg * v)
    cv = wconv[CONV_W - 1:CONV_W, :] * hist.get_c(0)
    for k in range(1, CONV_W):
        cv = cv + wconv[CONV_W - 1 - k:CONV_W - k, :] * hist.get_c(k)
    bg = _dot(h, w_in[:, OFF_B:OFF_B + CONV_WIDTH])
    b_in = (bg * cv).astype(_BF16)

    for g, win in enumerate(POOL_WINDOWS):
        cs = slice(g * POOL_GC, (g + 1) * POOL_GC)
        os_ = slice(g * POOL_OUT_GC, (g + 1) * POOL_OUT_GC)
        ug = hist.get_u(0, cs)
        s = ug
        for k in range(1, win):
            s = s + hist.get_u(k, cs)
        pooled = (pool_div(s, win) - ug).astype(_BF16)
        a = _dot(pooled, w_pool[g]) * pscale[:, os_]
        ga = _dot(h, w_in[:, OFF_GA + g * POOL_OUT_GC:OFF_GA + (g + 1) * POOL_OUT_GC])
        gb = _dot(h, w_in[:, OFF_GB + g * POOL_OUT_GC:OFF_GB + (g + 1) * POOL_OUT_GC])
        b = _dot(b_in, wco[:, os_])
        mbuf[:, os_] = (_sigmoid(ga) * a + _sigmoid(gb) * b).astype(_BF16)


def _mixer_out(w, mbuf, rows=slice(None)):
    return _dot(mbuf[rows, :], w[5][...])


def _mixer_residual(x, mixed, w):
    return x + _rms(mixed, w[10][...])


def _ffn_norm(x1, w):
    return _rms(x1, w[11][...]).astype(_BF16)


def _ffn_up(h2, w, hist, actbuf):
    wup, wfc = w[6], w[7]
    for c in range(D_FF // FF_CHUNK):
        conv = []
        for col in (c * FF_CHUNK, D_FF + c * FF_CHUNK):
            cols = slice(col, col + FF_CHUNK)
            hist.put_up(_dot(h2, wup[:, cols]), cols)
            y = wfc[CONV_W - 1:CONV_W, cols] * hist.get_up(0, cols)
            for k in range(1, CONV_W):
                y = y + wfc[CONV_W - 1 - k:CONV_W - k, cols] * hist.get_up(k, cols)
            conv.append(y)
        gate, val = conv
        actbuf[:, c * FF_CHUNK:(c + 1) * FF_CHUNK] = (gate * _sigmoid(gate) * val).astype(_BF16)


def _ffn_down(w, actbuf, rows=slice(None)):
    return _dot(actbuf[rows, :], w[8][...])


def _ffn_residual(x1, down, w):
    return x1 + _rms(down, w[12][...])


def _full_windows(s, win):
    return s * (1.0 / win)


class _RowHistory:
    def __init__(self, ubuf, cbuf, upbuf, m):
        self.ubuf, self.cbuf, self.upbuf, self.m = ubuf, cbuf, upbuf, m

    def put_u(self, val):
        self.ubuf[POOL_HALO:POOL_HALO + self.m, :] = val

    def get_u(self, k, cs):
        return self.ubuf[POOL_HALO - k:POOL_HALO - k + self.m, cs]

    def put_c(self, val):
        self.cbuf[CONV_HALO:CONV_HALO + self.m, :] = val

    def get_c(self, k):
        return self.cbuf[CONV_HALO - k:CONV_HALO - k + self.m, :]

    def put_up(self, val, cols):
        self.upbuf[CONV_HALO:CONV_HALO + self.m, cols] = val

    def get_up(self, k, cols):
        return self.upbuf[CONV_HALO - k:CONV_HALO - k + self.m, cols]


class _SegmentHistory:
    def __init__(self, ubuf, cbuf, upbuf, utail, ctail, ftail, tm):
        self.ubuf, self.cbuf, self.upbuf, self.tm = ubuf, cbuf, upbuf, tm
        self.utail, self.ctail, self.ftail = utail, ctail, ftail

    def _put(self, buf, tail, val, cols, depth):
        tm = self.tm
        buf[depth * SUBLANES:depth * SUBLANES + tm, cols] = val
        sublane = lax.broadcasted_iota(jnp.int32, (SUBLANES, val.shape[1]), 0)
        for j in range(depth):
            rows = slice(j * SUBLANES, (j + 1) * SUBLANES)
            cur = val[tm - (depth - j) * SUBLANES:tm - (depth - j - 1) * SUBLANES, :]
            mixed = jnp.where(sublane == SUBLANES - 1, tail[rows, cols], cur)
            buf[rows, cols] = pltpu.roll(mixed, 1, axis=0)
            tail[rows, cols] = cur

    def _get(self, buf, k, cols, depth):
        start = (depth - k) * SUBLANES
        return buf[start:start + self.tm, cols]

    def put_u(self, val):
        self._put(self.ubuf, self.utail, val, slice(None), POOL_BUF)

    def get_u(self, k, cs):
        return self._get(self.ubuf, k, cs, POOL_BUF)

    def put_c(self, val):
        self._put(self.cbuf, self.ctail, val, slice(None), CONV_W - 1)

    def get_c(self, k):
        return self._get(self.cbuf, k, slice(None), CONV_W - 1)

    def put_up(self, val, cols):
        self._put(self.upbuf, self.ftail, val, cols, CONV_W - 1)

    def get_up(self, k, cols):
        return self._get(self.upbuf, k, cols, CONV_W - 1)


class _StateHistory:
    def __init__(self, sp_ref, sc_ref, sf_ref, u_ref, c_ref, f_ref):
        self.sp, self.sc, self.sf = sp_ref, sc_ref, sf_ref
        self.u, self.c, self.f = u_ref, c_ref, f_ref

    def put_u(self, val):
        self.u[...] = val

    def get_u(self, k, cs):
        return self.u[:, cs] if k == 0 else self.sp[POOL_BUF - k, :, cs]

    def put_c(self, val):
        self.c[...] = val

    def get_c(self, k):
        return self.c[...] if k == 0 else self.sc[:, CONV_W - 1 - k, :]

    def put_up(self, val, cols):
        self.f[:, cols] = val

    def get_up(self, k, cols):
        return self.f[:, cols] if k == 0 else self.sf[CONV_W - 1 - k, :, cols]


def _fill_tail(tail, state_ref, first_row, depth):
    for j in range(depth):
        r = first_row + j
        tail[j * SUBLANES:(j + 1) * SUBLANES, :] = jnp.broadcast_to(
            state_ref[r:r + 1, :], (SUBLANES, tail.shape[1]))


def _read_tail(out_ref, tail, depth):
    for j in range(depth):
        out_ref[j:j + 1, :] = tail[(j + 1) * SUBLANES - 1:(j + 1) * SUBLANES, :]


def _cast_batches(pairs):
    jobs = []
    for src, dst in pairs:
        if len(src.shape) == 3:
            assert src.shape[1] == CAST_ROWS
            jobs += [(src.at[g], dst.at[g], src.shape[2]) for g in range(src.shape[0])]
        else:
            assert src.shape[0] % CAST_ROWS == 0
            jobs += [(src.at[pl.ds(r, CAST_ROWS), :], dst.at[pl.ds(r, CAST_ROWS), :], src.shape[1])
                     for r in range(0, src.shape[0], CAST_ROWS)]
    batches, cur, used = [], [], 0
    for s, d, cols in jobs:
        if used + cols > STAGE_COLS:
            batches.append(cur)
            cur, used = [], 0
        cur.append((s, d, used, cols))
        used += cols
    batches.append(cur)
    return batches


def _cast_weights(first_pairs, later_pairs, between, stage, sem):
    first = _cast_batches(first_pairs)
    batches = first + _cast_batches(later_pairs)
    hooks = {len(first) + 2 * i: fn for i, fn in enumerate(between)}
    assert not hooks or max(hooks) < len(batches)

    def copies(b, slot):
        return [pltpu.make_async_copy(s, stage.at[slot, :, pl.ds(lane, cols)], sem.at[slot])
                for s, _, lane, cols in batches[b]]

    for b in range(min(2, len(batches))):
        for cp in copies(b, b):
            cp.start()
    for b in range(len(batches)):
        slot = b % 2
        for cp in copies(b, slot):
            cp.wait()
        for _, d, lane, cols in batches[b]:
            d[...] = stage[slot, :, lane:lane + cols].astype(_BF16)
        if b + 2 < len(batches):
            for cp in copies(b + 2, slot):
                cp.start()
        if b in hooks:
            hooks[b]()


def _fused_kernel(x_hbm, xs_hbm, sp_hbm, sc_hbm, sf_hbm, meta_ref,
                  w_in_hbm, w_pool_hbm, wco_hbm, wo_hbm, wup_hbm, wdown_hbm,
                  pscale, wconv_hbm, wfc_hbm, g1, g2, g3, g4,
                  y_hbm, ys_hbm, op_ref, oc_ref, of_ref, spo_hbm, sco_hbm, sfo_hbm,
                  w_in, w_pool, wco, wo, wup, wdown, wconv, wfc, stage, sp_v, sc_v,
                  m_ubuf, m_cbuf, m_upbuf,
                  s_mbuf, s_actbuf, s_h2, m_h2, xs_v, ys_v, us_v, cs_v, s_up, sf_ring, xbuf, ybuf,
                  in_sem, out_sem, cast_sem, state_sem, ring_sem,
                  ubuf, cbuf, upbuf, utail, ctail, ftail, mbuf, actbuf, h2buf, x1_new, x1_old,
                  ubuf_first, cbuf_first, mbuf_first, upbuf_last, actbuf_last,
                  *, tm, nt, total, layer):
    w = (w_in, w_pool, pscale, wconv, wco, wo, wup, wfc, wdown, g1, g2, g3, g4)
    seg = tm // SUBLANES
    n = pl.program_id(0)
    slot = n % 2

    def x_copies(tile, sl):
        b, t0 = tile // nt, (tile % nt) * tm
        return [pltpu.make_async_copy(x_hbm.at[b, pl.ds(t0 + s * seg, seg), :],
                                      xbuf.at[sl, :, s, :], in_sem.at[sl]) for s in range(SUBLANES)]

    def y_copies(tile, sl):
        b, t0 = tile // nt, (tile % nt) * tm
        return [pltpu.make_async_copy(ybuf.at[sl, :, s, :],
                                      y_hbm.at[b, pl.ds(t0 + s * seg, seg), :], out_sem.at[sl])
                for s in range(SUBLANES)]

    ring_slots, group = sf_ring.shape[:2]
    n_groups = xs_v.shape[0] // group
    pool_out_cps = [
        pltpu.make_async_copy(sp_hbm.at[pl.ds(1, POOL_BUF - 1)], spo_hbm.at[pl.ds(0, POOL_BUF - 1)],
                              state_sem.at[6]),
        pltpu.make_async_copy(us_v, spo_hbm.at[POOL_BUF - 1], state_sem.at[6])]
    conv_out_cp = pltpu.make_async_copy(sc_v, sco_hbm.at[layer], state_sem.at[9])

    def ffn_in_copy(g):
        return pltpu.make_async_copy(sf_hbm.at[layer, pl.ds(g * group, group)],
                                     sf_ring.at[g % ring_slots], ring_sem.at[0, g % ring_slots])

    def ffn_out_copy(g):
        return pltpu.make_async_copy(sf_ring.at[g % ring_slots],
                                     sfo_hbm.at[layer, pl.ds(g * group, group)],
                                     ring_sem.at[1, g % ring_slots])

    def wait_tile(buf, sem):
        pltpu.make_async_copy(buf, buf, sem).wait()

    def start_sequence_mixer_history():
        _fill_tail(utail, m_ubuf, POOL_HALO + N_META - POOL_BUF, POOL_BUF)
        _fill_tail(ctail, m_cbuf, CONV_HALO + N_META - (CONV_W - 1), CONV_W - 1)

    def token_mixer(work):
        ub, cb, mb = work
        hist = _SegmentHistory(ub, cb, None, utail, ctail, None, tm)
        x = xbuf[slot].reshape(tm, D_MODEL)
        _mixer(x, w, hist, mb, _full_windows)
        x1 = _mixer_residual(x, _mixer_out(w, mb), w)
        x1_new[...] = x1
        h2buf[...] = _ffn_norm(x1, w)

    def channel_mixer_up(work):
        upb, ab = work
        _ffn_up(h2buf[...], w, _SegmentHistory(None, None, upb, None, None, ftail, tm), ab)

    def channel_mixer_out(work, x1_ref):
        y = _ffn_residual(x1_ref[...], _ffn_down(w, work[1]), w)
        ybuf[1 - slot] = y.reshape(seg, SUBLANES, D_MODEL)

    def prefetch_next_tile():
        for cp in x_copies(jnp.minimum(n + 1, total - 1), 1 - slot):
            cp.start()

    def send_previous_tile():
        for cp in y_copies(n - 1, 1 - slot):
            cp.start()

    @pl.when(n == 0)
    def _():
        for cp in x_copies(n, slot):
            cp.start()
        pool_cp = pltpu.make_async_copy(sp_hbm, sp_v, state_sem.at[0])
        conv_cp = pltpu.make_async_copy(sc_hbm.at[layer], sc_v, state_sem.at[1])
        wconv_cp = pltpu.make_async_copy(wconv_hbm.at[:, layer, :], wconv, state_sem.at[10])
        wfc_cp = pltpu.make_async_copy(wfc_hbm.at[:, layer, :], wfc, state_sem.at[10])
        wconv_cp.start()
        wfc_cp.start()
        xs_cp = pltpu.make_async_copy(xs_hbm.at[:, 0, :], xs_v, state_sem.at[4])
        ys_cp = pltpu.make_async_copy(ys_v, ys_hbm.at[:, 0, :], state_sem.at[5])
        pool_cp.start()
        conv_cp.start()
        xs_cp.start()

        m_ubuf[0:POOL_HALO, :] = jnp.zeros((POOL_HALO, POOL_WIDTH), _F32)
        m_cbuf[0:CONV_HALO, :] = jnp.zeros((CONV_HALO, CONV_WIDTH), _F32)
        m_upbuf[0:CONV_HALO, :] = jnp.zeros((CONV_HALO, 2 * D_FF), _F32)
        row1 = lax.broadcasted_iota(jnp.int32, (N_META, POOL_GC), 0) + 1

        def meta_windows(s, win):
            return s / jnp.minimum(win, row1).astype(_F32)

        meta_hist = _RowHistory(m_ubuf, m_cbuf, m_upbuf, N_META)
        meta_rows = pl.ds(0, N_META)

        def meta_token_mixer():
            wconv_cp.wait()
            wfc_cp.wait()
            _mixer(meta_ref[...], w, meta_hist, s_mbuf.at[meta_rows], meta_windows)
            x1 = _mixer_residual(meta_ref[...], _mixer_out(w, s_mbuf.at[meta_rows]), w)
            m_h2[...] = _ffn_norm(x1, w)

        sample_hist = _StateHistory(sp_v, sc_v, stage, us_v, cs_v, s_up)

        def sample_token_mixer():
            pool_cp.wait()
            conv_cp.wait()
            xs_cp.wait()
            _mixer(xs_v[...], w, sample_hist, s_mbuf, _full_windows)
            x1 = _mixer_residual(xs_v[...], _mixer_out(w, s_mbuf), w)
            ys_v[...] = x1
            s_h2[...] = _ffn_norm(x1, w)
            for cp in pool_out_cps:
                cp.start()
            for j in range(CONV_W - 2):
                sc_v[:, j, :] = sc_v[:, j + 1, :]
            sc_v[:, CONV_W - 2, :] = cs_v[...]
            conv_out_cp.start()

        def first_tile_token_mixer():
            wait_tile(xbuf.at[slot], in_sem.at[slot])
            prefetch_next_tile()
            start_sequence_mixer_history()
            token_mixer((ubuf_first, cbuf_first, mbuf_first))

        _cast_weights([(w_in_hbm, w_in), (w_pool_hbm, w_pool), (wco_hbm, wco), (wo_hbm, wo)],
                      [(wup_hbm, wup), (wdown_hbm, wdown)],
                      [meta_token_mixer, sample_token_mixer, first_tile_token_mixer],
                      stage, cast_sem)
        for g in range(min(ring_slots - 1, n_groups)):
            ffn_in_copy(g).start()
        _ffn_up(m_h2[...], w, meta_hist, s_actbuf.at[meta_rows])
        for g in range(n_groups):
            if g + ring_slots - 1 < n_groups:
                ffn_in_copy(g + ring_slots - 1).start()
            ffn_in_copy(g).wait()
            for k in range(CONV_W - 1):
                stage[k, g * group:(g + 1) * group, :] = sf_ring[g % ring_slots, :, k, :]

        _ffn_up(s_h2[...], w, sample_hist, s_actbuf)
        ys_v[...] = _ffn_residual(ys_v[...], _ffn_down(w, s_actbuf), w)
        ys_cp.start()

        for g in range(n_groups):
            if g >= ring_slots:
                ffn_out_copy(g - ring_slots).wait()
            rows = slice(g * group, (g + 1) * group)
            for k in range(1, CONV_W - 1):
                sf_ring[g % ring_slots, :, k - 1, :] = stage[k, rows, :]
            sf_ring[g % ring_slots, :, CONV_W - 2, :] = s_up[rows, :]
            ffn_out_copy(g).start()
        ys_cp.wait()

    @pl.when((n > 0) & (n < total))
    def _():
        wait_tile(xbuf.at[slot], in_sem.at[slot])

    @pl.when(n >= 3)
    def _():
        wait_tile(ybuf.at[1 - slot], out_sem.at[1 - slot])

    @pl.when((n % nt == 0) & (n > 0))
    def _():
        start_sequence_mixer_history()

    @pl.when(n % nt == 1)
    def _():
        _fill_tail(ftail, m_upbuf, CONV_HALO + N_META - (CONV_W - 1), CONV_W - 1)

    @pl.when((n > 0) & (n < total))
    def _():
        prefetch_next_tile()
        x1_old[...] = x1_new[...]
        channel_mixer_up((upbuf, actbuf))
        token_mixer((ubuf, cbuf, mbuf))
        channel_mixer_out((upbuf, actbuf), x1_old)
        send_previous_tile()

    @pl.when(n == total)
    def _():
        channel_mixer_up((upbuf_last, actbuf_last))
        channel_mixer_out((upbuf_last, actbuf_last), x1_new)
        send_previous_tile()

    @pl.when((n % nt == nt - 1) & (n < total))
    def _():
        for j in range(POOL_BUF):
            op_ref[j, pl.ds(n // nt, 1), :] = utail[(j + 1) * SUBLANES - 1:(j + 1) * SUBLANES, :]
        _read_tail(oc_ref, ctail, CONV_W - 1)

    @pl.when((n % nt == 0) & (n >= 1))
    def _():
        _read_tail(of_ref, ftail, CONV_W - 1)

    @pl.when(n == total)
    def _():
        wait_tile(ybuf.at[slot], out_sem.at[slot])
        wait_tile(ybuf.at[1 - slot], out_sem.at[1 - slot])
        wait_tile(xbuf.at[slot], in_sem.at[slot])
        for cp in pool_out_cps:
            cp.wait()
        conv_out_cp.wait()
        for g in range(max(n_groups - ring_slots, 0), n_groups):
            ffn_out_copy(g).wait()


def _resident(shape):
    zeros = (0,) * len(shape)
    return pl.BlockSpec(shape, lambda *_: zeros, pipeline_mode=pl.Buffered(1))


def _fused_call(x, xs, sp, sc, sf, meta, big_weights, pscale, w_conv, w_ffn_conv, gains, l, tm):
    nb, t, d = x.shape
    ns = xs.shape[0]
    seg = tm // SUBLANES
    assert t % tm == 0 and tm % SUBLANES == 0 and seg >= POOL_BUF
    assert xs.shape == (ns, 1, d) and ns == CAST_ROWS and ns % SF_GROUP == 0
    assert sf.shape[1:] == (ns, CONV_W - 1, STAGE_COLS) and CONV_W - 1 == 2
    assert sp.shape == (POOL_BUF, ns, POOL_WIDTH) and sc.shape[1:] == (ns, CONV_W - 1, CONV_WIDTH)
    nt = t // tm
    assert nt >= 2
    total = nb * nt
    mix_state = lambda rows, c: pl.BlockSpec(
        (None, None, rows, c), lambda n: (0, jnp.minimum(n, total - 1) // nt, 0, 0))
    ffn_state = lambda rows, c: pl.BlockSpec(
        (None, None, rows, c), lambda n: (0, jnp.maximum(n - 1, 0) // nt, 0, 0))
    hbm = pl.BlockSpec(memory_space=pl.ANY)
    rows_with_head = lambda depth: (depth + seg) * SUBLANES
    return pl.pallas_call(
        functools.partial(_fused_kernel, tm=tm, nt=nt, total=total, layer=l),
        grid=(total + 1,),
        in_specs=[hbm, hbm, hbm, hbm, hbm, _resident(meta.shape)]
                 + [hbm] * len(big_weights)
                 + [_resident(pscale.shape), hbm, hbm]
                 + [_resident(g.shape) for g in gains],
        out_specs=[hbm, hbm,
                   _resident((POOL_BUF, nb, POOL_WIDTH)), mix_state(CONV_W - 1, CONV_WIDTH),
                   ffn_state(CONV_W - 1, 2 * D_FF),
                   hbm, hbm, hbm],
        out_shape=[jax.ShapeDtypeStruct((nb, t, d), _F32),
                   jax.ShapeDtypeStruct((ns, 1, d), _F32),
                   jax.ShapeDtypeStruct((POOL_BUF, nb, POOL_WIDTH), _F32),
                   jax.ShapeDtypeStruct((1, nb, CONV_W - 1, CONV_WIDTH), _F32),
                   jax.ShapeDtypeStruct((1, nb, CONV_W - 1, 2 * D_FF), _F32),
                   jax.ShapeDtypeStruct(sp.shape, _F32),
                   jax.ShapeDtypeStruct(sc.shape, _F32),
                   jax.ShapeDtypeStruct(sf.shape, _F32)],
        scratch_shapes=[pltpu.VMEM((d, IN_TOTAL), _BF16),
                        pltpu.VMEM((N_POOL_GROUPS, POOL_GC, POOL_OUT_GC), _BF16),
                        pltpu.VMEM((CONV_WIDTH, d), _BF16),
                        pltpu.VMEM((d, d), _BF16),
                        pltpu.VMEM((d, 2 * D_FF), _BF16),
                        pltpu.VMEM((D_FF, d), _BF16),
                        pltpu.VMEM((CONV_W, CONV_WIDTH), _F32),
                        pltpu.VMEM((CONV_W, 2 * D_FF), _F32),
                        pltpu.VMEM((2, CAST_ROWS, STAGE_COLS), _F32),
                        pltpu.VMEM(sp.shape, _F32),
                        pltpu.VMEM(sc.shape[1:], _F32),
                        pltpu.VMEM((POOL_HALO + N_META, POOL_WIDTH), _F32),
                        pltpu.VMEM((CONV_HALO + N_META, CONV_WIDTH), _F32),
                        pltpu.VMEM((CONV_HALO + N_META, 2 * D_FF), _F32),
                        pltpu.VMEM((ns, d), _BF16),
                        pltpu.VMEM((ns, D_FF), _BF16),
                        pltpu.VMEM((ns, d), _BF16),
                        pltpu.VMEM((N_META, d), _BF16),
                        pltpu.VMEM((ns, d), _F32),
                        pltpu.VMEM((ns, d), _F32),
                        pltpu.VMEM((ns, POOL_WIDTH), _F32),
                        pltpu.VMEM((ns, CONV_WIDTH), _F32),
                        pltpu.VMEM((ns, 2 * D_FF), _F32),
                        pltpu.VMEM((SF_SLOTS, SF_GROUP, CONV_W - 1, 2 * D_FF), _F32),
                        pltpu.VMEM((2, seg, SUBLANES, d), _F32),
                        pltpu.VMEM((2, seg, SUBLANES, d), _F32),
                        pltpu.SemaphoreType.DMA((2,)),
                        pltpu.SemaphoreType.DMA((2,)),
                        pltpu.SemaphoreType.DMA((2,)),
                        pltpu.SemaphoreType.DMA((11,)),
                        pltpu.SemaphoreType.DMA((2, SF_SLOTS)),
                        pltpu.VMEM((rows_with_head(POOL_BUF), POOL_WIDTH), _F32),
                        pltpu.VMEM((rows_with_head(CONV_W - 1), CONV_WIDTH), _F32),
                        pltpu.VMEM((rows_with_head(CONV_W - 1), 2 * D_FF), _F32),
                        pltpu.VMEM((POOL_BUF * SUBLANES, POOL_WIDTH), _F32),
                        pltpu.VMEM(((CONV_W - 1) * SUBLANES, CONV_WIDTH), _F32),
                        pltpu.VMEM(((CONV_W - 1) * SUBLANES, 2 * D_FF), _F32),
                        pltpu.VMEM((tm, d), _BF16),
                        pltpu.VMEM((tm, D_FF), _BF16),
                        pltpu.VMEM((tm, d), _BF16),
                        pltpu.VMEM((tm, d), _F32),
                        pltpu.VMEM((tm, d), _F32),
                        pltpu.VMEM((rows_with_head(POOL_BUF), POOL_WIDTH), _F32),
                        pltpu.VMEM((rows_with_head(CONV_W - 1), CONV_WIDTH), _F32),
                        pltpu.VMEM((tm, d), _BF16),
                        pltpu.VMEM((rows_with_head(CONV_W - 1), 2 * D_FF), _F32),
                        pltpu.VMEM((tm, D_FF), _BF16)],
        compiler_params=pltpu.CompilerParams(
            dimension_semantics=("arbitrary",), vmem_limit_bytes=VMEM_LIMIT_BYTES),
        name="hybrid_step",
    )(x, xs, sp, sc, sf, meta, *big_weights, pscale, w_conv, w_ffn_conv, *gains)


def kernel(x_prompt, x_sample, state_pool, state_conv, state_ffn, meta_tokens, w_in, w_pool,
           pool_scale, w_conv, w_conv_out, w_o, w_up, w_ffn_conv, w_down, g_pre_mix,
           g_post_mix, g_pre_ffn, g_post_ffn):
    depth = w_in.shape[0]
    assert depth == 1, "the prompt / sample streams are chained for a single layer only"
    assert x_sample.shape[1] == 1 and meta_tokens.shape[0] == N_META
    assert N_META >= max(POOL_WINDOWS)
    row = lambda a: a.reshape(1, -1)
    l = 0
    big = (w_in[l], w_pool[l], w_conv_out[l], w_o[l], w_up[l], w_down[l])
    gains = (row(g_pre_mix[l]), row(g_post_mix[l]), row(g_pre_ffn[l]), row(g_post_ffn[l]))

    rows_first = lambda s: jnp.transpose(s, (1, 0, 2))
    y_prompt, y_sample, pp, pc, pf, sp_new, sc_new, sf_new = _fused_call(
        x_prompt, x_sample, rows_first(state_pool[l]), state_conv, state_ffn,
        meta_tokens.astype(_F32), big, row(pool_scale[l]), rows_first(w_conv),
        rows_first(w_ffn_conv), gains, l, TM_PROMPT)

    return (y_prompt, y_sample, rows_first(pp)[None], pc, pf, rows_first(sp_new)[None],
            sc_new, sf_new)
```

```python
import functools

import jax
import jax.numpy as jnp
from jax import lax
from jax.experimental import pallas as pl
from jax.experimental.pallas import tpu as pltpu

D_MODEL = 1024
N_META = 16
POOL_WINDOWS = (2, 4, 8, 16)
N_POOL_GROUPS = len(POOL_WINDOWS)
POOL_WIDTH = D_MODEL // 2
POOL_GC = POOL_WIDTH // N_POOL_GROUPS
POOL_OUT_GC = D_MODEL // N_POOL_GROUPS
POOL_BUF = max(POOL_WINDOWS) - 1
CONV_WIDTH = D_MODEL // 2
CONV_W = 3
D_FF = 2816
IN_TOTAL = 2 * POOL_WIDTH + 2 * CONV_WIDTH + 2 * D_MODEL
EPS = 1e-6
LOG2_E = 1.4426950408889634
OFF_U, OFF_V, OFF_B, OFF_C, OFF_GA, OFF_GB = 0, 512, 1024, 1536, 2048, 3072

SUBLANES = 8
FF_CHUNK = 256
POOL_HALO = 16
CONV_HALO = 8
TM_PROMPT = 256
CAST_ROWS = 128
CAST_SLOTS = 3
SF_GROUP = 8
SF_SLOTS = 4
STAGE_COLS = 2 * D_FF
VMEM_LIMIT_BYTES = 58 * 1024 * 1024

_BF16 = jnp.bfloat16
_F32 = jnp.float32


def _dot(a, b):
    return jnp.dot(a, b, preferred_element_type=_F32)


def _rms(x, g):
    ms = jnp.mean(x * x, axis=-1, keepdims=True)
    return x * lax.rsqrt(ms + EPS) * g


def _sigmoid(x):
    return 1.0 / (1.0 + jnp.exp2(x * (-LOG2_E)))


def _mixer(x, w, hist, mbuf, pool_div):
    (w_in, w_pool, pscale, wconv, wco, _, _, _, _, g1, _, _, _) = w
    h = _rms(x, g1[...]).astype(_BF16)

    hist.put_u(_dot(h, w_in[:, OFF_U:OFF_U + POOL_WIDTH]))
    v = _dot(h, w_in[:, OFF_V:OFF_V + CONV_WIDTH])
    cg = _dot(h, w_in[:, OFF_C:OFF_C + CONV_WIDTH])
    hist.put_c(cg * v)
    cv = wconv[CONV_W - 1:CONV_W, :] * hist.get_c(0)
    for k in range(1, CONV_W):
        cv = cv + wconv[CONV_W - 1 - k:CONV_W - k, :] * hist.get_c(k)
    bg = _dot(h, w_in[:, OFF_B:OFF_B + CONV_WIDTH])
    b_in = (bg * cv).astype(_BF16)

    for g, win in enumerate(POOL_WINDOWS):
        cs = slice(g * POOL_GC, (g + 1) * POOL_GC)
        os_ = slice(g * POOL_OUT_GC, (g + 1) * POOL_OUT_GC)
        ug = hist.get_u(0, cs)
        s = ug
        for k in range(1, win):
            s = s + hist.get_u(k, cs)
        pooled = (pool_div(s, win) - ug).astype(_BF16)
        a = _dot(pooled, w_pool[g]) * pscale[:, os_]
        ga = _dot(h, w_in[:, OFF_GA + g * POOL_OUT_GC:OFF_GA + (g + 1) * POOL_OUT_GC])
        gb = _dot(h, w_in[:, OFF_GB + g * POOL_OUT_GC:OFF_GB + (g + 1) * POOL_OUT_GC])
        b = _dot(b_in, wco[:, os_])
        mbuf[:, os_] = (_sigmoid(ga) * a + _sigmoid(gb) * b).astype(_BF16)


def _mixer_out(w, mbuf, rows=slice(None)):
    return _dot(mbuf[rows, :], w[5][...])


def _mixer_residual(x, mixed, w):
    return x + _rms(mixed, w[10][...])


def _ffn_norm(x1, w):
    return _rms(x1, w[11][...]).astype(_BF16)


def _ffn_up(h2, w, hist, actbuf):
    wup, wfc = w[6], w[7]
    for c in range(D_FF // FF_CHUNK):
        conv = []
        for col in (c * FF_CHUNK, D_FF + c * FF_CHUNK):
            cols = slice(col, col + FF_CHUNK)
            hist.put_up(_dot(h2, wup[:, cols]), cols)
            y = wfc[CONV_W - 1:CONV_W, cols] * hist.get_up(0, cols)
            for k in range(1, CONV_W):
                y = y + wfc[CONV_W - 1 - k:CONV_W - k, cols] * hist.get_up(k, cols)
            conv.append(y)
        gate, val = conv
        actbuf[:, c * FF_CHUNK:(c + 1) * FF_CHUNK] = (gate * _sigmoid(gate) * val).astype(_BF16)


def _ffn_down(w, actbuf, rows=slice(None)):
    return _dot(actbuf[rows, :], w[8][...])


def _ffn_residual(x1, down, w):
    return x1 + _rms(down, w[12][...])


def _full_windows(s, win):
    return s * (1.0 / win)


class _RowHistory:
    def __init__(self, ubuf, cbuf, upbuf, m):
        self.ubuf, self.cbuf, self.upbuf, self.m = ubuf, cbuf, upbuf, m

    def put_u(self, val):
        self.ubuf[POOL_HALO:POOL_HALO + self.m, :] = val

    def get_u(self, k, cs):
        return self.ubuf[POOL_HALO - k:POOL_HALO - k + self.m, cs]

    def put_c(self, val):
        self.cbuf[CONV_HALO:CONV_HALO + self.m, :] = val

    def get_c(self, k):
        return self.cbuf[CONV_HALO - k:CONV_HALO - k + self.m, :]

    def put_up(self, val, cols):
        self.upbuf[CONV_HALO:CONV_HALO + self.m, cols] = val

    def get_up(self, k, cols):
        return self.upbuf[CONV_HALO - k:CONV_HALO - k + self.m, cols]


class _SegmentHistory:
    def __init__(self, ubuf, cbuf, upbuf, utail, ctail, ftail, tm):
        self.ubuf, self.cbuf, self.upbuf, self.tm = ubuf, cbuf, upbuf, tm
        self.utail, self.ctail, self.ftail = utail, ctail, ftail

    def _put(self, buf, tail, val, cols, depth):
        tm = self.tm
        buf[depth * SUBLANES:depth * SUBLANES + tm, cols] = val
        sublane = lax.broadcasted_iota(jnp.int32, (SUBLANES, val.shape[1]), 0)
        for j in range(depth):
            rows = slice(j * SUBLANES, (j + 1) * SUBLANES)
            cur = val[tm - (depth - j) * SUBLANES:tm - (depth - j - 1) * SUBLANES, :]
            mixed = jnp.where(sublane == SUBLANES - 1, tail[rows, cols], cur)
            buf[rows, cols] = pltpu.roll(mixed, 1, axis=0)
            tail[rows, cols] = cur

    def _get(self, buf, k, cols, depth):
        start = (depth - k) * SUBLANES
        return buf[start:start + self.tm, cols]

    def put_u(self, val):
        self._put(self.ubuf, self.utail, val, slice(None), POOL_BUF)

    def get_u(self, k, cs):
        return self._get(self.ubuf, k, cs, POOL_BUF)

    def put_c(self, val):
        self._put(self.cbuf, self.ctail, val, slice(None), CONV_W - 1)

    def get_c(self, k):
        return self._get(self.cbuf, k, slice(None), CONV_W - 1)

    def put_up(self, val, cols):
        self._put(self.upbuf, self.ftail, val, cols, CONV_W - 1)

    def get_up(self, k, cols):
        return self._get(self.upbuf, k, cols, CONV_W - 1)


class _StateHistory:
    def __init__(self, sp_ref, sc_ref, sf_ref, u_ref, c_ref, f_ref):
        self.sp, self.sc, self.sf = sp_ref, sc_ref, sf_ref
        self.u, self.c, self.f = u_ref, c_ref, f_ref

    def put_u(self, val):
        self.u[...] = val

    def get_u(self, k, cs):
        return self.u[:, cs] if k == 0 else self.sp[POOL_BUF - k, :, cs]

    def put_c(self, val):
        self.c[...] = val

    def get_c(self, k):
        return self.c[...] if k == 0 else self.sc[:, CONV_W - 1 - k, :]

    def put_up(self, val, cols):
        self.f[:, cols] = val

    def get_up(self, k, cols):
        return self.f[:, cols] if k == 0 else self.sf[CONV_W - 1 - k, :, cols]


def _fill_tail(tail, state_ref, first_row, depth):
    for j in range(depth):
        r = first_row + j
        tail[j * SUBLANES:(j + 1) * SUBLANES, :] = jnp.broadcast_to(
            state_ref[r:r + 1, :], (SUBLANES, tail.shape[1]))


def _read_tail(out_ref, tail, depth):
    for j in range(depth):
        out_ref[j:j + 1, :] = tail[(j + 1) * SUBLANES - 1:(j + 1) * SUBLANES, :]


def _cast_batches(pairs):
    jobs = []
    for src, dst in pairs:
        if len(src.shape) == 3:
            assert src.shape[1] == CAST_ROWS
            jobs += [(src.at[g], dst.at[g], src.shape[2]) for g in range(src.shape[0])]
        else:
            assert src.shape[0] % CAST_ROWS == 0
            jobs += [(src.at[pl.ds(r, CAST_ROWS), :], dst.at[pl.ds(r, CAST_ROWS), :], src.shape[1])
                     for r in range(0, src.shape[0], CAST_ROWS)]
    batches, cur, used = [], [], 0
    for s, d, cols in jobs:
        if used + cols > STAGE_COLS:
            batches.append(cur)
            cur, used = [], 0
        cur.append((s, d, used, cols))
        used += cols
    batches.append(cur)
    return batches


def _cast_weights(first_pairs, later_pairs, between, stage, sem):
    slots = stage.shape[0]
    first = _cast_batches(first_pairs)
    batches = first + _cast_batches(later_pairs)
    hooks = {len(first) + 2 * i: fn for i, fn in enumerate(between)}
    assert not hooks or max(hooks) < len(batches)

    def copies(b):
        return [pltpu.make_async_copy(s, stage.at[b % slots, :, pl.ds(lane, cols)],
                                      sem.at[b % slots]) for s, _, lane, cols in batches[b]]

    for b in range(min(slots, len(batches))):
        for cp in copies(b):
            cp.start()
    for b in range(len(batches)):
        for cp in copies(b):
            cp.wait()
        for _, d, lane, cols in batches[b]:
            d[...] = stage[b % slots, :, lane:lane + cols].astype(_BF16)
        if b + slots < len(batches):
            for cp in copies(b + slots):
                cp.start()
        if b in hooks:
            hooks[b]()


def _fused_kernel(x_hbm, xs_hbm, sp_hbm, sc_hbm, sf_hbm, meta_ref,
                  w_in_hbm, w_pool_hbm, wco_hbm, wo_hbm, wup_hbm, wdown_hbm,
                  pscale, wconv_hbm, wfc_hbm, g1, g2, g3, g4,
                  y_hbm, ys_hbm, op_ref, oc_ref, of_ref, spo_hbm, sco_hbm, sfo_hbm,
                  w_in, w_pool, wco, wo, wup, wdown, wconv, wfc, stage, sp_v, sc_v,
                  m_ubuf, m_cbuf, m_upbuf,
                  s_mbuf, s_actbuf, s_h2, m_h2, xs_v, ys_v, us_v, cs_v, s_up, sf_ring, xbuf, ybuf,
                  in_sem, out_sem, cast_sem, state_sem, ring_sem,
                  ubuf, cbuf, upbuf, utail, ctail, ftail, mbuf, actbuf, h2buf, x1_new, x1_old,
                  ubuf_first, cbuf_first, mbuf_first, upbuf_last, actbuf_last,
                  *, tm, nt, total, layer):
    w = (w_in, w_pool, pscale, wconv, wco, wo, wup, wfc, wdown, g1, g2, g3, g4)
    seg = tm // SUBLANES
    n = pl.program_id(0)
    slot = n % 2

    def x_copies(tile, sl):
        b, t0 = tile // nt, (tile % nt) * tm
        return [pltpu.make_async_copy(x_hbm.at[b, pl.ds(t0 + s * seg, seg), :],
                                      xbuf.at[sl, :, s, :], in_sem.at[sl]) for s in range(SUBLANES)]

    def y_copies(tile, sl):
        b, t0 = tile // nt, (tile % nt) * tm
        return [pltpu.make_async_copy(ybuf.at[sl, :, s, :],
                                      y_hbm.at[b, pl.ds(t0 + s * seg, seg), :], out_sem.at[sl])
                for s in range(SUBLANES)]

    ring_slots, group = sf_ring.shape[:2]
    n_groups = xs_v.shape[0] // group
    pool_out_cps = [
        pltpu.make_async_copy(sp_hbm.at[pl.ds(1, POOL_BUF - 1)], spo_hbm.at[pl.ds(0, POOL_BUF - 1)],
                              state_sem.at[6]),
        pltpu.make_async_copy(us_v, spo_hbm.at[POOL_BUF - 1], state_sem.at[6])]
    conv_out_cp = pltpu.make_async_copy(sc_v, sco_hbm.at[layer], state_sem.at[9])

    def ffn_in_copy(g):
        return pltpu.make_async_copy(sf_hbm.at[layer, pl.ds(g * group, group)],
                                     sf_ring.at[g % ring_slots], ring_sem.at[0, g % ring_slots])

    def ffn_out_copy(g):
        return pltpu.make_async_copy(sf_ring.at[g % ring_slots],
                                     sfo_hbm.at[layer, pl.ds(g * group, group)],
                                     ring_sem.at[1, g % ring_slots])

    def wait_tile(buf, sem):
        pltpu.make_async_copy(buf, buf, sem).wait()

    def start_sequence_mixer_history():
        _fill_tail(utail, m_ubuf, POOL_HALO + N_META - POOL_BUF, POOL_BUF)
        _fill_tail(ctail, m_cbuf, CONV_HALO + N_META - (CONV_W - 1), CONV_W - 1)

    def token_mixer(work):
        ub, cb, mb = work
        hist = _SegmentHistory(ub, cb, None, utail, ctail, None, tm)
        x = xbuf[slot].reshape(tm, D_MODEL)
        _mixer(x, w, hist, mb, _full_windows)
        x1 = _mixer_residual(x, _mixer_out(w, mb), w)
        x1_new[...] = x1
        h2buf[...] = _ffn_norm(x1, w)

    def channel_mixer_up(work):
        upb, ab = work
        _ffn_up(h2buf[...], w, _SegmentHistory(None, None, upb, None, None, ftail, tm), ab)

    def channel_mixer_out(work, x1_ref):
        y = _ffn_residual(x1_ref[...], _ffn_down(w, work[1]), w)
        ybuf[1 - slot] = y.reshape(seg, SUBLANES, D_MODEL)

    def prefetch_next_tile():
        for cp in x_copies(jnp.minimum(n + 1, total - 1), 1 - slot):
            cp.start()

    def send_previous_tile():
        for cp in y_copies(n - 1, 1 - slot):
            cp.start()

    @pl.when(n == 0)
    def _():
        for cp in x_copies(n, slot):
            cp.start()
        pool_cp = pltpu.make_async_copy(sp_hbm, sp_v, state_sem.at[0])
        conv_cp = pltpu.make_async_copy(sc_hbm.at[layer], sc_v, state_sem.at[1])
        wconv_cp = pltpu.make_async_copy(wconv_hbm.at[:, layer, :], wconv, state_sem.at[10])
        wfc_cp = pltpu.make_async_copy(wfc_hbm.at[:, layer, :], wfc, state_sem.at[10])
        wconv_cp.start()
        wfc_cp.start()
        xs_cp = pltpu.make_async_copy(xs_hbm.at[:, 0, :], xs_v, state_sem.at[4])
        ys_cp = pltpu.make_async_copy(ys_v, ys_hbm.at[:, 0, :], state_sem.at[5])
        pool_cp.start()
        conv_cp.start()
        xs_cp.start()

        m_ubuf[0:POOL_HALO, :] = jnp.zeros((POOL_HALO, POOL_WIDTH), _F32)
        m_cbuf[0:CONV_HALO, :] = jnp.zeros((CONV_HALO, CONV_WIDTH), _F32)
        m_upbuf[0:CONV_HALO, :] = jnp.zeros((CONV_HALO, 2 * D_FF), _F32)
        row1 = lax.broadcasted_iota(jnp.int32, (N_META, POOL_GC), 0) + 1

        def meta_windows(s, win):
            return s / jnp.minimum(win, row1).astype(_F32)

        meta_hist = _RowHistory(m_ubuf, m_cbuf, m_upbuf, N_META)
        meta_rows = pl.ds(0, N_META)

        def meta_token_mixer():
            wconv_cp.wait()
            wfc_cp.wait()
            _mixer(meta_ref[...], w, meta_hist, s_mbuf.at[meta_rows], meta_windows)
            x1 = _mixer_residual(meta_ref[...], _mixer_out(w, s_mbuf.at[meta_rows]), w)
            m_h2[...] = _ffn_norm(x1, w)

        sample_hist = _StateHistory(sp_v, sc_v, stage, us_v, cs_v, s_up)

        def sample_token_mixer():
            pool_cp.wait()
            conv_cp.wait()
            xs_cp.wait()
            _mixer(xs_v[...], w, sample_hist, s_mbuf, _full_windows)
            x1 = _mixer_residual(xs_v[...], _mixer_out(w, s_mbuf), w)
            ys_v[...] = x1
            s_h2[...] = _ffn_norm(x1, w)
            for cp in pool_out_cps:
                cp.start()
            for j in range(CONV_W - 2):
                sc_v[:, j, :] = sc_v[:, j + 1, :]
            sc_v[:, CONV_W - 2, :] = cs_v[...]
            conv_out_cp.start()

        def first_tile_token_mixer():
            wait_tile(xbuf.at[slot], in_sem.at[slot])
            prefetch_next_tile()
            start_sequence_mixer_history()
            token_mixer((ubuf_first, cbuf_first, mbuf_first))

        _cast_weights([(w_in_hbm, w_in), (w_pool_hbm, w_pool), (wco_hbm, wco), (wo_hbm, wo)],
                      [(wup_hbm, wup), (wdown_hbm, wdown)],
                      [meta_token_mixer, sample_token_mixer, first_tile_token_mixer],
                      stage, cast_sem)
        for g in range(min(ring_slots - 1, n_groups)):
            ffn_in_copy(g).start()
        _ffn_up(m_h2[...], w, meta_hist, s_actbuf.at[meta_rows])
        for g in range(n_groups):
            if g + ring_slots - 1 < n_groups:
                ffn_in_copy(g + ring_slots - 1).start()
            ffn_in_copy(g).wait()
            for k in range(CONV_W - 1):
                stage[k, g * group:(g + 1) * group, :] = sf_ring[g % ring_slots, :, k, :]

        _ffn_up(s_h2[...], w, sample_hist, s_actbuf)
        ys_v[...] = _ffn_residual(ys_v[...], _ffn_down(w, s_actbuf), w)
        ys_cp.start()

        for g in range(n_groups):
            if g >= ring_slots:
                ffn_out_copy(g - ring_slots).wait()
            rows = slice(g * group, (g + 1) * group)
            for k in range(1, CONV_W - 1):
                sf_ring[g % ring_slots, :, k - 1, :] = stage[k, rows, :]
            sf_ring[g % ring_slots, :, CONV_W - 2, :] = s_up[rows, :]
            ffn_out_copy(g).start()
        ys_cp.wait()

    @pl.when((n > 0) & (n < total))
    def _():
        wait_tile(xbuf.at[slot], in_sem.at[slot])

    @pl.when(n >= 3)
    def _():
        wait_tile(ybuf.at[1 - slot], out_sem.at[1 - slot])

    @pl.when((n % nt == 0) & (n > 0))
    def _():
        start_sequence_mixer_history()

    @pl.when(n % nt == 1)
    def _():
        _fill_tail(ftail, m_upbuf, CONV_HALO + N_META - (CONV_W - 1), CONV_W - 1)

    @pl.when((n > 0) & (n < total))
    def _():
        prefetch_next_tile()
        x1_old[...] = x1_new[...]
        channel_mixer_up((upbuf, actbuf))
        token_mixer((ubuf, cbuf, mbuf))
        channel_mixer_out((upbuf, actbuf), x1_old)
        send_previous_tile()

    @pl.when(n == total)
    def _():
        channel_mixer_up((upbuf_last, actbuf_last))
        channel_mixer_out((upbuf_last, actbuf_last), x1_new)
        send_previous_tile()

    @pl.when((n % nt == nt - 1) & (n < total))
    def _():
        for j in range(POOL_BUF):
            op_ref[j, pl.ds(n // nt, 1), :] = utail[(j + 1) * SUBLANES - 1:(j + 1) * SUBLANES, :]
        _read_tail(oc_ref.at[n // nt], ctail, CONV_W - 1)

    @pl.when((n % nt == 0) & (n >= 1))
    def _():
        _read_tail(of_ref.at[(n - 1) // nt], ftail, CONV_W - 1)

    @pl.when(n == total)
    def _():
        wait_tile(ybuf.at[slot], out_sem.at[slot])
        wait_tile(ybuf.at[1 - slot], out_sem.at[1 - slot])
        wait_tile(xbuf.at[slot], in_sem.at[slot])
        for cp in pool_out_cps:
            cp.wait()
        conv_out_cp.wait()
        for g in range(max(n_groups - ring_slots, 0), n_groups):
            ffn_out_copy(g).wait()


def _resident(shape, layer=0):
    index = (layer if shape[0] is None else 0,) + (0,) * (len(shape) - 1)
    return pl.BlockSpec(shape, lambda *_: index, pipeline_mode=pl.Buffered(1))


def _fused_call(x, xs, sp, sc, sf, meta, big_weights, pscale, w_conv, w_ffn_conv, gains, l, tm):
    nb, t, d = x.shape
    ns = xs.shape[0]
    seg = tm // SUBLANES
    assert t % tm == 0 and tm % SUBLANES == 0 and seg >= POOL_BUF
    assert xs.shape == (ns, 1, d) and ns == CAST_ROWS and ns % SF_GROUP == 0
    assert sf.shape[1:] == (ns, CONV_W - 1, STAGE_COLS) and CONV_W - 1 == 2
    assert sp.shape == (POOL_BUF, ns, POOL_WIDTH) and sc.shape[1:] == (ns, CONV_W - 1, CONV_WIDTH)
    nt = t // tm
    assert nt >= 2
    total = nb * nt
    hbm = pl.BlockSpec(memory_space=pl.ANY)
    rows_with_head = lambda depth: (depth + seg) * SUBLANES
    return pl.pallas_call(
        functools.partial(_fused_kernel, tm=tm, nt=nt, total=total, layer=l),
        grid=(total + 1,),
        in_specs=[hbm, hbm, hbm, hbm, hbm, _resident(meta.shape)]
                 + [hbm] * len(big_weights)
                 + [_resident(pscale.shape), hbm, hbm]
                 + [_resident(g.shape) for g in gains],
        out_specs=[hbm, hbm,
                   _resident((POOL_BUF, nb, POOL_WIDTH)),
                   _resident((None, nb, CONV_W - 1, CONV_WIDTH), l),
                   _resident((None, nb, CONV_W - 1, 2 * D_FF), l),
                   hbm, hbm, hbm],
        out_shape=[jax.ShapeDtypeStruct((nb, t, d), _F32),
                   jax.ShapeDtypeStruct((ns, 1, d), _F32),
                   jax.ShapeDtypeStruct((POOL_BUF, nb, POOL_WIDTH), _F32),
                   jax.ShapeDtypeStruct((1, nb, CONV_W - 1, CONV_WIDTH), _F32),
                   jax.ShapeDtypeStruct((1, nb, CONV_W - 1, 2 * D_FF), _F32),
                   jax.ShapeDtypeStruct(sp.shape, _F32),
                   jax.ShapeDtypeStruct(sc.shape, _F32),
                   jax.ShapeDtypeStruct(sf.shape, _F32)],
        scratch_shapes=[pltpu.VMEM((d, IN_TOTAL), _BF16),
                        pltpu.VMEM((N_POOL_GROUPS, POOL_GC, POOL_OUT_GC), _BF16),
                        pltpu.VMEM((CONV_WIDTH, d), _BF16),
                        pltpu.VMEM((d, d), _BF16),
                        pltpu.VMEM((d, 2 * D_FF), _BF16),
                        pltpu.VMEM((D_FF, d), _BF16),
                        pltpu.VMEM((CONV_W, CONV_WIDTH), _F32),
                        pltpu.VMEM((CONV_W, 2 * D_FF), _F32),
                        pltpu.VMEM((CAST_SLOTS, CAST_ROWS, STAGE_COLS), _F32),
                        pltpu.VMEM(sp.shape, _F32),
                        pltpu.VMEM(sc.shape[1:], _F32),
                        pltpu.VMEM((POOL_HALO + N_META, POOL_WIDTH), _F32),
                        pltpu.VMEM((CONV_HALO + N_META, CONV_WIDTH), _F32),
                        pltpu.VMEM((CONV_HALO + N_META, 2 * D_FF), _F32),
                        pltpu.VMEM((ns, d), _BF16),
                        pltpu.VMEM((ns, D_FF), _BF16),
                        pltpu.VMEM((ns, d), _BF16),
                        pltpu.VMEM((N_META, d), _BF16),
                        pltpu.VMEM((ns, d), _F32),
                        pltpu.VMEM((ns, d), _F32),
                        pltpu.VMEM((ns, POOL_WIDTH), _F32),
                        pltpu.VMEM((ns, CONV_WIDTH), _F32),
                        pltpu.VMEM((ns, 2 * D_FF), _F32),
                        pltpu.VMEM((SF_SLOTS, SF_GROUP, CONV_W - 1, 2 * D_FF), _F32),
                        pltpu.VMEM((2, seg, SUBLANES, d), _F32),
                        pltpu.VMEM((2, seg, SUBLANES, d), _F32),
                        pltpu.SemaphoreType.DMA((2,)),
                        pltpu.SemaphoreType.DMA((2,)),
                        pltpu.SemaphoreType.DMA((CAST_SLOTS,)),
                        pltpu.SemaphoreType.DMA((11,)),
                        pltpu.SemaphoreType.DMA((2, SF_SLOTS)),
                        pltpu.VMEM((rows_with_head(POOL_BUF), POOL_WIDTH), _F32),
                        pltpu.VMEM((rows_with_head(CONV_W - 1), CONV_WIDTH), _F32),
                        pltpu.VMEM((rows_with_head(CONV_W - 1), 2 * D_FF), _F32),
                        pltpu.VMEM((POOL_BUF * SUBLANES, POOL_WIDTH), _F32),
                        pltpu.VMEM(((CONV_W - 1) * SUBLANES, CONV_WIDTH), _F32),
                        pltpu.VMEM(((CONV_W - 1) * SUBLANES, 2 * D_FF), _F32),
                        pltpu.VMEM((tm, d), _BF16),
                        pltpu.VMEM((tm, D_FF), _BF16),
                        pltpu.VMEM((tm, d), _BF16),
                        pltpu.VMEM((tm, d), _F32),
                        pltpu.VMEM((tm, d), _F32),
                        pltpu.VMEM((rows_with_head(POOL_BUF), POOL_WIDTH), _F32),
                        pltpu.VMEM((rows_with_head(CONV_W - 1), CONV_WIDTH), _F32),
                        pltpu.VMEM((tm, d), _BF16),
                        pltpu.VMEM((rows_with_head(CONV_W - 1), 2 * D_FF), _F32),
                        pltpu.VMEM((tm, D_FF), _BF16)],
        compiler_params=pltpu.CompilerParams(
            dimension_semantics=("arbitrary",), vmem_limit_bytes=VMEM_LIMIT_BYTES),
        name="hybrid_step",
    )(x, xs, sp, sc, sf, meta, *big_weights, pscale, w_conv, w_ffn_conv, *gains)


def kernel(x_prompt, x_sample, state_pool, state_conv, state_ffn, meta_tokens, w_in, w_pool,
           pool_scale, w_conv, w_conv_out, w_o, w_up, w_ffn_conv, w_down, g_pre_mix,
           g_post_mix, g_pre_ffn, g_post_ffn):
    depth = w_in.shape[0]
    assert depth == 1, "the prompt / sample streams are chained for a single layer only"
    assert x_sample.shape[1] == 1 and meta_tokens.shape[0] == N_META
    assert N_META >= max(POOL_WINDOWS)
    row = lambda a: a.reshape(1, -1)
    l = 0
    big = (w_in[l], w_pool[l], w_conv_out[l], w_o[l], w_up[l], w_down[l])
    gains = (row(g_pre_mix[l]), row(g_post_mix[l]), row(g_pre_ffn[l]), row(g_post_ffn[l]))

    rows_first = lambda s: jnp.transpose(s, (1, 0, 2))
    y_prompt, y_sample, pp, pc, pf, sp_new, sc_new, sf_new = _fused_call(
        x_prompt, x_sample, rows_first(state_pool[l]), state_conv, state_ffn,
        meta_tokens.astype(_F32), big, row(pool_scale[l]), rows_first(w_conv),
        rows_first(w_ffn_conv), gains, l, TM_PROMPT)

    return (y_prompt, y_sample, rows_first(pp)[None], pc, pf, rows_first(sp_new)[None],
            sc_new, sf_new)
```

```python
import functools

import jax
import jax.numpy as jnp
from jax import lax
from jax.experimental import pallas as pl
from jax.experimental.pallas import tpu as pltpu

D_MODEL = 1024
N_META = 16
POOL_WINDOWS = (2, 4, 8, 16)
N_POOL_GROUPS = len(POOL_WINDOWS)
POOL_WIDTH = D_MODEL // 2
POOL_GC = POOL_WIDTH // N_POOL_GROUPS
POOL_OUT_GC = D_MODEL // N_POOL_GROUPS
POOL_BUF = max(POOL_WINDOWS) - 1
CONV_WIDTH = D_MODEL // 2
CONV_W = 3
D_FF = 2816
IN_TOTAL = 2 * POOL_WIDTH + 2 * CONV_WIDTH + 2 * D_MODEL
EPS = 1e-6
LOG2_E = 1.4426950408889634
OFF_U, OFF_V, OFF_B, OFF_C, OFF_GA, OFF_GB = 0, 512, 1024, 1536, 2048, 3072

SUBLANES = 8
FF_CHUNK = 256
POOL_HALO = 16
CONV_HALO = 8
TM_PROMPT = 256
CAST_ROWS = 128
CAST_SLOTS = 3
SF_GROUP = 8
SF_SLOTS = 4
STAGE_COLS = 2 * D_FF
VMEM_LIMIT_BYTES = 58 * 1024 * 1024

_BF16 = jnp.bfloat16
_F32 = jnp.float32


def _dot(a, b):
    return jnp.dot(a, b, preferred_element_type=_F32)


def _rms(x, g):
    ms = jnp.mean(x * x, axis=-1, keepdims=True)
    return x * lax.rsqrt(ms + EPS) * g


def _sigmoid(x):
    return 1.0 / (1.0 + jnp.exp2(x * (-LOG2_E)))


def _mixer_in(x, w):
    w_in, g1 = w[0], w[9]
    h = _rms(x, g1[...]).astype(_BF16)
    u = _dot(h, w_in[:, OFF_U:OFF_U + POOL_WIDTH])
    v = _dot(h, w_in[:, OFF_V:OFF_V + CONV_WIDTH])
    cg = _dot(h, w_in[:, OFF_C:OFF_C + CONV_WIDTH])
    return h, u, cg * v


def _mixer(x, w, hist, mbuf, pool_div):
    _mixer_body(*_mixer_in(x, w), w, hist, mbuf, pool_div)


def _mixer_body(h, u, cvin, w, hist, mbuf, pool_div):
    (w_in, w_pool, pscale, wconv, wco, _, _, _, _, _, _, _, _) = w
    hist.put_u(u)
    hist.put_c(cvin)
    cv = wconv[CONV_W - 1:CONV_W, :] * hist.get_c(0)
    for k in range(1, CONV_W):
        cv = cv + wconv[CONV_W - 1 - k:CONV_W - k, :] * hist.get_c(k)
    bg = _dot(h, w_in[:, OFF_B:OFF_B + CONV_WIDTH])
    b_in = (bg * cv).astype(_BF16)

    for g, win in enumerate(POOL_WINDOWS):
        cs = slice(g * POOL_GC, (g + 1) * POOL_GC)
        os_ = slice(g * POOL_OUT_GC, (g + 1) * POOL_OUT_GC)
        ug = hist.get_u(0, cs)
        s = ug
        for k in range(1, win):
            s = s + hist.get_u(k, cs)
        pooled = (pool_div(s, win) - ug).astype(_BF16)
        a = _dot(pooled, w_pool[g]) * pscale[:, os_]
        ga = _dot(h, w_in[:, OFF_GA + g * POOL_OUT_GC:OFF_GA + (g + 1) * POOL_OUT_GC])
        gb = _dot(h, w_in[:, OFF_GB + g * POOL_OUT_GC:OFF_GB + (g + 1) * POOL_OUT_GC])
        b = _dot(b_in, wco[:, os_])
        mbuf[:, os_] = (_sigmoid(ga) * a + _sigmoid(gb) * b).astype(_BF16)


def _mixer_out(w, mbuf, rows=slice(None)):
    return _dot(mbuf[rows, :], w[5][...])


def _mixer_residual(x, mixed, w):
    return x + _rms(mixed, w[10][...])


def _ffn_norm(x1, w):
    return _rms(x1, w[11][...]).astype(_BF16)


def _ffn_up(h2, w, hist, actbuf):
    wup, wfc = w[6], w[7]
    for c in range(D_FF // FF_CHUNK):
        conv = []
        for col in (c * FF_CHUNK, D_FF + c * FF_CHUNK):
            cols = slice(col, col + FF_CHUNK)
            hist.put_up(_dot(h2, wup[:, cols]), cols)
            y = wfc[CONV_W - 1:CONV_W, cols] * hist.get_up(0, cols)
            for k in range(1, CONV_W):
                y = y + wfc[CONV_W - 1 - k:CONV_W - k, cols] * hist.get_up(k, cols)
            conv.append(y)
        gate, val = conv
        actbuf[:, c * FF_CHUNK:(c + 1) * FF_CHUNK] = (gate * _sigmoid(gate) * val).astype(_BF16)


def _ffn_down(w, actbuf, rows=slice(None)):
    return _dot(actbuf[rows, :], w[8][...])


def _ffn_residual(x1, down, w):
    return x1 + _rms(down, w[12][...])


def _full_windows(s, win):
    return s * (1.0 / win)


class _RowHistory:
    def __init__(self, ubuf, cbuf, upbuf, m):
        self.ubuf, self.cbuf, self.upbuf, self.m = ubuf, cbuf, upbuf, m

    def put_u(self, val):
        self.ubuf[POOL_HALO:POOL_HALO + self.m, :] = val

    def get_u(self, k, cs):
        return self.ubuf[POOL_HALO - k:POOL_HALO - k + self.m, cs]

    def put_c(self, val):
        self.cbuf[CONV_HALO:CONV_HALO + self.m, :] = val

    def get_c(self, k):
        return self.cbuf[CONV_HALO - k:CONV_HALO - k + self.m, :]

    def put_up(self, val, cols):
        self.upbuf[CONV_HALO:CONV_HALO + self.m, cols] = val

    def get_up(self, k, cols):
        return self.upbuf[CONV_HALO - k:CONV_HALO - k + self.m, cols]


class _SegmentHistory:
    def __init__(self, ubuf, cbuf, upbuf, utail, ctail, ftail, tm):
        self.ubuf, self.cbuf, self.upbuf, self.tm = ubuf, cbuf, upbuf, tm
        self.utail, self.ctail, self.ftail = utail, ctail, ftail

    def _put(self, buf, tail, val, cols, depth):
        tm = self.tm
        buf[depth * SUBLANES:depth * SUBLANES + tm, cols] = val
        sublane = lax.broadcasted_iota(jnp.int32, (SUBLANES, val.shape[1]), 0)
        for j in range(depth):
            rows = slice(j * SUBLANES, (j + 1) * SUBLANES)
            cur = val[tm - (depth - j) * SUBLANES:tm - (depth - j - 1) * SUBLANES, :]
            mixed = jnp.where(sublane == SUBLANES - 1, tail[rows, cols], cur)
            buf[rows, cols] = pltpu.roll(mixed, 1, axis=0)
            tail[rows, cols] = cur

    def _get(self, buf, k, cols, depth):
        start = (depth - k) * SUBLANES
        return buf[start:start + self.tm, cols]

    def put_u(self, val):
        self._put(self.ubuf, self.utail, val, slice(None), POOL_BUF)

    def get_u(self, k, cs):
        return self._get(self.ubuf, k, cs, POOL_BUF)

    def put_c(self, val):
        self._put(self.cbuf, self.ctail, val, slice(None), CONV_W - 1)

    def get_c(self, k):
        return self._get(self.cbuf, k, slice(None), CONV_W - 1)

    def put_up(self, val, cols):
        self._put(self.upbuf, self.ftail, val, cols, CONV_W - 1)

    def get_up(self, k, cols):
        return self._get(self.upbuf, k, cols, CONV_W - 1)


class _StateHistory:
    def __init__(self, sp_ref, sc_ref, sf_ref, u_ref, c_ref, f_ref):
        self.sp, self.sc, self.sf = sp_ref, sc_ref, sf_ref
        self.u, self.c, self.f = u_ref, c_ref, f_ref

    def put_u(self, val):
        self.u[...] = val

    def get_u(self, k, cs):
        return self.u[:, cs] if k == 0 else self.sp[POOL_BUF - k, :, cs]

    def put_c(self, val):
        self.c[...] = val

    def get_c(self, k):
        return self.c[...] if k == 0 else self.sc[:, CONV_W - 1 - k, :]

    def put_up(self, val, cols):
        self.f[:, cols] = val

    def get_up(self, k, cols):
        return self.f[:, cols] if k == 0 else self.sf[CONV_W - 1 - k, :, cols]


def _fill_tail(tail, state_ref, first_row, depth):
    for j in range(depth):
        r = first_row + j
        tail[j * SUBLANES:(j + 1) * SUBLANES, :] = jnp.broadcast_to(
            state_ref[r:r + 1, :], (SUBLANES, tail.shape[1]))


def _read_tail(out_ref, tail, depth):
    for j in range(depth):
        out_ref[j:j + 1, :] = tail[(j + 1) * SUBLANES - 1:(j + 1) * SUBLANES, :]


def _cast_batches(pairs):
    jobs = []
    for src, dst in pairs:
        if len(src.shape) == 3:
            assert src.shape[1] == CAST_ROWS
            jobs += [(src.at[g], dst.at[g], src.shape[2]) for g in range(src.shape[0])]
        else:
            assert src.shape[0] % CAST_ROWS == 0
            jobs += [(src.at[pl.ds(r, CAST_ROWS), :], dst.at[pl.ds(r, CAST_ROWS), :], src.shape[1])
                     for r in range(0, src.shape[0], CAST_ROWS)]
    batches, cur, used = [], [], 0
    for s, d, cols in jobs:
        if used + cols > STAGE_COLS:
            batches.append(cur)
            cur, used = [], 0
        cur.append((s, d, used, cols))
        used += cols
    batches.append(cur)
    return batches


def _cast_weights(first_pairs, later_pairs, between, stage, sem):
    slots = stage.shape[0]
    first = _cast_batches(first_pairs)
    batches = first + _cast_batches(later_pairs)
    hooks = {len(first) + 2 * i: fn for i, fn in enumerate(between)}
    assert not hooks or max(hooks) < len(batches)

    def copies(b):
        return [pltpu.make_async_copy(s, stage.at[b % slots, :, pl.ds(lane, cols)],
                                      sem.at[b % slots]) for s, _, lane, cols in batches[b]]

    for b in range(min(slots, len(batches))):
        for cp in copies(b):
            cp.start()
    for b in range(len(batches)):
        for cp in copies(b):
            cp.wait()
        for _, d, lane, cols in batches[b]:
            d[...] = stage[b % slots, :, lane:lane + cols].astype(_BF16)
        if b + slots < len(batches):
            for cp in copies(b + slots):
                cp.start()
        if b in hooks:
            hooks[b]()


def _fused_kernel(x_hbm, xs_hbm, sp_hbm, sc_hbm, sf_hbm, meta_ref,
                  w_in_hbm, w_pool_hbm, wco_hbm, wo_hbm, wup_hbm, wdown_hbm,
                  pscale, wconv_hbm, wfc_hbm, g1, g2, g3, g4,
                  y_hbm, ys_hbm, op_ref, oc_ref, of_ref, spo_hbm, sco_hbm, sfo_hbm,
                  w_in, w_pool, wco, wo, wup, wdown, wconv, wfc, stage, sp_v, sc_v,
                  m_ubuf, m_cbuf, m_upbuf,
                  s_mbuf, s_actbuf, s_h2, m_h2, xs_v, ys_v, us_v, cs_v, sf_ring, xbuf, ybuf,
                  in_sem, out_sem, cast_sem, state_sem, ring_sem,
                  ubuf, cbuf, upbuf, utail, ctail, ftail, mbuf, actbuf, h2buf, x1_new, x1_old,
                  ubuf_first, cbuf_first, mbuf_first, upbuf_last, actbuf_last,
                  h_carry, u_carry, c_carry,
                  *, tm, nt, total, layer):
    w = (w_in, w_pool, pscale, wconv, wco, wo, wup, wfc, wdown, g1, g2, g3, g4)
    seg = tm // SUBLANES
    n = pl.program_id(0)
    slot = n % 2

    def x_copies(tile, sl):
        b, t0 = tile // nt, (tile % nt) * tm
        return [pltpu.make_async_copy(x_hbm.at[b, pl.ds(t0 + s * seg, seg), :],
                                      xbuf.at[sl, :, s, :], in_sem.at[sl]) for s in range(SUBLANES)]

    def y_copies(tile, sl):
        b, t0 = tile // nt, (tile % nt) * tm
        return [pltpu.make_async_copy(ybuf.at[sl, :, s, :],
                                      y_hbm.at[b, pl.ds(t0 + s * seg, seg), :], out_sem.at[sl])
                for s in range(SUBLANES)]

    ring_slots, group = sf_ring.shape[:2]
    n_groups = xs_v.shape[0] // group
    pool_out_cps = [
        pltpu.make_async_copy(sp_hbm.at[pl.ds(1, POOL_BUF - 1)], spo_hbm.at[pl.ds(0, POOL_BUF - 1)],
                              state_sem.at[6]),
        pltpu.make_async_copy(us_v, spo_hbm.at[POOL_BUF - 1], state_sem.at[6])]
    conv_out_cp = pltpu.make_async_copy(sc_v, sco_hbm.at[layer], state_sem.at[9])

    def ffn_in_copy(g):
        return pltpu.make_async_copy(sf_hbm.at[layer, pl.ds(g * group, group)],
                                     sf_ring.at[g % ring_slots], ring_sem.at[0, g % ring_slots])

    def ffn_out_copy(g):
        return pltpu.make_async_copy(sf_ring.at[g % ring_slots],
                                     sfo_hbm.at[layer, pl.ds(g * group, group)],
                                     ring_sem.at[1, g % ring_slots])

    def wait_tile(buf, sem):
        pltpu.make_async_copy(buf, buf, sem).wait()

    def start_sequence_mixer_history():
        _fill_tail(utail, m_ubuf, POOL_HALO + N_META - POOL_BUF, POOL_BUF)
        _fill_tail(ctail, m_cbuf, CONV_HALO + N_META - (CONV_W - 1), CONV_W - 1)

    xslot = n % 3
    next_xslot = (n + 1) % 3

    def token_mixer_head(sl):
        h, u, cvin = _mixer_in(xbuf[sl].reshape(tm, D_MODEL), w)
        h_carry[...] = h
        u_carry[...] = u
        c_carry[...] = cvin

    def token_mixer(work):
        ub, cb, mb = work
        hist = _SegmentHistory(ub, cb, None, utail, ctail, None, tm)
        _mixer_body(h_carry[...], u_carry[...], c_carry[...], w, hist, mb, _full_windows)
        x1 = _mixer_residual(xbuf[xslot].reshape(tm, D_MODEL), _mixer_out(w, mb), w)
        x1_new[...] = x1
        h2buf[...] = _ffn_norm(x1, w)

    def channel_mixer_up(work):
        upb, ab = work
        _ffn_up(h2buf[...], w, _SegmentHistory(None, None, upb, None, None, ftail, tm), ab)

    def channel_mixer_out(work, x1_ref):
        y = _ffn_residual(x1_ref[...], _ffn_down(w, work[1]), w)
        ybuf[1 - slot] = y.reshape(seg, SUBLANES, D_MODEL)

    def prefetch_tile_after_next():
        for cp in x_copies(jnp.minimum(n + 2, total - 1), (n + 2) % 3):
            cp.start()

    def send_previous_tile():
        for cp in y_copies(n - 1, 1 - slot):
            cp.start()

    @pl.when(n == 0)
    def _():
        for tile in range(2):
            for cp in x_copies(tile, tile):
                cp.start()
        prefetch_tile_after_next()
        pool_cp = pltpu.make_async_copy(sp_hbm, sp_v, state_sem.at[0])
        conv_cp = pltpu.make_async_copy(sc_hbm.at[layer], sc_v, state_sem.at[1])
        wconv_cp = pltpu.make_async_copy(wconv_hbm.at[:, layer, :], wconv, state_sem.at[10])
        wfc_cp = pltpu.make_async_copy(wfc_hbm.at[:, layer, :], wfc, state_sem.at[10])
        wconv_cp.start()
        wfc_cp.start()
        xs_cp = pltpu.make_async_copy(xs_hbm.at[:, 0, :], xs_v, state_sem.at[4])
        ys_cp = pltpu.make_async_copy(ys_v, ys_hbm.at[:, 0, :], state_sem.at[5])
        pool_cp.start()
        conv_cp.start()
        xs_cp.start()

        m_ubuf[0:POOL_HALO, :] = jnp.zeros((POOL_HALO, POOL_WIDTH), _F32)
        m_cbuf[0:CONV_HALO, :] = jnp.zeros((CONV_HALO, CONV_WIDTH), _F32)
        m_upbuf[0:CONV_HALO, :] = jnp.zeros((CONV_HALO, 2 * D_FF), _F32)
        row1 = lax.broadcasted_iota(jnp.int32, (N_META, POOL_GC), 0) + 1

        def meta_windows(s, win):
            return s / jnp.minimum(win, row1).astype(_F32)

        meta_hist = _RowHistory(m_ubuf, m_cbuf, m_upbuf, N_META)
        meta_rows = pl.ds(0, N_META)

        def meta_token_mixer():
            wconv_cp.wait()
            wfc_cp.wait()
            _mixer(meta_ref[...], w, meta_hist, s_mbuf.at[meta_rows], meta_windows)
            x1 = _mixer_residual(meta_ref[...], _mixer_out(w, s_mbuf.at[meta_rows]), w)
            m_h2[...] = _ffn_norm(x1, w)

        s_up = stage.at[CONV_W - 1]
        sample_hist = _StateHistory(sp_v, sc_v, stage, us_v, cs_v, s_up)

        def sample_token_mixer():
            pool_cp.wait()
            conv_cp.wait()
            xs_cp.wait()
            _mixer(xs_v[...], w, sample_hist, s_mbuf, _full_windows)
            x1 = _mixer_residual(xs_v[...], _mixer_out(w, s_mbuf), w)
            ys_v[...] = x1
            s_h2[...] = _ffn_norm(x1, w)
            for cp in pool_out_cps:
                cp.start()
            for j in range(CONV_W - 2):
                sc_v[:, j, :] = sc_v[:, j + 1, :]
            sc_v[:, CONV_W - 2, :] = cs_v[...]
            conv_out_cp.start()

        def first_tile_token_mixer():
            wait_tile(xbuf.at[0], in_sem.at[0])
            wait_tile(xbuf.at[1], in_sem.at[1])
            start_sequence_mixer_history()
            token_mixer_head(0)
            token_mixer((ubuf_first, cbuf_first, mbuf_first))
            token_mixer_head(1)

        _cast_weights([(w_in_hbm, w_in), (w_pool_hbm, w_pool), (wco_hbm, wco), (wo_hbm, wo)],
                      [(wup_hbm, wup), (wdown_hbm, wdown)],
                      [meta_token_mixer, sample_token_mixer, first_tile_token_mixer],
                      stage, cast_sem)
        for g in range(min(ring_slots - 1, n_groups)):
            ffn_in_copy(g).start()
        _ffn_up(m_h2[...], w, meta_hist, s_actbuf.at[meta_rows])
        for g in range(n_groups):
            if g + ring_slots - 1 < n_groups:
                ffn_in_copy(g + ring_slots - 1).start()
            ffn_in_copy(g).wait()
            for k in range(CONV_W - 1):
                stage[k, g * group:(g + 1) * group, :] = sf_ring[g % ring_slots, :, k, :]

        _ffn_up(s_h2[...], w, sample_hist, s_actbuf)
        ys_v[...] = _ffn_residual(ys_v[...], _ffn_down(w, s_actbuf), w)
        ys_cp.start()

        for g in range(n_groups):
            if g >= ring_slots:
                ffn_out_copy(g - ring_slots).wait()
            rows = slice(g * group, (g + 1) * group)
            for k in range(1, CONV_W - 1):
                sf_ring[g % ring_slots, :, k - 1, :] = stage[k, rows, :]
            sf_ring[g % ring_slots, :, CONV_W - 2, :] = s_up[rows, :]
            ffn_out_copy(g).start()
        ys_cp.wait()

    @pl.when((n > 0) & (n < total))
    def _():
        wait_tile(xbuf.at[next_xslot], in_sem.at[next_xslot])

    @pl.when(n >= 3)
    def _():
        wait_tile(ybuf.at[1 - slot], out_sem.at[1 - slot])

    @pl.when((n % nt == 0) & (n > 0))
    def _():
        start_sequence_mixer_history()

    @pl.when(n % nt == 1)
    def _():
        _fill_tail(ftail, m_upbuf, CONV_HALO + N_META - (CONV_W - 1), CONV_W - 1)

    @pl.when((n > 0) & (n < total))
    def _():
        prefetch_tile_after_next()
        x1_old[...] = x1_new[...]
        channel_mixer_up((upbuf, actbuf))
        token_mixer((ubuf, cbuf, mbuf))
        channel_mixer_out((upbuf, actbuf), x1_old)
        token_mixer_head(next_xslot)
        send_previous_tile()

    @pl.when(n == total)
    def _():
        channel_mixer_up((upbuf_last, actbuf_last))
        channel_mixer_out((upbuf_last, actbuf_last), x1_new)
        send_previous_tile()

    @pl.when((n % nt == nt - 1) & (n < total))
    def _():
        for j in range(POOL_BUF):
            op_ref[j, pl.ds(n // nt, 1), :] = utail[(j + 1) * SUBLANES - 1:(j + 1) * SUBLANES, :]
        _read_tail(oc_ref.at[n // nt], ctail, CONV_W - 1)

    @pl.when((n % nt == 0) & (n >= 1))
    def _():
        _read_tail(of_ref.at[(n - 1) // nt], ftail, CONV_W - 1)

    @pl.when(n == total)
    def _():
        wait_tile(ybuf.at[slot], out_sem.at[slot])
        wait_tile(ybuf.at[1 - slot], out_sem.at[1 - slot])
        wait_tile(xbuf.at[next_xslot], in_sem.at[next_xslot])
        for cp in pool_out_cps:
            cp.wait()
        conv_out_cp.wait()
        for g in range(max(n_groups - ring_slots, 0), n_groups):
            ffn_out_copy(g).wait()


def _resident(shape, layer=0):
    index = (layer if shape[0] is None else 0,) + (0,) * (len(shape) - 1)
    return pl.BlockSpec(shape, lambda *_: index, pipeline_mode=pl.Buffered(1))


def _fused_call(x, xs, sp, sc, sf, meta, big_weights, pscale, w_conv, w_ffn_conv, gains, l, tm):
    nb, t, d = x.shape
    ns = xs.shape[0]
    seg = tm // SUBLANES
    assert t % tm == 0 and tm % SUBLANES == 0 and seg >= POOL_BUF
    assert xs.shape == (ns, 1, d) and ns == CAST_ROWS and ns % SF_GROUP == 0
    assert sf.shape[1:] == (ns, CONV_W - 1, STAGE_COLS) and CAST_SLOTS >= CONV_W
    assert sp.shape == (POOL_BUF, ns, POOL_WIDTH) and sc.shape[1:] == (ns, CONV_W - 1, CONV_WIDTH)
    nt = t // tm
    assert nt >= 2
    total = nb * nt
    hbm = pl.BlockSpec(memory_space=pl.ANY)
    rows_with_head = lambda depth: (depth + seg) * SUBLANES
    return pl.pallas_call(
        functools.partial(_fused_kernel, tm=tm, nt=nt, total=total, layer=l),
        grid=(total + 1,),
        in_specs=[hbm, hbm, hbm, hbm, hbm, _resident(meta.shape)]
                 + [hbm] * len(big_weights)
                 + [_resident(pscale.shape), hbm, hbm]
                 + [_resident(g.shape) for g in gains],
        out_specs=[hbm, hbm,
                   _resident((POOL_BUF, nb, POOL_WIDTH)),
                   _resident((None, nb, CONV_W - 1, CONV_WIDTH), l),
                   _resident((None, nb, CONV_W - 1, 2 * D_FF), l),
                   hbm, hbm, hbm],
        out_shape=[jax.ShapeDtypeStruct((nb, t, d), _F32),
                   jax.ShapeDtypeStruct((ns, 1, d), _F32),
                   jax.ShapeDtypeStruct((POOL_BUF, nb, POOL_WIDTH), _F32),
                   jax.ShapeDtypeStruct((1, nb, CONV_W - 1, CONV_WIDTH), _F32),
                   jax.ShapeDtypeStruct((1, nb, CONV_W - 1, 2 * D_FF), _F32),
                   jax.ShapeDtypeStruct(sp.shape, _F32),
                   jax.ShapeDtypeStruct(sc.shape, _F32),
                   jax.ShapeDtypeStruct(sf.shape, _F32)],
        scratch_shapes=[pltpu.VMEM((d, IN_TOTAL), _BF16),
                        pltpu.VMEM((N_POOL_GROUPS, POOL_GC, POOL_OUT_GC), _BF16),
                        pltpu.VMEM((CONV_WIDTH, d), _BF16),
                        pltpu.VMEM((d, d), _BF16),
                        pltpu.VMEM((d, 2 * D_FF), _BF16),
                        pltpu.VMEM((D_FF, d), _BF16),
                        pltpu.VMEM((CONV_W, CONV_WIDTH), _F32),
                        pltpu.VMEM((CONV_W, 2 * D_FF), _F32),
                        pltpu.VMEM((CAST_SLOTS, CAST_ROWS, STAGE_COLS), _F32),
                        pltpu.VMEM(sp.shape, _F32),
                        pltpu.VMEM(sc.shape[1:], _F32),
                        pltpu.VMEM((POOL_HALO + N_META, POOL_WIDTH), _F32),
                        pltpu.VMEM((CONV_HALO + N_META, CONV_WIDTH), _F32),
                        pltpu.VMEM((CONV_HALO + N_META, 2 * D_FF), _F32),
                        pltpu.VMEM((ns, d), _BF16),
                        pltpu.VMEM((ns, D_FF), _BF16),
                        pltpu.VMEM((ns, d), _BF16),
                        pltpu.VMEM((N_META, d), _BF16),
                        pltpu.VMEM((ns, d), _F32),
                        pltpu.VMEM((ns, d), _F32),
                        pltpu.VMEM((ns, POOL_WIDTH), _F32),
                        pltpu.VMEM((ns, CONV_WIDTH), _F32),
                        pltpu.VMEM((SF_SLOTS, SF_GROUP, CONV_W - 1, 2 * D_FF), _F32),
                        pltpu.VMEM((3, seg, SUBLANES, d), _F32),
                        pltpu.VMEM((2, seg, SUBLANES, d), _F32),
                        pltpu.SemaphoreType.DMA((3,)),
                        pltpu.SemaphoreType.DMA((2,)),
                        pltpu.SemaphoreType.DMA((CAST_SLOTS,)),
                        pltpu.SemaphoreType.DMA((11,)),
                        pltpu.SemaphoreType.DMA((2, SF_SLOTS)),
                        pltpu.VMEM((rows_with_head(POOL_BUF), POOL_WIDTH), _F32),
                        pltpu.VMEM((rows_with_head(CONV_W - 1), CONV_WIDTH), _F32),
                        pltpu.VMEM((rows_with_head(CONV_W - 1), 2 * D_FF), _F32),
                        pltpu.VMEM((POOL_BUF * SUBLANES, POOL_WIDTH), _F32),
                        pltpu.VMEM(((CONV_W - 1) * SUBLANES, CONV_WIDTH), _F32),
                        pltpu.VMEM(((CONV_W - 1) * SUBLANES, 2 * D_FF), _F32),
                        pltpu.VMEM((tm, d), _BF16),
                        pltpu.VMEM((tm, D_FF), _BF16),
                        pltpu.VMEM((tm, d), _BF16),
                        pltpu.VMEM((tm, d), _F32),
                        pltpu.VMEM((tm, d), _F32),
                        pltpu.VMEM((rows_with_head(POOL_BUF), POOL_WIDTH), _F32),
                        pltpu.VMEM((rows_with_head(CONV_W - 1), CONV_WIDTH), _F32),
                        pltpu.VMEM((tm, d), _BF16),
                        pltpu.VMEM((rows_with_head(CONV_W - 1), 2 * D_FF), _F32),
                        pltpu.VMEM((tm, D_FF), _BF16),
                        pltpu.VMEM((tm, d), _BF16),
                        pltpu.VMEM((tm, POOL_WIDTH), _F32),
                        pltpu.VMEM((tm, CONV_WIDTH), _F32)],
        compiler_params=pltpu.CompilerParams(
            dimension_semantics=("arbitrary",), vmem_limit_bytes=VMEM_LIMIT_BYTES),
        name="hybrid_step",
    )(x, xs, sp, sc, sf, meta, *big_weights, pscale, w_conv, w_ffn_conv, *gains)


def kernel(x_prompt, x_sample, state_pool, state_conv, state_ffn, meta_tokens, w_in, w_pool,
           pool_scale, w_conv, w_conv_out, w_o, w_up, w_ffn_conv, w_down, g_pre_mix,
           g_post_mix, g_pre_ffn, g_post_ffn):
    depth = w_in.shape[0]
    assert depth == 1, "the prompt / sample streams are chained for a single layer only"
    assert x_sample.shape[1] == 1 and meta_tokens.shape[0] == N_META
    assert N_META >= max(POOL_WINDOWS)
    row = lambda a: a.reshape(1, -1)
    l = 0
    big = (w_in[l], w_pool[l], w_conv_out[l], w_o[l], w_up[l], w_down[l])
    gains = (row(g_pre_mix[l]), row(g_post_mix[l]), row(g_pre_ffn[l]), row(g_post_ffn[l]))

    rows_first = lambda s: jnp.transpose(s, (1, 0, 2))
    y_prompt, y_sample, pp, pc, pf, sp_new, sc_new, sf_new = _fused_call(
        x_prompt, x_sample, rows_first(state_pool[l]), state_conv, state_ffn,
        meta_tokens.astype(_F32), big, row(pool_scale[l]), rows_first(w_conv),
        rows_first(w_ffn_conv), gains, l, TM_PROMPT)

    return (y_prompt, y_sample, rows_first(pp)[None], pc, pf, rows_first(sp_new)[None],
            sc_new, sf_new)
```

```python
import functools

import jax
import jax.numpy as jnp
from jax import lax
from jax.experimental import pallas as pl
from jax.experimental.pallas import tpu as pltpu

D_MODEL = 1024
N_META = 16
POOL_WINDOWS = (2, 4, 8, 16)
N_POOL_GROUPS = len(POOL_WINDOWS)
POOL_WIDTH = D_MODEL // 2
POOL_GC = POOL_WIDTH // N_POOL_GROUPS
POOL_OUT_GC = D_MODEL // N_POOL_GROUPS
POOL_BUF = max(POOL_WINDOWS) - 1
CONV_WIDTH = D_MODEL // 2
CONV_W = 3
D_FF = 2816
IN_TOTAL = 2 * POOL_WIDTH + 2 * CONV_WIDTH + 2 * D_MODEL
EPS = 1e-6
LOG2_E = 1.4426950408889634
OFF_U, OFF_V, OFF_B, OFF_C, OFF_GA, OFF_GB = 0, 512, 1024, 1536, 2048, 3072

SUBLANES = 8
FF_CHUNK = 256
POOL_HALO = 16
CONV_HALO = 8
TM_PROMPT = 256
CAST_ROWS = 128
CAST_SLOTS = 3
SF_GROUP = 8
SF_SLOTS = 4
STAGE_COLS = 2 * D_FF
VMEM_LIMIT_BYTES = 58 * 1024 * 1024

_BF16 = jnp.bfloat16
_F32 = jnp.float32


def _dot(a, b):
    return jnp.dot(a, b, preferred_element_type=_F32)


def _rms(x, g):
    ms = jnp.mean(x * x, axis=-1, keepdims=True)
    return x * lax.rsqrt(ms + EPS) * g


def _sigmoid(x):
    return 1.0 / (1.0 + jnp.exp2(x * (-LOG2_E)))


def _mixer(x, w, hist, mbuf, pool_div):
    (w_in, w_pool, pscale, wconv, wco, _, _, _, _, g1, _, _, _) = w
    h = _rms(x, g1[...]).astype(_BF16)

    hist.put_u(_dot(h, w_in[:, OFF_U:OFF_U + POOL_WIDTH]))
    v = _dot(h, w_in[:, OFF_V:OFF_V + CONV_WIDTH])
    cg = _dot(h, w_in[:, OFF_C:OFF_C + CONV_WIDTH])
    hist.put_c(cg * v)
    cv = wconv[CONV_W - 1:CONV_W, :] * hist.get_c(0)
    for k in range(1, CONV_W):
        cv = cv + wconv[CONV_W - 1 - k:CONV_W - k, :] * hist.get_c(k)
    bg = _dot(h, w_in[:, OFF_B:OFF_B + CONV_WIDTH])
    b_in = (bg * cv).astype(_BF16)

    for g, win in enumerate(POOL_WINDOWS):
        cs = slice(g * POOL_GC, (g + 1) * POOL_GC)
        os_ = slice(g * POOL_OUT_GC, (g + 1) * POOL_OUT_GC)
        ug = hist.get_u(0, cs)
        s = ug
        for k in range(1, win):
            s = s + hist.get_u(k, cs)
        pooled = (pool_div(s, win) - ug).astype(_BF16)
        a = _dot(pooled, w_pool[g]) * pscale[:, os_]
        ga = _dot(h, w_in[:, OFF_GA + g * POOL_OUT_GC:OFF_GA + (g + 1) * POOL_OUT_GC])
        gb = _dot(h, w_in[:, OFF_GB + g * POOL_OUT_GC:OFF_GB + (g + 1) * POOL_OUT_GC])
        b = _dot(b_in, wco[:, os_])
        mbuf[:, os_] = (_sigmoid(ga) * a + _sigmoid(gb) * b).astype(_BF16)


def _mixer_out(w, mbuf, rows=slice(None)):
    return _dot(mbuf[rows, :], w[5][...])


def _mixer_residual(x, mixed, w):
    return x + _rms(mixed, w[10][...])


def _ffn_norm(x1, w):
    return _rms(x1, w[11][...]).astype(_BF16)


def _ffn_up(h2, w, hist, actbuf):
    wup, wfc = w[6], w[7]
    for c in range(D_FF // FF_CHUNK):
        conv = []
        for col in (c * FF_CHUNK, D_FF + c * FF_CHUNK):
            cols = slice(col, col + FF_CHUNK)
            hist.put_up(_dot(h2, wup[:, cols]), cols)
            y = wfc[CONV_W - 1:CONV_W, cols] * hist.get_up(0, cols)
            for k in range(1, CONV_W):
                y = y + wfc[CONV_W - 1 - k:CONV_W - k, cols] * hist.get_up(k, cols)
            conv.append(y)
        gate, val = conv
        actbuf[:, c * FF_CHUNK:(c + 1) * FF_CHUNK] = (gate * _sigmoid(gate) * val).astype(_BF16)


def _ffn_down(w, actbuf, rows=slice(None)):
    return _dot(actbuf[rows, :], w[8][...])


def _ffn_residual(x1, down, w):
    return x1 + _rms(down, w[12][...])


def _full_windows(s, win):
    return s * (1.0 / win)


class _RowHistory:
    def __init__(self, ubuf, cbuf, upbuf, m):
        self.ubuf, self.cbuf, self.upbuf, self.m = ubuf, cbuf, upbuf, m

    def put_u(self, val):
        self.ubuf[POOL_HALO:POOL_HALO + self.m, :] = val

    def get_u(self, k, cs):
        return self.ubuf[POOL_HALO - k:POOL_HALO - k + self.m, cs]

    def put_c(self, val):
        self.cbuf[CONV_HALO:CONV_HALO + self.m, :] = val

    def get_c(self, k):
        return self.cbuf[CONV_HALO - k:CONV_HALO - k + self.m, :]

    def put_up(self, val, cols):
        self.upbuf[CONV_HALO:CONV_HALO + self.m, cols] = val

    def get_up(self, k, cols):
        return self.upbuf[CONV_HALO - k:CONV_HALO - k + self.m, cols]


class _SegmentHistory:
    def __init__(self, ubuf, cbuf, upbuf, utail, ctail, ftail, tm):
        self.ubuf, self.cbuf, self.upbuf, self.tm = ubuf, cbuf, upbuf, tm
        self.utail, self.ctail, self.ftail = utail, ctail, ftail

    def _put(self, buf, tail, val, cols, depth):
        tm = self.tm
        buf[depth * SUBLANES:depth * SUBLANES + tm, cols] = val
        sublane = lax.broadcasted_iota(jnp.int32, (SUBLANES, val.shape[1]), 0)
        for j in range(depth):
            rows = slice(j * SUBLANES, (j + 1) * SUBLANES)
            cur = val[tm - (depth - j) * SUBLANES:tm - (depth - j - 1) * SUBLANES, :]
            mixed = jnp.where(sublane == SUBLANES - 1, tail[rows, cols], cur)
            buf[rows, cols] = pltpu.roll(mixed, 1, axis=0)
            tail[rows, cols] = cur

    def _get(self, buf, k, cols, depth):
        start = (depth - k) * SUBLANES
        return buf[start:start + self.tm, cols]

    def put_u(self, val):
        self._put(self.ubuf, self.utail, val, slice(None), POOL_BUF)

    def get_u(self, k, cs):
        return self._get(self.ubuf, k, cs, POOL_BUF)

    def put_c(self, val):
        self._put(self.cbuf, self.ctail, val, slice(None), CONV_W - 1)

    def get_c(self, k):
        return self._get(self.cbuf, k, slice(None), CONV_W - 1)

    def put_up(self, val, cols):
        self._put(self.upbuf, self.ftail, val, cols, CONV_W - 1)

    def get_up(self, k, cols):
        return self._get(self.upbuf, k, cols, CONV_W - 1)


class _StateHistory:
    def __init__(self, sp_ref, sc_ref, sf_ref, u_ref, c_ref, f_ref):
        self.sp, self.sc, self.sf = sp_ref, sc_ref, sf_ref
        self.u, self.c, self.f = u_ref, c_ref, f_ref

    def put_u(self, val):
        self.u[...] = val

    def get_u(self, k, cs):
        return self.u[:, cs] if k == 0 else self.sp[POOL_BUF - k, :, cs]

    def put_c(self, val):
        self.c[...] = val

    def get_c(self, k):
        return self.c[...] if k == 0 else self.sc[:, CONV_W - 1 - k, :]

    def put_up(self, val, cols):
        self.f[:, cols] = val

    def get_up(self, k, cols):
        return self.f[:, cols] if k == 0 else self.sf[CONV_W - 1 - k, :, cols]


def _fill_tail(tail, state_ref, first_row, depth):
    for j in range(depth):
        r = first_row + j
        tail[j * SUBLANES:(j + 1) * SUBLANES, :] = jnp.broadcast_to(
            state_ref[r:r + 1, :], (SUBLANES, tail.shape[1]))


def _read_tail(out_ref, tail, depth):
    for j in range(depth):
        out_ref[j:j + 1, :] = tail[(j + 1) * SUBLANES - 1:(j + 1) * SUBLANES, :]


def _cast_batches(pairs):
    jobs = []
    for src, dst in pairs:
        if len(src.shape) == 3:
            assert src.shape[1] == CAST_ROWS
            jobs += [(src.at[g], dst.at[g], src.shape[2]) for g in range(src.shape[0])]
        else:
            assert src.shape[0] % CAST_ROWS == 0
            jobs += [(src.at[pl.ds(r, CAST_ROWS), :], dst.at[pl.ds(r, CAST_ROWS), :], src.shape[1])
                     for r in range(0, src.shape[0], CAST_ROWS)]
    batches, cur, used = [], [], 0
    for s, d, cols in jobs:
        if used + cols > STAGE_COLS:
            batches.append(cur)
            cur, used = [], 0
        cur.append((s, d, used, cols))
        used += cols
    batches.append(cur)
    return batches


def _cast_weights(first_pairs, later_pairs, between, stage, sem):
    slots = stage.shape[0]
    first = _cast_batches(first_pairs)
    batches = first + _cast_batches(later_pairs)
    hooks = {len(first) + 2 * i: fn for i, fn in enumerate(between)}
    assert not hooks or max(hooks) < len(batches)

    def copies(b):
        return [pltpu.make_async_copy(s, stage.at[b % slots, :, pl.ds(lane, cols)],
                                      sem.at[b % slots]) for s, _, lane, cols in batches[b]]

    for b in range(min(slots, len(batches))):
        for cp in copies(b):
            cp.start()
    for b in range(len(batches)):
        for cp in copies(b):
            cp.wait()
        for _, d, lane, cols in batches[b]:
            d[...] = stage[b % slots, :, lane:lane + cols].astype(_BF16)
        if b + slots < len(batches):
            for cp in copies(b + slots):
                cp.start()
        if b in hooks:
            hooks[b]()


def _fused_kernel(x_hbm, xs_hbm, sp_hbm, sc_hbm, sf_hbm, meta_ref,
                  w_in_hbm, w_pool_hbm, wco_hbm, wo_hbm, wup_hbm, wdown_hbm,
                  pscale, wconv_hbm, wfc_hbm, g1, g2, g3, g4,
                  y_hbm, ys_hbm, op_ref, oc_ref, of_ref, spo_hbm, sco_hbm, sfo_hbm,
                  w_in, w_pool, wco, wo, wup, wdown, wconv, wfc, stage, sp_v, sc_v,
                  m_ubuf, m_cbuf, m_upbuf,
                  s_mbuf, s_actbuf, s_h2, m_h2, xs_v, ys_v, us_v, cs_v, sf_ring, xbuf, ybuf,
                  in_sem, out_sem, cast_sem, state_sem, ring_sem,
                  ubuf, cbuf, upbuf, utail, ctail, ftail, mbuf, actbuf, h2buf, x1_new, x1_old,
                  ubuf_first, cbuf_first, mbuf_first, upbuf_last, actbuf_last,
                  *, tm, nt, total, layer):
    w = (w_in, w_pool, pscale, wconv, wco, wo, wup, wfc, wdown, g1, g2, g3, g4)
    seg = tm // SUBLANES
    n = pl.program_id(0)
    slot = n % 2

    def x_copies(tile, sl):
        b, t0 = tile // nt, (tile % nt) * tm
        return [pltpu.make_async_copy(x_hbm.at[b, pl.ds(t0 + s * seg, seg), :],
                                      xbuf.at[sl, :, s, :], in_sem.at[sl]) for s in range(SUBLANES)]

    def y_copies(tile, sl):
        b, t0 = tile // nt, (tile % nt) * tm
        return [pltpu.make_async_copy(ybuf.at[sl, :, s, :],
                                      y_hbm.at[b, pl.ds(t0 + s * seg, seg), :], out_sem.at[sl])
                for s in range(SUBLANES)]

    ring_slots, group = sf_ring.shape[:2]
    n_groups = xs_v.shape[0] // group
    pool_out_cps = [
        pltpu.make_async_copy(sp_hbm.at[pl.ds(1, POOL_BUF - 1)], spo_hbm.at[pl.ds(0, POOL_BUF - 1)],
                              state_sem.at[6]),
        pltpu.make_async_copy(us_v, spo_hbm.at[POOL_BUF - 1], state_sem.at[6])]
    conv_out_cp = pltpu.make_async_copy(sc_v, sco_hbm.at[layer], state_sem.at[9])

    def ffn_in_copy(g):
        return pltpu.make_async_copy(sf_hbm.at[layer, pl.ds(g * group, group)],
                                     sf_ring.at[g % ring_slots], ring_sem.at[0, g % ring_slots])

    def ffn_out_copy(g):
        return pltpu.make_async_copy(sf_ring.at[g % ring_slots],
                                     sfo_hbm.at[layer, pl.ds(g * group, group)],
                                     ring_sem.at[1, g % ring_slots])

    def wait_tile(buf, sem):
        pltpu.make_async_copy(buf, buf, sem).wait()

    def start_sequence_mixer_history():
        _fill_tail(utail, m_ubuf, POOL_HALO + N_META - POOL_BUF, POOL_BUF)
        _fill_tail(ctail, m_cbuf, CONV_HALO + N_META - (CONV_W - 1), CONV_W - 1)

    def token_mixer(work):
        ub, cb, mb = work
        hist = _SegmentHistory(ub, cb, None, utail, ctail, None, tm)
        x = xbuf[slot].reshape(tm, D_MODEL)
        _mixer(x, w, hist, mb, _full_windows)
        x1 = _mixer_residual(x, _mixer_out(w, mb), w)
        x1_new[...] = x1
        h2buf[...] = _ffn_norm(x1, w)

    def channel_mixer_up(work):
        upb, ab = work
        _ffn_up(h2buf[...], w, _SegmentHistory(None, None, upb, None, None, ftail, tm), ab)

    def channel_mixer_out(work, x1_ref):
        y = _ffn_residual(x1_ref[...], _ffn_down(w, work[1]), w)
        ybuf[1 - slot] = y.reshape(seg, SUBLANES, D_MODEL)

    def prefetch_next_tile():
        for cp in x_copies(jnp.minimum(n + 1, total - 1), 1 - slot):
            cp.start()

    def send_previous_tile():
        for cp in y_copies(n - 1, 1 - slot):
            cp.start()

    @pl.when(n == 0)
    def _():
        for cp in x_copies(n, slot):
            cp.start()
        pool_cp = pltpu.make_async_copy(sp_hbm, sp_v, state_sem.at[0])
        conv_cp = pltpu.make_async_copy(sc_hbm.at[layer], sc_v, state_sem.at[1])
        wconv_cp = pltpu.make_async_copy(wconv_hbm.at[:, layer, :], wconv, state_sem.at[10])
        wfc_cp = pltpu.make_async_copy(wfc_hbm.at[:, layer, :], wfc, state_sem.at[10])
        wconv_cp.start()
        wfc_cp.start()
        xs_cp = pltpu.make_async_copy(xs_hbm.at[:, 0, :], xs_v, state_sem.at[4])
        ys_cp = pltpu.make_async_copy(ys_v, ys_hbm.at[:, 0, :], state_sem.at[5])
        pool_cp.start()
        conv_cp.start()
        xs_cp.start()

        m_ubuf[0:POOL_HALO, :] = jnp.zeros((POOL_HALO, POOL_WIDTH), _F32)
        m_cbuf[0:CONV_HALO, :] = jnp.zeros((CONV_HALO, CONV_WIDTH), _F32)
        m_upbuf[0:CONV_HALO, :] = jnp.zeros((CONV_HALO, 2 * D_FF), _F32)
        row1 = lax.broadcasted_iota(jnp.int32, (N_META, POOL_GC), 0) + 1

        def meta_windows(s, win):
            return s / jnp.minimum(win, row1).astype(_F32)

        meta_hist = _RowHistory(m_ubuf, m_cbuf, m_upbuf, N_META)
        meta_rows = pl.ds(0, N_META)

        def meta_token_mixer():
            wconv_cp.wait()
            wfc_cp.wait()
            _mixer(meta_ref[...], w, meta_hist, s_mbuf.at[meta_rows], meta_windows)
            x1 = _mixer_residual(meta_ref[...], _mixer_out(w, s_mbuf.at[meta_rows]), w)
            m_h2[...] = _ffn_norm(x1, w)

        s_up = stage.at[CONV_W - 1]
        sample_hist = _StateHistory(sp_v, sc_v, stage, us_v, cs_v, s_up)

        def sample_token_mixer():
            pool_cp.wait()
            conv_cp.wait()
            xs_cp.wait()
            _mixer(xs_v[...], w, sample_hist, s_mbuf, _full_windows)
            x1 = _mixer_residual(xs_v[...], _mixer_out(w, s_mbuf), w)
            ys_v[...] = x1
            s_h2[...] = _ffn_norm(x1, w)
            for cp in pool_out_cps:
                cp.start()
            for j in range(CONV_W - 2):
                sc_v[:, j, :] = sc_v[:, j + 1, :]
            sc_v[:, CONV_W - 2, :] = cs_v[...]
            conv_out_cp.start()

        def first_tile_token_mixer():
            wait_tile(xbuf.at[slot], in_sem.at[slot])
            prefetch_next_tile()
            start_sequence_mixer_history()
            token_mixer((ubuf_first, cbuf_first, mbuf_first))

        _cast_weights([(w_in_hbm, w_in), (w_pool_hbm, w_pool), (wco_hbm, wco), (wo_hbm, wo)],
                      [(wup_hbm, wup), (wdown_hbm, wdown)],
                      [meta_token_mixer, sample_token_mixer, first_tile_token_mixer],
                      stage, cast_sem)
        for g in range(min(ring_slots - 1, n_groups)):
            ffn_in_copy(g).start()
        _ffn_up(m_h2[...], w, meta_hist, s_actbuf.at[meta_rows])
        for g in range(n_groups):
            if g + ring_slots - 1 < n_groups:
                ffn_in_copy(g + ring_slots - 1).start()
            ffn_in_copy(g).wait()
            for k in range(CONV_W - 1):
                stage[k, g * group:(g + 1) * group, :] = sf_ring[g % ring_slots, :, k, :]

        _ffn_up(s_h2[...], w, sample_hist, s_actbuf)
        ys_v[...] = _ffn_residual(ys_v[...], _ffn_down(w, s_actbuf), w)
        ys_cp.start()

        for g in range(n_groups):
            if g >= ring_slots:
                ffn_out_copy(g - ring_slots).wait()
            rows = slice(g * group, (g + 1) * group)
            for k in range(1, CONV_W - 1):
                sf_ring[g % ring_slots, :, k - 1, :] = stage[k, rows, :]
            sf_ring[g % ring_slots, :, CONV_W - 2, :] = s_up[rows, :]
            ffn_out_copy(g).start()
        ys_cp.wait()

    @pl.when((n > 0) & (n < total))
    def _():
        wait_tile(xbuf.at[slot], in_sem.at[slot])

    @pl.when(n >= 3)
    def _():
        wait_tile(ybuf.at[1 - slot], out_sem.at[1 - slot])

    @pl.when((n % nt == 0) & (n > 0))
    def _():
        start_sequence_mixer_history()

    @pl.when(n % nt == 1)
    def _():
        _fill_tail(ftail, m_upbuf, CONV_HALO + N_META - (CONV_W - 1), CONV_W - 1)

    @pl.when((n > 0) & (n < total))
    def _():
        prefetch_next_tile()
        x1_old[...] = x1_new[...]
        channel_mixer_up((upbuf, actbuf))
        token_mixer((ubuf, cbuf, mbuf))
        channel_mixer_out((upbuf, actbuf), x1_old)
        send_previous_tile()

    @pl.when(n == total)
    def _():
        channel_mixer_up((upbuf_last, actbuf_last))
        channel_mixer_out((upbuf_last, actbuf_last), x1_new)
        send_previous_tile()

    @pl.when((n % nt == nt - 1) & (n < total))
    def _():
        for j in range(POOL_BUF):
            op_ref[j, pl.ds(n // nt, 1), :] = utail[(j + 1) * SUBLANES - 1:(j + 1) * SUBLANES, :]
        _read_tail(oc_ref.at[n // nt], ctail, CONV_W - 1)

    @pl.when((n % nt == 0) & (n >= 1))
    def _():
        _read_tail(of_ref.at[(n - 1) // nt], ftail, CONV_W - 1)

    @pl.when(n == total)
    def _():
        wait_tile(ybuf.at[slot], out_sem.at[slot])
        wait_tile(ybuf.at[1 - slot], out_sem.at[1 - slot])
        wait_tile(xbuf.at[slot], in_sem.at[slot])
        for cp in pool_out_cps:
            cp.wait()
        conv_out_cp.wait()
        for g in range(max(n_groups - ring_slots, 0), n_groups):
            ffn_out_copy(g).wait()


def _resident(shape, layer=0):
    index = (layer if shape[0] is None else 0,) + (0,) * (len(shape) - 1)
    return pl.BlockSpec(shape, lambda *_: index, pipeline_mode=pl.Buffered(1))


def _fused_call(x, xs, sp, sc, sf, meta, big_weights, pscale, w_conv, w_ffn_conv, gains, l, tm):
    nb, t, d = x.shape
    ns = xs.shape[0]
    seg = tm // SUBLANES
    assert t % tm == 0 and tm % SUBLANES == 0 and seg >= POOL_BUF
    assert xs.shape == (ns, 1, d) and ns == CAST_ROWS and ns % SF_GROUP == 0
    assert sf.shape[1:] == (ns, CONV_W - 1, STAGE_COLS) and CAST_SLOTS >= CONV_W
    assert sp.shape == (POOL_BUF, ns, POOL_WIDTH) and sc.shape[1:] == (ns, CONV_W - 1, CONV_WIDTH)
    nt = t // tm
    assert nt >= 2
    total = nb * nt
    hbm = pl.BlockSpec(memory_space=pl.ANY)
    rows_with_head = lambda depth: (depth + seg) * SUBLANES
    return pl.pallas_call(
        functools.partial(_fused_kernel, tm=tm, nt=nt, total=total, layer=l),
        grid=(total + 1,),
        in_specs=[hbm, hbm, hbm, hbm, hbm, _resident(meta.shape)]
                 + [hbm] * len(big_weights)
                 + [_resident(pscale.shape), hbm, hbm]
                 + [_resident(g.shape) for g in gains],
        out_specs=[hbm, hbm,
                   _resident((POOL_BUF, nb, POOL_WIDTH)),
                   _resident((None, nb, CONV_W - 1, CONV_WIDTH), l),
                   _resident((None, nb, CONV_W - 1, 2 * D_FF), l),
                   hbm, hbm, hbm],
        out_shape=[jax.ShapeDtypeStruct((nb, t, d), _F32),
                   jax.ShapeDtypeStruct((ns, 1, d), _F32),
                   jax.ShapeDtypeStruct((POOL_BUF, nb, POOL_WIDTH), _F32),
                   jax.ShapeDtypeStruct((1, nb, CONV_W - 1, CONV_WIDTH), _F32),
                   jax.ShapeDtypeStruct((1, nb, CONV_W - 1, 2 * D_FF), _F32),
                   jax.ShapeDtypeStruct(sp.shape, _F32),
                   jax.ShapeDtypeStruct(sc.shape, _F32),
                   jax.ShapeDtypeStruct(sf.shape, _F32)],
        scratch_shapes=[pltpu.VMEM((d, IN_TOTAL), _BF16),
                        pltpu.VMEM((N_POOL_GROUPS, POOL_GC, POOL_OUT_GC), _BF16),
                        pltpu.VMEM((CONV_WIDTH, d), _BF16),
                        pltpu.VMEM((d, d), _BF16),
                        pltpu.VMEM((d, 2 * D_FF), _BF16),
                        pltpu.VMEM((D_FF, d), _BF16),
                        pltpu.VMEM((CONV_W, CONV_WIDTH), _F32),
                        pltpu.VMEM((CONV_W, 2 * D_FF), _F32),
                        pltpu.VMEM((CAST_SLOTS, CAST_ROWS, STAGE_COLS), _F32),
                        pltpu.VMEM(sp.shape, _F32),
                        pltpu.VMEM(sc.shape[1:], _F32),
                        pltpu.VMEM((POOL_HALO + N_META, POOL_WIDTH), _F32),
                        pltpu.VMEM((CONV_HALO + N_META, CONV_WIDTH), _F32),
                        pltpu.VMEM((CONV_HALO + N_META, 2 * D_FF), _F32),
                        pltpu.VMEM((ns, d), _BF16),
                        pltpu.VMEM((ns, D_FF), _BF16),
                        pltpu.VMEM((ns, d), _BF16),
                        pltpu.VMEM((N_META, d), _BF16),
                        pltpu.VMEM((ns, d), _F32),
                        pltpu.VMEM((ns, d), _F32),
                        pltpu.VMEM((ns, POOL_WIDTH), _F32),
                        pltpu.VMEM((ns, CONV_WIDTH), _F32),
                        pltpu.VMEM((SF_SLOTS, SF_GROUP, CONV_W - 1, 2 * D_FF), _F32),
                        pltpu.VMEM((2, seg, SUBLANES, d), _F32),
                        pltpu.VMEM((2, seg, SUBLANES, d), _F32),
                        pltpu.SemaphoreType.DMA((2,)),
                        pltpu.SemaphoreType.DMA((2,)),
                        pltpu.SemaphoreType.DMA((CAST_SLOTS,)),
                        pltpu.SemaphoreType.DMA((11,)),
                        pltpu.SemaphoreType.DMA((2, SF_SLOTS)),
                        pltpu.VMEM((rows_with_head(POOL_BUF), POOL_WIDTH), _F32),
                        pltpu.VMEM((rows_with_head(CONV_W - 1), CONV_WIDTH), _F32),
                        pltpu.VMEM((rows_with_head(CONV_W - 1), 2 * D_FF), _F32),
                        pltpu.VMEM((POOL_BUF * SUBLANES, POOL_WIDTH), _F32),
                        pltpu.VMEM(((CONV_W - 1) * SUBLANES, CONV_WIDTH), _F32),
                        pltpu.VMEM(((CONV_W - 1) * SUBLANES, 2 * D_FF), _F32),
                        pltpu.VMEM((tm, d), _BF16),
                        pltpu.VMEM((tm, D_FF), _BF16),
                        pltpu.VMEM((tm, d), _BF16),
                        pltpu.VMEM((tm, d), _F32),
                        pltpu.VMEM((tm, d), _F32),
                        pltpu.VMEM((rows_with_head(POOL_BUF), POOL_WIDTH), _F32),
                        pltpu.VMEM((rows_with_head(CONV_W - 1), CONV_WIDTH), _F32),
                        pltpu.VMEM((tm, d), _BF16),
                        pltpu.VMEM((rows_with_head(CONV_W - 1), 2 * D_FF), _F32),
                        pltpu.VMEM((tm, D_FF), _BF16)],
        compiler_params=pltpu.CompilerParams(
            dimension_semantics=("arbitrary",), vmem_limit_bytes=VMEM_LIMIT_BYTES),
        name="hybrid_step",
    )(x, xs, sp, sc, sf, meta, *big_weights, pscale, w_conv, w_ffn_conv, *gains)


def kernel(x_prompt, x_sample, state_pool, state_conv, state_ffn, meta_tokens, w_in, w_pool,
           pool_scale, w_conv, w_conv_out, w_o, w_up, w_ffn_conv, w_down, g_pre_mix,
           g_post_mix, g_pre_ffn, g_post_ffn):
    depth = w_in.shape[0]
    assert depth == 1, "the prompt / sample streams are chained for a single layer only"
    assert x_sample.shape[1] == 1 and meta_tokens.shape[0] == N_META
    assert N_META >= max(POOL_WINDOWS)
    row = lambda a: a.reshape(1, -1)
    l = 0
    big = (w_in[l], w_pool[l], w_conv_out[l], w_o[l], w_up[l], w_down[l])
    gains = (row(g_pre_mix[l]), row(g_post_mix[l]), row(g_pre_ffn[l]), row(g_post_ffn[l]))

    rows_first = lambda s: jnp.transpose(s, (1, 0, 2))
    y_prompt, y_sample, pp, pc, pf, sp_new, sc_new, sf_new = _fused_call(
        x_prompt, x_sample, rows_first(state_pool[l]), state_conv, state_ffn,
        meta_tokens.astype(_F32), big, row(pool_scale[l]), rows_first(w_conv),
        rows_first(w_ffn_conv), gains, l, TM_PROMPT)

    return (y_prompt, y_sample, rows_first(pp)[None], pc, pf, rows_first(sp_new)[None],
            sc_new, sf_new)
```

```python
import functools

import jax
import jax.numpy as jnp
from jax import lax
from jax.experimental import pallas as pl
from jax.experimental.pallas import tpu as pltpu

D_MODEL = 1024
N_META = 16
POOL_WINDOWS = (2, 4, 8, 16)
N_POOL_GROUPS = len(POOL_WINDOWS)
POOL_WIDTH = D_MODEL // 2
POOL_GC = POOL_WIDTH // N_POOL_GROUPS
POOL_OUT_GC = D_MODEL // N_POOL_GROUPS
POOL_BUF = max(POOL_WINDOWS) - 1
CONV_WIDTH = D_MODEL // 2
CONV_W = 3
D_FF = 2816
IN_TOTAL = 2 * POOL_WIDTH + 2 * CONV_WIDTH + 2 * D_MODEL
EPS = 1e-6
LOG2_E = 1.4426950408889634
OFF_U, OFF_V, OFF_B, OFF_C, OFF_GA, OFF_GB = 0, 512, 1024, 1536, 2048, 3072

SUBLANES = 8
FF_CHUNK = 256
POOL_HALO = 16
CONV_HALO = 8
TM_PROMPT = 256
CAST_ROWS = 128
CAST_SLOTS = 3
SF_GROUP = 8
SF_SLOTS = 4
STAGE_COLS = 2 * D_FF
VMEM_LIMIT_BYTES = 58 * 1024 * 1024

_BF16 = jnp.bfloat16
_F32 = jnp.float32

(_SEM_POOL_IN, _SEM_CONV_IN, _SEM_X_SAMPLE, _SEM_Y_SAMPLE, _SEM_POOL_OUT, _SEM_CONV_OUT,
 _SEM_CONV_WEIGHTS, _N_STATE_SEMS) = range(8)


def _dot(a, b):
    return jnp.dot(a, b, preferred_element_type=_F32)


def _rms(x, g):
    ms = jnp.mean(x * x, axis=-1, keepdims=True)
    return x * lax.rsqrt(ms + EPS) * g


def _sigmoid(x):
    return 1.0 / (1.0 + jnp.exp2(x * (-LOG2_E)))


def _mixer(x, w, hist, mbuf, pool_div):
    (w_in, w_pool, pscale, wconv, wco, _, _, _, _, g1, _, _, _) = w
    h = _rms(x, g1[...]).astype(_BF16)

    hist.put_u(_dot(h, w_in[:, OFF_U:OFF_U + POOL_WIDTH]))
    v = _dot(h, w_in[:, OFF_V:OFF_V + CONV_WIDTH])
    cg = _dot(h, w_in[:, OFF_C:OFF_C + CONV_WIDTH])
    hist.put_c(cg * v)
    cv = wconv[CONV_W - 1:CONV_W, :] * hist.get_c(0)
    for k in range(1, CONV_W):
        cv = cv + wconv[CONV_W - 1 - k:CONV_W - k, :] * hist.get_c(k)
    bg = _dot(h, w_in[:, OFF_B:OFF_B + CONV_WIDTH])
    b_in = (bg * cv).astype(_BF16)

    for g, win in enumerate(POOL_WINDOWS):
        cs = slice(g * POOL_GC, (g + 1) * POOL_GC)
        os_ = slice(g * POOL_OUT_GC, (g + 1) * POOL_OUT_GC)
        ug = hist.get_u(0, cs)
        s = ug
        for k in range(1, win):
            s = s + hist.get_u(k, cs)
        pooled = (pool_div(s, win) - ug).astype(_BF16)
        a = _dot(pooled, w_pool[g]) * pscale[:, os_]
        ga = _dot(h, w_in[:, OFF_GA + g * POOL_OUT_GC:OFF_GA + (g + 1) * POOL_OUT_GC])
        gb = _dot(h, w_in[:, OFF_GB + g * POOL_OUT_GC:OFF_GB + (g + 1) * POOL_OUT_GC])
        b = _dot(b_in, wco[:, os_])
        mbuf[:, os_] = (_sigmoid(ga) * a + _sigmoid(gb) * b).astype(_BF16)


def _mixer_out(w, mbuf, rows=slice(None)):
    return _dot(mbuf[rows, :], w[5][...])


def _mixer_residual(x, mixed, w):
    return x + _rms(mixed, w[10][...])


def _ffn_norm(x1, w):
    return _rms(x1, w[11][...]).astype(_BF16)


def _ffn_up(h2, w, hist, actbuf):
    wup, wfc = w[6], w[7]
    for c in range(D_FF // FF_CHUNK):
        conv = []
        for col in (c * FF_CHUNK, D_FF + c * FF_CHUNK):
            cols = slice(col, col + FF_CHUNK)
            hist.put_up(_dot(h2, wup[:, cols]), cols)
            y = wfc[CONV_W - 1:CONV_W, cols] * hist.get_up(0, cols)
            for k in range(1, CONV_W):
                y = y + wfc[CONV_W - 1 - k:CONV_W - k, cols] * hist.get_up(k, cols)
            conv.append(y)
        gate, val = conv
        actbuf[:, c * FF_CHUNK:(c + 1) * FF_CHUNK] = (gate * _sigmoid(gate) * val).astype(_BF16)


def _ffn_down(w, actbuf, rows=slice(None)):
    return _dot(actbuf[rows, :], w[8][...])


def _ffn_residual(x1, down, w):
    return x1 + _rms(down, w[12][...])


def _full_windows(s, win):
    return s * (1.0 / win)


class _RowHistory:
    def __init__(self, ubuf, cbuf, upbuf, m):
        self.ubuf, self.cbuf, self.upbuf, self.m = ubuf, cbuf, upbuf, m

    def put_u(self, val):
        self.ubuf[POOL_HALO:POOL_HALO + self.m, :] = val

    def get_u(self, k, cs):
        return self.ubuf[POOL_HALO - k:POOL_HALO - k + self.m, cs]

    def put_c(self, val):
        self.cbuf[CONV_HALO:CONV_HALO + self.m, :] = val

    def get_c(self, k):
        return self.cbuf[CONV_HALO - k:CONV_HALO - k + self.m, :]

    def put_up(self, val, cols):
        self.upbuf[CONV_HALO:CONV_HALO + self.m, cols] = val

    def get_up(self, k, cols):
        return self.upbuf[CONV_HALO - k:CONV_HALO - k + self.m, cols]


class _SegmentHistory:
    def __init__(self, ubuf, cbuf, upbuf, utail, ctail, ftail, tm):
        self.ubuf, self.cbuf, self.upbuf, self.tm = ubuf, cbuf, upbuf, tm
        self.utail, self.ctail, self.ftail = utail, ctail, ftail

    def _put(self, buf, tail, val, cols, depth):
        tm = self.tm
        buf[depth * SUBLANES:depth * SUBLANES + tm, cols] = val
        sublane = lax.broadcasted_iota(jnp.int32, (SUBLANES, val.shape[1]), 0)
        for j in range(depth):
            rows = slice(j * SUBLANES, (j + 1) * SUBLANES)
            cur = val[tm - (depth - j) * SUBLANES:tm - (depth - j - 1) * SUBLANES, :]
            mixed = jnp.where(sublane == SUBLANES - 1, tail[rows, cols], cur)
            buf[rows, cols] = pltpu.roll(mixed, 1, axis=0)
            tail[rows, cols] = cur

    def _get(self, buf, k, cols, depth):
        start = (depth - k) * SUBLANES
        return buf[start:start + self.tm, cols]

    def put_u(self, val):
        self._put(self.ubuf, self.utail, val, slice(None), POOL_BUF)

    def get_u(self, k, cs):
        return self._get(self.ubuf, k, cs, POOL_BUF)

    def put_c(self, val):
        self._put(self.cbuf, self.ctail, val, slice(None), CONV_W - 1)

    def get_c(self, k):
        return self._get(self.cbuf, k, slice(None), CONV_W - 1)

    def put_up(self, val, cols):
        self._put(self.upbuf, self.ftail, val, cols, CONV_W - 1)

    def get_up(self, k, cols):
        return self._get(self.upbuf, k, cols, CONV_W - 1)


class _StateHistory:
    def __init__(self, sp_ref, sc_ref, sf_ref, u_ref, c_ref, f_ref):
        self.sp, self.sc, self.sf = sp_ref, sc_ref, sf_ref
        self.u, self.c, self.f = u_ref, c_ref, f_ref

    def put_u(self, val):
        self.u[...] = val

    def get_u(self, k, cs):
        return self.u[:, cs] if k == 0 else self.sp[POOL_BUF - k, :, cs]

    def put_c(self, val):
        self.c[...] = val

    def get_c(self, k):
        return self.c[...] if k == 0 else self.sc[:, CONV_W - 1 - k, :]

    def put_up(self, val, cols):
        self.f[:, cols] = val

    def get_up(self, k, cols):
        return self.f[:, cols] if k == 0 else self.sf[CONV_W - 1 - k, :, cols]


def _fill_tail(tail, state_ref, first_row, depth):
    for j in range(depth):
        r = first_row + j
        tail[j * SUBLANES:(j + 1) * SUBLANES, :] = jnp.broadcast_to(
            state_ref[r:r + 1, :], (SUBLANES, tail.shape[1]))


def _read_tail(out_ref, tail, depth):
    for j in range(depth):
        out_ref[j:j + 1, :] = tail[(j + 1) * SUBLANES - 1:(j + 1) * SUBLANES, :]


def _cast_batches(pairs):
    jobs = []
    for src, dst in pairs:
        if len(src.shape) == 3:
            assert src.shape[1] == CAST_ROWS
            jobs += [(src.at[g], dst.at[g], src.shape[2]) for g in range(src.shape[0])]
        else:
            assert src.shape[0] % CAST_ROWS == 0
            jobs += [(src.at[pl.ds(r, CAST_ROWS), :], dst.at[pl.ds(r, CAST_ROWS), :], src.shape[1])
                     for r in range(0, src.shape[0], CAST_ROWS)]
    batches, cur, used = [], [], 0
    for s, d, cols in jobs:
        if used + cols > STAGE_COLS:
            batches.append(cur)
            cur, used = [], 0
        cur.append((s, d, used, cols))
        used += cols
    batches.append(cur)
    return batches


def _cast_weights(first_pairs, later_pairs, between, stage, sem):
    slots = stage.shape[0]
    first = _cast_batches(first_pairs)
    batches = first + _cast_batches(later_pairs)
    hooks = {len(first) + 2 * i: fn for i, fn in enumerate(between)}
    assert not hooks or max(hooks) < len(batches)

    def copies(b):
        return [pltpu.make_async_copy(s, stage.at[b % slots, :, pl.ds(lane, cols)],
                                      sem.at[b % slots]) for s, _, lane, cols in batches[b]]

    for b in range(min(slots, len(batches))):
        for cp in copies(b):
            cp.start()
    for b in range(len(batches)):
        for cp in copies(b):
            cp.wait()
        for _, d, lane, cols in batches[b]:
            d[...] = stage[b % slots, :, lane:lane + cols].astype(_BF16)
        if b + slots < len(batches):
            for cp in copies(b + slots):
                cp.start()
        if b in hooks:
            hooks[b]()


def _fused_kernel(x_hbm, xs_hbm, sp_hbm, sc_hbm, sf_hbm, meta_ref,
                  w_in_hbm, w_pool_hbm, wco_hbm, wo_hbm, wup_hbm, wdown_hbm,
                  pscale, wconv_hbm, wfc_hbm, g1, g2, g3, g4,
                  y_hbm, ys_hbm, op_ref, oc_ref, of_ref, spo_hbm, sco_hbm, sfo_hbm,
                  w_in, w_pool, wco, wo, wup, wdown, wconv, wfc, stage, sp_v, sc_v,
                  m_ubuf, m_cbuf, m_upbuf,
                  s_mbuf, s_actbuf, s_h2, m_h2, xs_v, ys_v, us_v, cs_v, sf_ring, xbuf, ybuf,
                  in_sem, out_sem, cast_sem, state_sem, ring_sem,
                  ubuf, cbuf, upbuf, utail, ctail, ftail, mbuf, actbuf, h2buf, x1_new, x1_old,
                  ubuf_first, cbuf_first, mbuf_first, upbuf_last, actbuf_last,
                  *, tm, nt, total, layer):
    w = (w_in, w_pool, pscale, wconv, wco, wo, wup, wfc, wdown, g1, g2, g3, g4)
    seg = tm // SUBLANES
    n = pl.program_id(0)
    slot = n % 2

    def x_copies(tile, sl):
        b, t0 = tile // nt, (tile % nt) * tm
        return [pltpu.make_async_copy(x_hbm.at[b, pl.ds(t0 + s * seg, seg), :],
                                      xbuf.at[sl, :, s, :], in_sem.at[sl]) for s in range(SUBLANES)]

    def y_copies(tile, sl):
        b, t0 = tile // nt, (tile % nt) * tm
        return [pltpu.make_async_copy(ybuf.at[sl, :, s, :],
                                      y_hbm.at[b, pl.ds(t0 + s * seg, seg), :], out_sem.at[sl])
                for s in range(SUBLANES)]

    ring_slots, group = sf_ring.shape[:2]
    n_groups = xs_v.shape[0] // group
    pool_out_cps = [
        pltpu.make_async_copy(sp_hbm.at[pl.ds(1, POOL_BUF - 1)], spo_hbm.at[pl.ds(0, POOL_BUF - 1)],
                              state_sem.at[_SEM_POOL_OUT]),
        pltpu.make_async_copy(us_v, spo_hbm.at[POOL_BUF - 1], state_sem.at[_SEM_POOL_OUT])]
    conv_out_cp = pltpu.make_async_copy(sc_v, sco_hbm.at[layer], state_sem.at[_SEM_CONV_OUT])

    def ffn_in_copy(g):
        return pltpu.make_async_copy(sf_hbm.at[layer, pl.ds(g * group, group)],
                                     sf_ring.at[g % ring_slots], ring_sem.at[0, g % ring_slots])

    def ffn_out_copy(g):
        return pltpu.make_async_copy(sf_ring.at[g % ring_slots],
                                     sfo_hbm.at[layer, pl.ds(g * group, group)],
                                     ring_sem.at[1, g % ring_slots])

    def wait_tile(buf, sem):
        pltpu.make_async_copy(buf, buf, sem).wait()

    def start_sequence_mixer_history():
        _fill_tail(utail, m_ubuf, POOL_HALO + N_META - POOL_BUF, POOL_BUF)
        _fill_tail(ctail, m_cbuf, CONV_HALO + N_META - (CONV_W - 1), CONV_W - 1)

    def token_mixer(work):
        ub, cb, mb = work
        hist = _SegmentHistory(ub, cb, None, utail, ctail, None, tm)
        x = xbuf[slot].reshape(tm, D_MODEL)
        _mixer(x, w, hist, mb, _full_windows)
        x1 = _mixer_residual(x, _mixer_out(w, mb), w)
        x1_new[...] = x1
        h2buf[...] = _ffn_norm(x1, w)

    def channel_mixer_up(work):
        upb, ab = work
        _ffn_up(h2buf[...], w, _SegmentHistory(None, None, upb, None, None, ftail, tm), ab)

    def channel_mixer_out(work, x1_ref):
        y = _ffn_residual(x1_ref[...], _ffn_down(w, work[1]), w)
        ybuf[1 - slot] = y.reshape(seg, SUBLANES, D_MODEL)

    def prefetch_next_tile():
        for cp in x_copies(jnp.minimum(n + 1, total - 1), 1 - slot):
            cp.start()

    def send_previous_tile():
        for cp in y_copies(n - 1, 1 - slot):
            cp.start()

    @pl.when(n == 0)
    def _():
        for cp in x_copies(n, slot):
            cp.start()
        pool_cp = pltpu.make_async_copy(sp_hbm, sp_v, state_sem.at[_SEM_POOL_IN])
        conv_cp = pltpu.make_async_copy(sc_hbm.at[layer], sc_v, state_sem.at[_SEM_CONV_IN])
        wconv_cp = pltpu.make_async_copy(wconv_hbm.at[:, layer, :], wconv,
                                         state_sem.at[_SEM_CONV_WEIGHTS])
        wfc_cp = pltpu.make_async_copy(wfc_hbm.at[:, layer, :], wfc,
                                       state_sem.at[_SEM_CONV_WEIGHTS])
        wconv_cp.start()
        wfc_cp.start()
        xs_cp = pltpu.make_async_copy(xs_hbm.at[:, 0, :], xs_v, state_sem.at[_SEM_X_SAMPLE])
        ys_cp = pltpu.make_async_copy(ys_v, ys_hbm.at[:, 0, :], state_sem.at[_SEM_Y_SAMPLE])
        pool_cp.start()
        conv_cp.start()
        xs_cp.start()
        for g in range(min(ring_slots - 1, n_groups)):
            ffn_in_copy(g).start()

        m_ubuf[0:POOL_HALO, :] = jnp.zeros((POOL_HALO, POOL_WIDTH), _F32)
        m_cbuf[0:CONV_HALO, :] = jnp.zeros((CONV_HALO, CONV_WIDTH), _F32)
        m_upbuf[0:CONV_HALO, :] = jnp.zeros((CONV_HALO, 2 * D_FF), _F32)
        row1 = lax.broadcasted_iota(jnp.int32, (N_META, POOL_GC), 0) + 1

        def meta_windows(s, win):
            return s / jnp.minimum(win, row1).astype(_F32)

        meta_hist = _RowHistory(m_ubuf, m_cbuf, m_upbuf, N_META)
        meta_rows = pl.ds(0, N_META)

        def meta_token_mixer():
            wconv_cp.wait()
            wfc_cp.wait()
            _mixer(meta_ref[...], w, meta_hist, s_mbuf.at[meta_rows], meta_windows)
            x1 = _mixer_residual(meta_ref[...], _mixer_out(w, s_mbuf.at[meta_rows]), w)
            m_h2[...] = _ffn_norm(x1, w)

        s_up = stage.at[CONV_W - 1]
        sample_hist = _StateHistory(sp_v, sc_v, stage, us_v, cs_v, s_up)

        def sample_token_mixer():
            pool_cp.wait()
            conv_cp.wait()
            xs_cp.wait()
            _mixer(xs_v[...], w, sample_hist, s_mbuf, _full_windows)
            x1 = _mixer_residual(xs_v[...], _mixer_out(w, s_mbuf), w)
            ys_v[...] = x1
            s_h2[...] = _ffn_norm(x1, w)
            for cp in pool_out_cps:
                cp.start()
            for j in range(CONV_W - 2):
                sc_v[:, j, :] = sc_v[:, j + 1, :]
            sc_v[:, CONV_W - 2, :] = cs_v[...]
            conv_out_cp.start()

        def first_tile_token_mixer():
            wait_tile(xbuf.at[slot], in_sem.at[slot])
            prefetch_next_tile()
            start_sequence_mixer_history()
            token_mixer((ubuf_first, cbuf_first, mbuf_first))

        _cast_weights([(w_in_hbm, w_in), (w_pool_hbm, w_pool), (wco_hbm, wco), (wo_hbm, wo)],
                      [(wup_hbm, wup), (wdown_hbm, wdown)],
                      [meta_token_mixer, sample_token_mixer, first_tile_token_mixer],
                      stage, cast_sem)
        _ffn_up(m_h2[...], w, meta_hist, s_actbuf.at[meta_rows])
        for g in range(n_groups):
            if g + ring_slots - 1 < n_groups:
                ffn_in_copy(g + ring_slots - 1).start()
            ffn_in_copy(g).wait()
            for k in range(CONV_W - 1):
                stage[k, g * group:(g + 1) * group, :] = sf_ring[g % ring_slots, :, k, :]

        _ffn_up(s_h2[...], w, sample_hist, s_actbuf)
        ys_v[...] = _ffn_residual(ys_v[...], _ffn_down(w, s_actbuf), w)
        ys_cp.start()

        for g in range(n_groups):
            if g >= ring_slots:
                ffn_out_copy(g - ring_slots).wait()
            rows = slice(g * group, (g + 1) * group)
            for k in range(1, CONV_W - 1):
                sf_ring[g % ring_slots, :, k - 1, :] = stage[k, rows, :]
            sf_ring[g % ring_slots, :, CONV_W - 2, :] = s_up[rows, :]
            ffn_out_copy(g).start()
        ys_cp.wait()

    @pl.when((n > 0) & (n < total))
    def _():
        wait_tile(xbuf.at[slot], in_sem.at[slot])

    @pl.when(n >= 3)
    def _():
        wait_tile(ybuf.at[1 - slot], out_sem.at[1 - slot])

    @pl.when((n % nt == 0) & (n > 0))
    def _():
        start_sequence_mixer_history()

    @pl.when(n % nt == 1)
    def _():
        _fill_tail(ftail, m_upbuf, CONV_HALO + N_META - (CONV_W - 1), CONV_W - 1)

    @pl.when((n > 0) & (n < total))
    def _():
        prefetch_next_tile()
        x1_old[...] = x1_new[...]
        channel_mixer_up((upbuf, actbuf))
        token_mixer((ubuf, cbuf, mbuf))
        channel_mixer_out((upbuf, actbuf), x1_old)
        send_previous_tile()

    @pl.when(n == total)
    def _():
        channel_mixer_up((upbuf_last, actbuf_last))
        channel_mixer_out((upbuf_last, actbuf_last), x1_new)
        send_previous_tile()

    @pl.when((n % nt == nt - 1) & (n < total))
    def _():
        for j in range(POOL_BUF):
            op_ref[j, pl.ds(n // nt, 1), :] = utail[(j + 1) * SUBLANES - 1:(j + 1) * SUBLANES, :]
        _read_tail(oc_ref.at[n // nt], ctail, CONV_W - 1)

    @pl.when((n % nt == 0) & (n >= 1))
    def _():
        _read_tail(of_ref.at[(n - 1) // nt], ftail, CONV_W - 1)

    @pl.when(n == total)
    def _():
        wait_tile(ybuf.at[slot], out_sem.at[slot])
        wait_tile(ybuf.at[1 - slot], out_sem.at[1 - slot])
        wait_tile(xbuf.at[slot], in_sem.at[slot])
        for cp in pool_out_cps:
            cp.wait()
        conv_out_cp.wait()
        for g in range(max(n_groups - ring_slots, 0), n_groups):
            ffn_out_copy(g).wait()


def _resident(shape, layer=0):
    index = (layer if shape[0] is None else 0,) + (0,) * (len(shape) - 1)
    return pl.BlockSpec(shape, lambda *_: index, pipeline_mode=pl.Buffered(1))


def _fused_call(x, xs, sp, sc, sf, meta, big_weights, pscale, w_conv, w_ffn_conv, gains, l, tm):
    nb, t, d = x.shape
    ns = xs.shape[0]
    seg = tm // SUBLANES
    assert t % tm == 0 and tm % SUBLANES == 0 and seg >= POOL_BUF
    assert xs.shape == (ns, 1, d) and ns == CAST_ROWS and ns % SF_GROUP == 0
    assert sf.shape[1:] == (ns, CONV_W - 1, STAGE_COLS) and CAST_SLOTS >= CONV_W
    assert sp.shape == (POOL_BUF, ns, POOL_WIDTH) and sc.shape[1:] == (ns, CONV_W - 1, CONV_WIDTH)
    nt = t // tm
    assert nt >= 2
    total = nb * nt
    hbm = pl.BlockSpec(memory_space=pl.ANY)
    rows_with_head = lambda depth: (depth + seg) * SUBLANES
    return pl.pallas_call(
        functools.partial(_fused_kernel, tm=tm, nt=nt, total=total, layer=l),
        grid=(total + 1,),
        in_specs=[hbm, hbm, hbm, hbm, hbm, _resident(meta.shape)]
                 + [hbm] * len(big_weights)
                 + [_resident(pscale.shape), hbm, hbm]
                 + [_resident(g.shape) for g in gains],
        out_specs=[hbm, hbm,
                   _resident((POOL_BUF, nb, POOL_WIDTH)),
                   _resident((None, nb, CONV_W - 1, CONV_WIDTH), l),
                   _resident((None, nb, CONV_W - 1, 2 * D_FF), l),
                   hbm, hbm, hbm],
        out_shape=[jax.ShapeDtypeStruct((nb, t, d), _F32),
                   jax.ShapeDtypeStruct((ns, 1, d), _F32),
                   jax.ShapeDtypeStruct((POOL_BUF, nb, POOL_WIDTH), _F32),
                   jax.ShapeDtypeStruct((1, nb, CONV_W - 1, CONV_WIDTH), _F32),
                   jax.ShapeDtypeStruct((1, nb, CONV_W - 1, 2 * D_FF), _F32),
                   jax.ShapeDtypeStruct(sp.shape, _F32),
                   jax.ShapeDtypeStruct(sc.shape, _F32),
                   jax.ShapeDtypeStruct(sf.shape, _F32)],
        scratch_shapes=[pltpu.VMEM((d, IN_TOTAL), _BF16),
                        pltpu.VMEM((N_POOL_GROUPS, POOL_GC, POOL_OUT_GC), _BF16),
                        pltpu.VMEM((CONV_WIDTH, d), _BF16),
                        pltpu.VMEM((d, d), _BF16),
                        pltpu.VMEM((d, 2 * D_FF), _BF16),
                        pltpu.VMEM((D_FF, d), _BF16),
                        pltpu.VMEM((CONV_W, CONV_WIDTH), _F32),
                        pltpu.VMEM((CONV_W, 2 * D_FF), _F32),
                        pltpu.VMEM((CAST_SLOTS, CAST_ROWS, STAGE_COLS), _F32),
                        pltpu.VMEM(sp.shape, _F32),
                        pltpu.VMEM(sc.shape[1:], _F32),
                        pltpu.VMEM((POOL_HALO + N_META, POOL_WIDTH), _F32),
                        pltpu.VMEM((CONV_HALO + N_META, CONV_WIDTH), _F32),
                        pltpu.VMEM((CONV_HALO + N_META, 2 * D_FF), _F32),
                        pltpu.VMEM((ns, d), _BF16),
                        pltpu.VMEM((ns, D_FF), _BF16),
                        pltpu.VMEM((ns, d), _BF16),
                        pltpu.VMEM((N_META, d), _BF16),
                        pltpu.VMEM((ns, d), _F32),
                        pltpu.VMEM((ns, d), _F32),
                        pltpu.VMEM((ns, POOL_WIDTH), _F32),
                        pltpu.VMEM((ns, CONV_WIDTH), _F32),
                        pltpu.VMEM((SF_SLOTS, SF_GROUP, CONV_W - 1, 2 * D_FF), _F32),
                        pltpu.VMEM((2, seg, SUBLANES, d), _F32),
                        pltpu.VMEM((2, seg, SUBLANES, d), _F32),
                        pltpu.SemaphoreType.DMA((2,)),
                        pltpu.SemaphoreType.DMA((2,)),
                        pltpu.SemaphoreType.DMA((CAST_SLOTS,)),
                        pltpu.SemaphoreType.DMA((_N_STATE_SEMS,)),
                        pltpu.SemaphoreType.DMA((2, SF_SLOTS)),
                        pltpu.VMEM((rows_with_head(POOL_BUF), POOL_WIDTH), _F32),
                        pltpu.VMEM((rows_with_head(CONV_W - 1), CONV_WIDTH), _F32),
                        pltpu.VMEM((rows_with_head(CONV_W - 1), 2 * D_FF), _F32),
                        pltpu.VMEM((POOL_BUF * SUBLANES, POOL_WIDTH), _F32),
                        pltpu.VMEM(((CONV_W - 1) * SUBLANES, CONV_WIDTH), _F32),
                        pltpu.VMEM(((CONV_W - 1) * SUBLANES, 2 * D_FF), _F32),
                        pltpu.VMEM((tm, d), _BF16),
                        pltpu.VMEM((tm, D_FF), _BF16),
                        pltpu.VMEM((tm, d), _BF16),
                        pltpu.VMEM((tm, d), _F32),
                        pltpu.VMEM((tm, d), _F32),
                        pltpu.VMEM((rows_with_head(POOL_BUF), POOL_WIDTH), _F32),
                        pltpu.VMEM((rows_with_head(CONV_W - 1), CONV_WIDTH), _F32),
                        pltpu.VMEM((tm, d), _BF16),
                        pltpu.VMEM((rows_with_head(CONV_W - 1), 2 * D_FF), _F32),
                        pltpu.VMEM((tm, D_FF), _BF16)],
        compiler_params=pltpu.CompilerParams(
            dimension_semantics=("arbitrary",), vmem_limit_bytes=VMEM_LIMIT_BYTES),
        name="hybrid_step",
    )(x, xs, sp, sc, sf, meta, *big_weights, pscale, w_conv, w_ffn_conv, *gains)


def kernel(x_prompt, x_sample, state_pool, state_conv, state_ffn, meta_tokens, w_in, w_pool,
           pool_scale, w_conv, w_conv_out, w_o, w_up, w_ffn_conv, w_down, g_pre_mix,
           g_post_mix, g_pre_ffn, g_post_ffn):
    depth = w_in.shape[0]
    assert depth == 1, "the prompt / sample streams are chained for a single layer only"
    assert x_sample.shape[1] == 1 and meta_tokens.shape[0] == N_META
    assert N_META >= max(POOL_WINDOWS)
    row = lambda a: a.reshape(1, -1)
    l = 0
    big = (w_in[l], w_pool[l], w_conv_out[l], w_o[l], w_up[l], w_down[l])
    gains = (row(g_pre_mix[l]), row(g_post_mix[l]), row(g_pre_ffn[l]), row(g_post_ffn[l]))

    rows_first = lambda s: jnp.transpose(s, (1, 0, 2))
    y_prompt, y_sample, pp, pc, pf, sp_new, sc_new, sf_new = _fused_call(
        x_prompt, x_sample, rows_first(state_pool[l]), state_conv, state_ffn,
        meta_tokens.astype(_F32), big, row(pool_scale[l]), rows_first(w_conv),
        rows_first(w_ffn_conv), gains, l, TM_PROMPT)

    return (y_prompt, y_sample, rows_first(pp)[None], pc, pf, rows_first(sp_new)[None],
            sc_new, sf_new)
```

```python
import functools

import jax
import jax.numpy as jnp
from jax import lax
from jax.experimental import pallas as pl
from jax.experimental.pallas import tpu as pltpu

D_MODEL = 1024
N_META = 16
POOL_WINDOWS = (2, 4, 8, 16)
N_POOL_GROUPS = len(POOL_WINDOWS)
POOL_WIDTH = D_MODEL // 2
POOL_GC = POOL_WIDTH // N_POOL_GROUPS
POOL_OUT_GC = D_MODEL // N_POOL_GROUPS
POOL_BUF = max(POOL_WINDOWS) - 1
CONV_WIDTH = D_MODEL // 2
CONV_W = 3
D_FF = 2816
IN_TOTAL = 2 * POOL_WIDTH + 2 * CONV_WIDTH + 2 * D_MODEL
EPS = 1e-6
LOG2_E = 1.4426950408889634
OFF_U, OFF_V, OFF_B, OFF_C, OFF_GA, OFF_GB = 0, 512, 1024, 1536, 2048, 3072

SUBLANES = 8
FF_CHUNK = 256
POOL_HALO = 16
CONV_HALO = 8
TM_PROMPT = 256
CAST_ROWS = 128
CAST_SLOTS = 3
SF_GROUP = 8
SF_SLOTS = 4
STAGE_COLS = 2 * D_FF
VMEM_LIMIT_BYTES = 58 * 1024 * 1024

_BF16 = jnp.bfloat16
_F32 = jnp.float32

(_SEM_POOL_IN, _SEM_CONV_IN, _SEM_X_SAMPLE, _SEM_Y_SAMPLE, _SEM_POOL_OUT, _SEM_CONV_OUT,
 _SEM_CONV_WEIGHTS, _N_STATE_SEMS) = range(8)


def _dot(a, b):
    return jnp.dot(a, b, preferred_element_type=_F32)


def _rms(x, g):
    ms = jnp.mean(x * x, axis=-1, keepdims=True)
    return x * lax.rsqrt(ms + EPS) * g


def _sigmoid(x):
    return 1.0 / (1.0 + jnp.exp2(x * (-LOG2_E)))


def _mixer(x, w, hist, mbuf, pool_div):
    (w_in, w_pool, pscale, wconv, wco, _, _, _, _, g1, _, _, _) = w
    h = _rms(x, g1[...]).astype(_BF16)

    hist.put_u(_dot(h, w_in[:, OFF_U:OFF_U + POOL_WIDTH]))
    v = _dot(h, w_in[:, OFF_V:OFF_V + CONV_WIDTH])
    cg = _dot(h, w_in[:, OFF_C:OFF_C + CONV_WIDTH])
    hist.put_c(cg * v)
    cv = wconv[CONV_W - 1:CONV_W, :] * hist.get_c(0)
    for k in range(1, CONV_W):
        cv = cv + wconv[CONV_W - 1 - k:CONV_W - k, :] * hist.get_c(k)
    bg = _dot(h, w_in[:, OFF_B:OFF_B + CONV_WIDTH])
    b_in = (bg * cv).astype(_BF16)

    for g, win in enumerate(POOL_WINDOWS):
        cs = slice(g * POOL_GC, (g + 1) * POOL_GC)
        os_ = slice(g * POOL_OUT_GC, (g + 1) * POOL_OUT_GC)
        ug = hist.get_u(0, cs)
        s = ug
        for k in range(1, win):
            s = s + hist.get_u(k, cs)
        pooled = (pool_div(s, win) - ug).astype(_BF16)
        a = _dot(pooled, w_pool[g]) * pscale[:, os_]
        ga = _dot(h, w_in[:, OFF_GA + g * POOL_OUT_GC:OFF_GA + (g + 1) * POOL_OUT_GC])
        gb = _dot(h, w_in[:, OFF_GB + g * POOL_OUT_GC:OFF_GB + (g + 1) * POOL_OUT_GC])
        b = _dot(b_in, wco[:, os_])
        mbuf[:, os_] = (_sigmoid(ga) * a + _sigmoid(gb) * b).astype(_BF16)


def _mixer_out(w, mbuf, rows=slice(None)):
    return _dot(mbuf[rows, :], w[5][...])


def _mixer_residual(x, mixed, w):
    return x + _rms(mixed, w[10][...])


def _ffn_norm(x1, w):
    return _rms(x1, w[11][...]).astype(_BF16)


def _ffn_up(h2, w, hist, actbuf):
    wup, wfc = w[6], w[7]
    for c in range(D_FF // FF_CHUNK):
        conv = []
        for col in (c * FF_CHUNK, D_FF + c * FF_CHUNK):
            cols = slice(col, col + FF_CHUNK)
            hist.put_up(_dot(h2, wup[:, cols]), cols)
            y = wfc[CONV_W - 1:CONV_W, cols] * hist.get_up(0, cols)
            for k in range(1, CONV_W):
                y = y + wfc[CONV_W - 1 - k:CONV_W - k, cols] * hist.get_up(k, cols)
            conv.append(y)
        gate, val = conv
        actbuf[:, c * FF_CHUNK:(c + 1) * FF_CHUNK] = (gate * _sigmoid(gate) * val).astype(_BF16)


def _ffn_down(w, actbuf, rows=slice(None)):
    return _dot(actbuf[rows, :], w[8][...])


def _ffn_residual(x1, down, w):
    return x1 + _rms(down, w[12][...])


def _full_windows(s, win):
    return s * (1.0 / win)


class _RowHistory:
    def __init__(self, ubuf, cbuf, upbuf, m):
        self.ubuf, self.cbuf, self.upbuf, self.m = ubuf, cbuf, upbuf, m

    def put_u(self, val):
        self.ubuf[POOL_HALO:POOL_HALO + self.m, :] = val

    def get_u(self, k, cs):
        return self.ubuf[POOL_HALO - k:POOL_HALO - k + self.m, cs]

    def put_c(self, val):
        self.cbuf[CONV_HALO:CONV_HALO + self.m, :] = val

    def get_c(self, k):
        return self.cbuf[CONV_HALO - k:CONV_HALO - k + self.m, :]

    def put_up(self, val, cols):
        self.upbuf[CONV_HALO:CONV_HALO + self.m, cols] = val

    def get_up(self, k, cols):
        return self.upbuf[CONV_HALO - k:CONV_HALO - k + self.m, cols]


class _SegmentHistory:
    def __init__(self, ubuf, cbuf, upbuf, utail, ctail, ftail, tm):
        self.ubuf, self.cbuf, self.upbuf, self.tm = ubuf, cbuf, upbuf, tm
        self.utail, self.ctail, self.ftail = utail, ctail, ftail

    def _put(self, buf, tail, val, cols, depth):
        tm = self.tm
        buf[depth * SUBLANES:depth * SUBLANES + tm, cols] = val
        sublane = lax.broadcasted_iota(jnp.int32, (SUBLANES, val.shape[1]), 0)
        for j in range(depth):
            rows = slice(j * SUBLANES, (j + 1) * SUBLANES)
            cur = val[tm - (depth - j) * SUBLANES:tm - (depth - j - 1) * SUBLANES, :]
            mixed = jnp.where(sublane == SUBLANES - 1, tail[rows, cols], cur)
            buf[rows, cols] = pltpu.roll(mixed, 1, axis=0)
            tail[rows, cols] = cur

    def _get(self, buf, k, cols, depth):
        start = (depth - k) * SUBLANES
        return buf[start:start + self.tm, cols]

    def put_u(self, val):
        self._put(self.ubuf, self.utail, val, slice(None), POOL_BUF)

    def get_u(self, k, cs):
        return self._get(self.ubuf, k, cs, POOL_BUF)

    def put_c(self, val):
        self._put(self.cbuf, self.ctail, val, slice(None), CONV_W - 1)

    def get_c(self, k):
        return self._get(self.cbuf, k, slice(None), CONV_W - 1)

    def put_up(self, val, cols):
        self._put(self.upbuf, self.ftail, val, cols, CONV_W - 1)

    def get_up(self, k, cols):
        return self._get(self.upbuf, k, cols, CONV_W - 1)


class _StateHistory:
    def __init__(self, sp_ref, sc_ref, sf_ref, u_ref, c_ref, f_ref):
        self.sp, self.sc, self.sf = sp_ref, sc_ref, sf_ref
        self.u, self.c, self.f = u_ref, c_ref, f_ref

    def put_u(self, val):
        self.u[...] = val

    def get_u(self, k, cs):
        return self.u[:, cs] if k == 0 else self.sp[POOL_BUF - k, :, cs]

    def put_c(self, val):
        self.c[...] = val

    def get_c(self, k):
        return self.c[...] if k == 0 else self.sc[:, CONV_W - 1 - k, :]

    def put_up(self, val, cols):
        self.f[:, cols] = val

    def get_up(self, k, cols):
        return self.f[:, cols] if k == 0 else self.sf[CONV_W - 1 - k, :, cols]


class _StackedHistory:
    def __init__(self, top, bottom, split):
        self.top, self.bottom, self.split = top, bottom, split

    def _both(self, get, *args):
        return jnp.concatenate([get(self.top)(*args), get(self.bottom)(*args)], axis=0)

    def put_u(self, val):
        self.top.put_u(val[:self.split])
        self.bottom.put_u(val[self.split:])

    def get_u(self, k, cs):
        return self._both(lambda h: h.get_u, k, cs)

    def put_c(self, val):
        self.top.put_c(val[:self.split])
        self.bottom.put_c(val[self.split:])

    def get_c(self, k):
        return self._both(lambda h: h.get_c, k)

    def put_up(self, val, cols):
        self.top.put_up(val[:self.split], cols)
        self.bottom.put_up(val[self.split:], cols)

    def get_up(self, k, cols):
        return self._both(lambda h: h.get_up, k, cols)


def _fill_tail(tail, state_ref, first_row, depth):
    for j in range(depth):
        r = first_row + j
        tail[j * SUBLANES:(j + 1) * SUBLANES, :] = jnp.broadcast_to(
            state_ref[r:r + 1, :], (SUBLANES, tail.shape[1]))


def _read_tail(out_ref, tail, depth):
    for j in range(depth):
        out_ref[j:j + 1, :] = tail[(j + 1) * SUBLANES - 1:(j + 1) * SUBLANES, :]


def _cast_batches(pairs):
    jobs = []
    for src, dst in pairs:
        if len(src.shape) == 3:
            assert src.shape[1] == CAST_ROWS
            jobs += [(src.at[g], dst.at[g], src.shape[2]) for g in range(src.shape[0])]
        else:
            assert src.shape[0] % CAST_ROWS == 0
            jobs += [(src.at[pl.ds(r, CAST_ROWS), :], dst.at[pl.ds(r, CAST_ROWS), :], src.shape[1])
                     for r in range(0, src.shape[0], CAST_ROWS)]
    batches, cur, used = [], [], 0
    for s, d, cols in jobs:
        if used + cols > STAGE_COLS:
            batches.append(cur)
            cur, used = [], 0
        cur.append((s, d, used, cols))
        used += cols
    batches.append(cur)
    return batches


def _cast_weights(first_pairs, later_pairs, between, stage, sem):
    slots = stage.shape[0]
    first = _cast_batches(first_pairs)
    batches = first + _cast_batches(later_pairs)
    hooks = {len(first) + 2 * i: fn for i, fn in enumerate(between)}
    assert not hooks or max(hooks) < len(batches)

    def copies(b):
        return [pltpu.make_async_copy(s, stage.at[b % slots, :, pl.ds(lane, cols)],
                                      sem.at[b % slots]) for s, _, lane, cols in batches[b]]

    for b in range(min(slots, len(batches))):
        for cp in copies(b):
            cp.start()
    for b in range(len(batches)):
        for cp in copies(b):
            cp.wait()
        for _, d, lane, cols in batches[b]:
            d[...] = stage[b % slots, :, lane:lane + cols].astype(_BF16)
        if b + slots < len(batches):
            for cp in copies(b + slots):
                cp.start()
        if b in hooks:
            hooks[b]()


def _fused_kernel(x_hbm, xs_hbm, sp_hbm, sc_hbm, sf_hbm, meta_ref,
                  w_in_hbm, w_pool_hbm, wco_hbm, wo_hbm, wup_hbm, wdown_hbm,
                  pscale, wconv_hbm, wfc_hbm, g1, g2, g3, g4,
                  y_hbm, ys_hbm, op_ref, oc_ref, of_ref, spo_hbm, sco_hbm, sfo_hbm,
                  w_in, w_pool, wco, wo, wup, wdown, wconv, wfc, stage, sp_v, sc_v,
                  m_ubuf, m_cbuf, m_upbuf,
                  s_mbuf, s_actbuf, s_h2, xs_v, ys_v, us_v, cs_v, sf_ring, xbuf, ybuf,
                  in_sem, out_sem, cast_sem, state_sem, ring_sem,
                  ubuf, cbuf, upbuf, utail, ctail, ftail, mbuf, actbuf, h2buf, x1_new, x1_old,
                  ubuf_first, cbuf_first, mbuf_first, upbuf_last, actbuf_last,
                  *, tm, nt, total, layer):
    w = (w_in, w_pool, pscale, wconv, wco, wo, wup, wfc, wdown, g1, g2, g3, g4)
    seg = tm // SUBLANES
    n = pl.program_id(0)
    slot = n % 2

    def x_copies(tile, sl):
        b, t0 = tile // nt, (tile % nt) * tm
        return [pltpu.make_async_copy(x_hbm.at[b, pl.ds(t0 + s * seg, seg), :],
                                      xbuf.at[sl, :, s, :], in_sem.at[sl]) for s in range(SUBLANES)]

    def y_copies(tile, sl):
        b, t0 = tile // nt, (tile % nt) * tm
        return [pltpu.make_async_copy(ybuf.at[sl, :, s, :],
                                      y_hbm.at[b, pl.ds(t0 + s * seg, seg), :], out_sem.at[sl])
                for s in range(SUBLANES)]

    ring_slots, group = sf_ring.shape[:2]
    n_sample = ys_v.shape[0]
    n_groups = n_sample // group
    pool_out_cps = [
        pltpu.make_async_copy(sp_hbm.at[pl.ds(1, POOL_BUF - 1)], spo_hbm.at[pl.ds(0, POOL_BUF - 1)],
                              state_sem.at[_SEM_POOL_OUT]),
        pltpu.make_async_copy(us_v, spo_hbm.at[POOL_BUF - 1], state_sem.at[_SEM_POOL_OUT])]
    conv_out_cp = pltpu.make_async_copy(sc_v, sco_hbm.at[layer], state_sem.at[_SEM_CONV_OUT])

    def ffn_in_copy(g):
        return pltpu.make_async_copy(sf_hbm.at[layer, pl.ds(g * group, group)],
                                     sf_ring.at[g % ring_slots], ring_sem.at[0, g % ring_slots])

    def ffn_out_copy(g):
        return pltpu.make_async_copy(sf_ring.at[g % ring_slots],
                                     sfo_hbm.at[layer, pl.ds(g * group, group)],
                                     ring_sem.at[1, g % ring_slots])

    def wait_tile(buf, sem):
        pltpu.make_async_copy(buf, buf, sem).wait()

    def start_sequence_mixer_history():
        _fill_tail(utail, m_ubuf, POOL_HALO + N_META - POOL_BUF, POOL_BUF)
        _fill_tail(ctail, m_cbuf, CONV_HALO + N_META - (CONV_W - 1), CONV_W - 1)

    def token_mixer(work):
        ub, cb, mb = work
        hist = _SegmentHistory(ub, cb, None, utail, ctail, None, tm)
        x = xbuf[slot].reshape(tm, D_MODEL)
        _mixer(x, w, hist, mb, _full_windows)
        x1 = _mixer_residual(x, _mixer_out(w, mb), w)
        x1_new[...] = x1
        h2buf[...] = _ffn_norm(x1, w)

    def channel_mixer_up(work):
        upb, ab = work
        _ffn_up(h2buf[...], w, _SegmentHistory(None, None, upb, None, None, ftail, tm), ab)

    def channel_mixer_out(work, x1_ref):
        y = _ffn_residual(x1_ref[...], _ffn_down(w, work[1]), w)
        ybuf[1 - slot] = y.reshape(seg, SUBLANES, D_MODEL)

    def prefetch_next_tile():
        for cp in x_copies(jnp.minimum(n + 1, total - 1), 1 - slot):
            cp.start()

    def send_previous_tile():
        for cp in y_copies(n - 1, 1 - slot):
            cp.start()

    @pl.when(n == 0)
    def _():
        for cp in x_copies(n, slot):
            cp.start()
        pool_cp = pltpu.make_async_copy(sp_hbm, sp_v, state_sem.at[_SEM_POOL_IN])
        conv_cp = pltpu.make_async_copy(sc_hbm.at[layer], sc_v, state_sem.at[_SEM_CONV_IN])
        wconv_cp = pltpu.make_async_copy(wconv_hbm.at[:, layer, :], wconv,
                                         state_sem.at[_SEM_CONV_WEIGHTS])
        wfc_cp = pltpu.make_async_copy(wfc_hbm.at[:, layer, :], wfc,
                                       state_sem.at[_SEM_CONV_WEIGHTS])
        wconv_cp.start()
        wfc_cp.start()
        xs_cp = pltpu.make_async_copy(xs_hbm.at[:, 0, :], xs_v.at[pl.ds(0, n_sample)],
                                      state_sem.at[_SEM_X_SAMPLE])
        ys_cp = pltpu.make_async_copy(ys_v, ys_hbm.at[:, 0, :], state_sem.at[_SEM_Y_SAMPLE])
        pool_cp.start()
        conv_cp.start()
        xs_cp.start()
        for g in range(min(ring_slots - 1, n_groups)):
            ffn_in_copy(g).start()

        xs_v[n_sample:n_sample + N_META, :] = meta_ref[...]
        m_ubuf[0:POOL_HALO, :] = jnp.zeros((POOL_HALO, POOL_WIDTH), _F32)
        m_cbuf[0:CONV_HALO, :] = jnp.zeros((CONV_HALO, CONV_WIDTH), _F32)
        m_upbuf[0:CONV_HALO, :] = jnp.zeros((CONV_HALO, 2 * D_FF), _F32)
        row = lax.broadcasted_iota(jnp.int32, (n_sample + N_META, POOL_GC), 0)

        def short_windows(s, win):
            count = jnp.where(row < n_sample, win, jnp.minimum(win, row - n_sample + 1))
            return s / count.astype(_F32)

        s_up = stage.at[CONV_W - 1]
        short_hist = _StackedHistory(_StateHistory(sp_v, sc_v, stage, us_v, cs_v, s_up),
                                     _RowHistory(m_ubuf, m_cbuf, m_upbuf, N_META), n_sample)

        def short_token_mixer():
            wconv_cp.wait()
            wfc_cp.wait()
            pool_cp.wait()
            conv_cp.wait()
            xs_cp.wait()
            _mixer(xs_v[...], w, short_hist, s_mbuf, short_windows)
            x1 = _mixer_residual(xs_v[...], _mixer_out(w, s_mbuf), w)
            ys_v[...] = x1[:n_sample]
            s_h2[...] = _ffn_norm(x1, w)
            for cp in pool_out_cps:
                cp.start()
            for j in range(CONV_W - 2):
                sc_v[:, j, :] = sc_v[:, j + 1, :]
            sc_v[:, CONV_W - 2, :] = cs_v[...]
            conv_out_cp.start()

        def first_tile_token_mixer():
            wait_tile(xbuf.at[slot], in_sem.at[slot])
            prefetch_next_tile()
            start_sequence_mixer_history()
            token_mixer((ubuf_first, cbuf_first, mbuf_first))

        _cast_weights([(w_in_hbm, w_in), (w_pool_hbm, w_pool), (wco_hbm, wco), (wo_hbm, wo)],
                      [(wup_hbm, wup), (wdown_hbm, wdown)],
                      [short_token_mixer, first_tile_token_mixer],
                      stage, cast_sem)
        for g in range(n_groups):
            if g + ring_slots - 1 < n_groups:
                ffn_in_copy(g + ring_slots - 1).start()
            ffn_in_copy(g).wait()
            for k in range(CONV_W - 1):
                stage[k, g * group:(g + 1) * group, :] = sf_ring[g % ring_slots, :, k, :]

        _ffn_up(s_h2[...], w, short_hist, s_actbuf)
        ys_v[...] = _ffn_residual(ys_v[...], _ffn_down(w, s_actbuf, slice(0, n_sample)), w)
        ys_cp.start()

        for g in range(n_groups):
            if g >= ring_slots:
                ffn_out_copy(g - ring_slots).wait()
            rows = slice(g * group, (g + 1) * group)
            for k in range(1, CONV_W - 1):
                sf_ring[g % ring_slots, :, k - 1, :] = stage[k, rows, :]
            sf_ring[g % ring_slots, :, CONV_W - 2, :] = s_up[rows, :]
            ffn_out_copy(g).start()
        ys_cp.wait()

    @pl.when((n > 0) & (n < total))
    def _():
        wait_tile(xbuf.at[slot], in_sem.at[slot])

    @pl.when(n >= 3)
    def _():
        wait_tile(ybuf.at[1 - slot], out_sem.at[1 - slot])

    @pl.when((n % nt == 0) & (n > 0))
    def _():
        start_sequence_mixer_history()

    @pl.when(n % nt == 1)
    def _():
        _fill_tail(ftail, m_upbuf, CONV_HALO + N_META - (CONV_W - 1), CONV_W - 1)

    @pl.when((n > 0) & (n < total))
    def _():
        prefetch_next_tile()
        x1_old[...] = x1_new[...]
        channel_mixer_up((upbuf, actbuf))
        token_mixer((ubuf, cbuf, mbuf))
        channel_mixer_out((upbuf, actbuf), x1_old)
        send_previous_tile()

    @pl.when(n == total)
    def _():
        channel_mixer_up((upbuf_last, actbuf_last))
        channel_mixer_out((upbuf_last, actbuf_last), x1_new)
        send_previous_tile()

    @pl.when((n % nt == nt - 1) & (n < total))
    def _():
        for j in range(POOL_BUF):
            op_ref[j, pl.ds(n // nt, 1), :] = utail[(j + 1) * SUBLANES - 1:(j + 1) * SUBLANES, :]
        _read_tail(oc_ref.at[n // nt], ctail, CONV_W - 1)

    @pl.when((n % nt == 0) & (n >= 1))
    def _():
        _read_tail(of_ref.at[(n - 1) // nt], ftail, CONV_W - 1)

    @pl.when(n == total)
    def _():
        wait_tile(ybuf.at[slot], out_sem.at[slot])
        wait_tile(ybuf.at[1 - slot], out_sem.at[1 - slot])
        wait_tile(xbuf.at[slot], in_sem.at[slot])
        for cp in pool_out_cps:
            cp.wait()
        conv_out_cp.wait()
        for g in range(max(n_groups - ring_slots, 0), n_groups):
            ffn_out_copy(g).wait()


def _resident(shape, layer=0):
    index = (layer if shape[0] is None else 0,) + (0,) * (len(shape) - 1)
    return pl.BlockSpec(shape, lambda *_: index, pipeline_mode=pl.Buffered(1))


def _fused_call(x, xs, sp, sc, sf, meta, big_weights, pscale, w_conv, w_ffn_conv, gains, l, tm):
    nb, t, d = x.shape
    ns = xs.shape[0]
    seg = tm // SUBLANES
    assert t % tm == 0 and tm % SUBLANES == 0 and seg >= POOL_BUF
    assert xs.shape == (ns, 1, d) and ns == CAST_ROWS and ns % SF_GROUP == 0
    assert sf.shape[1:] == (ns, CONV_W - 1, STAGE_COLS) and CAST_SLOTS >= CONV_W
    assert sp.shape == (POOL_BUF, ns, POOL_WIDTH) and sc.shape[1:] == (ns, CONV_W - 1, CONV_WIDTH)
    nt = t // tm
    assert nt >= 2
    total = nb * nt
    hbm = pl.BlockSpec(memory_space=pl.ANY)
    rows_with_head = lambda depth: (depth + seg) * SUBLANES
    return pl.pallas_call(
        functools.partial(_fused_kernel, tm=tm, nt=nt, total=total, layer=l),
        grid=(total + 1,),
        in_specs=[hbm, hbm, hbm, hbm, hbm, _resident(meta.shape)]
                 + [hbm] * len(big_weights)
                 + [_resident(pscale.shape), hbm, hbm]
                 + [_resident(g.shape) for g in gains],
        out_specs=[hbm, hbm,
                   _resident((POOL_BUF, nb, POOL_WIDTH)),
                   _resident((None, nb, CONV_W - 1, CONV_WIDTH), l),
                   _resident((None, nb, CONV_W - 1, 2 * D_FF), l),
                   hbm, hbm, hbm],
        out_shape=[jax.ShapeDtypeStruct((nb, t, d), _F32),
                   jax.ShapeDtypeStruct((ns, 1, d), _F32),
                   jax.ShapeDtypeStruct((POOL_BUF, nb, POOL_WIDTH), _F32),
                   jax.ShapeDtypeStruct((1, nb, CONV_W - 1, CONV_WIDTH), _F32),
                   jax.ShapeDtypeStruct((1, nb, CONV_W - 1, 2 * D_FF), _F32),
                   jax.ShapeDtypeStruct(sp.shape, _F32),
                   jax.ShapeDtypeStruct(sc.shape, _F32),
                   jax.ShapeDtypeStruct(sf.shape, _F32)],
        scratch_shapes=[pltpu.VMEM((d, IN_TOTAL), _BF16),
                        pltpu.VMEM((N_POOL_GROUPS, POOL_GC, POOL_OUT_GC), _BF16),
                        pltpu.VMEM((CONV_WIDTH, d), _BF16),
                        pltpu.VMEM((d, d), _BF16),
                        pltpu.VMEM((d, 2 * D_FF), _BF16),
                        pltpu.VMEM((D_FF, d), _BF16),
                        pltpu.VMEM((CONV_W, CONV_WIDTH), _F32),
                        pltpu.VMEM((CONV_W, 2 * D_FF), _F32),
                        pltpu.VMEM((CAST_SLOTS, CAST_ROWS, STAGE_COLS), _F32),
                        pltpu.VMEM(sp.shape, _F32),
                        pltpu.VMEM(sc.shape[1:], _F32),
                        pltpu.VMEM((POOL_HALO + N_META, POOL_WIDTH), _F32),
                        pltpu.VMEM((CONV_HALO + N_META, CONV_WIDTH), _F32),
                        pltpu.VMEM((CONV_HALO + N_META, 2 * D_FF), _F32),
                        pltpu.VMEM((ns + N_META, d), _BF16),
                        pltpu.VMEM((ns + N_META, D_FF), _BF16),
                        pltpu.VMEM((ns + N_META, d), _BF16),
                        pltpu.VMEM((ns + N_META, d), _F32),
                        pltpu.VMEM((ns, d), _F32),
                        pltpu.VMEM((ns, POOL_WIDTH), _F32),
                        pltpu.VMEM((ns, CONV_WIDTH), _F32),
                        pltpu.VMEM((SF_SLOTS, SF_GROUP, CONV_W - 1, 2 * D_FF), _F32),
                        pltpu.VMEM((2, seg, SUBLANES, d), _F32),
                        pltpu.VMEM((2, seg, SUBLANES, d), _F32),
                        pltpu.SemaphoreType.DMA((2,)),
                        pltpu.SemaphoreType.DMA((2,)),
                        pltpu.SemaphoreType.DMA((CAST_SLOTS,)),
                        pltpu.SemaphoreType.DMA((_N_STATE_SEMS,)),
                        pltpu.SemaphoreType.DMA((2, SF_SLOTS)),
                        pltpu.VMEM((rows_with_head(POOL_BUF), POOL_WIDTH), _F32),
                        pltpu.VMEM((rows_with_head(CONV_W - 1), CONV_WIDTH), _F32),
                        pltpu.VMEM((rows_with_head(CONV_W - 1), 2 * D_FF), _F32),
                        pltpu.VMEM((POOL_BUF * SUBLANES, POOL_WIDTH), _F32),
                        pltpu.VMEM(((CONV_W - 1) * SUBLANES, CONV_WIDTH), _F32),
                        pltpu.VMEM(((CONV_W - 1) * SUBLANES, 2 * D_FF), _F32),
                        pltpu.VMEM((tm, d), _BF16),
                        pltpu.VMEM((tm, D_FF), _BF16),
                        pltpu.VMEM((tm, d), _BF16),
                        pltpu.VMEM((tm, d), _F32),
                        pltpu.VMEM((tm, d), _F32),
                        pltpu.VMEM((rows_with_head(POOL_BUF), POOL_WIDTH), _F32),
                        pltpu.VMEM((rows_with_head(CONV_W - 1), CONV_WIDTH), _F32),
                        pltpu.VMEM((tm, d), _BF16),
                        pltpu.VMEM((rows_with_head(CONV_W - 1), 2 * D_FF), _F32),
                        pltpu.VMEM((tm, D_FF), _BF16)],
        compiler_params=pltpu.CompilerParams(
            dimension_semantics=("arbitrary",), vmem_limit_bytes=VMEM_LIMIT_BYTES),
        name="hybrid_step",
    )(x, xs, sp, sc, sf, meta, *big_weights, pscale, w_conv, w_ffn_conv, *gains)


def kernel(x_prompt, x_sample, state_pool, state_conv, state_ffn, meta_tokens, w_in, w_pool,
           pool_scale, w_conv, w_conv_out, w_o, w_up, w_ffn_conv, w_down, g_pre_mix,
           g_post_mix, g_pre_ffn, g_post_ffn):
    depth = w_in.shape[0]
    assert depth == 1, "the prompt / sample streams are chained for a single layer only"
    assert x_sample.shape[1] == 1 and meta_tokens.shape[0] == N_META
    assert N_META >= max(POOL_WINDOWS)
    row = lambda a: a.reshape(1, -1)
    l = 0
    big = (w_in[l], w_pool[l], w_conv_out[l], w_o[l], w_up[l], w_down[l])
    gains = (row(g_pre_mix[l]), row(g_post_mix[l]), row(g_pre_ffn[l]), row(g_post_ffn[l]))

    rows_first = lambda s: jnp.transpose(s, (1, 0, 2))
    y_prompt, y_sample, pp, pc, pf, sp_new, sc_new, sf_new = _fused_call(
        x_prompt, x_sample, rows_first(state_pool[l]), state_conv, state_ffn,
        meta_tokens.astype(_F32), big, row(pool_scale[l]), rows_first(w_conv),
        rows_first(w_ffn_conv), gains, l, TM_PROMPT)

    return (y_prompt, y_sample, rows_first(pp)[None], pc, pf, rows_first(sp_new)[None],
            sc_new, sf_new)
```

```python
import functools

import jax
import jax.numpy as jnp
from jax import lax
from jax.experimental import pallas as pl
from jax.experimental.pallas import tpu as pltpu

D_MODEL = 1024
N_META = 16
POOL_WINDOWS = (2, 4, 8, 16)
N_POOL_GROUPS = len(POOL_WINDOWS)
POOL_WIDTH = D_MODEL // 2
POOL_GC = POOL_WIDTH // N_POOL_GROUPS
POOL_OUT_GC = D_MODEL // N_POOL_GROUPS
POOL_BUF = max(POOL_WINDOWS) - 1
CONV_WIDTH = D_MODEL // 2
CONV_W = 3
D_FF = 2816
IN_TOTAL = 2 * POOL_WIDTH + 2 * CONV_WIDTH + 2 * D_MODEL
EPS = 1e-6
LOG2_E = 1.4426950408889634
OFF_U, OFF_V, OFF_B, OFF_C, OFF_GA, OFF_GB = 0, 512, 1024, 1536, 2048, 3072

SUBLANES = 8
FF_CHUNK = 256
POOL_HALO = 16
CONV_HALO = 8
TM_PROMPT = 256
CAST_ROWS = 128
CAST_SLOTS = 3
SF_GROUP = 8
SF_SLOTS = 4
STAGE_COLS = 2 * D_FF
VMEM_LIMIT_BYTES = 58 * 1024 * 1024

_BF16 = jnp.bfloat16
_F32 = jnp.float32

(_SEM_POOL_IN, _SEM_CONV_IN, _SEM_X_SAMPLE, _SEM_Y_SAMPLE, _SEM_POOL_OUT, _SEM_CONV_OUT,
 _SEM_CONV_WEIGHTS, _N_STATE_SEMS) = range(8)


def _dot(a, b):
    return jnp.dot(a, b, preferred_element_type=_F32)


def _rms(x, g):
    ms = jnp.mean(x * x, axis=-1, keepdims=True)
    return x * lax.rsqrt(ms + EPS) * g


def _sigmoid(x):
    return 1.0 / (1.0 + jnp.exp2(x * (-LOG2_E)))


def _mixer(x, w, hist, mbuf, pool_div):
    for _ in _mixer_halves(x, w, hist, mbuf, pool_div):
        pass


def _mixer_halves(x, w, hist, mbuf, pool_div):
    (w_in, w_pool, pscale, wconv, wco, _, _, _, _, g1, _, _, _) = w
    h = _rms(x, g1[...]).astype(_BF16)

    hist.put_u(_dot(h, w_in[:, OFF_U:OFF_U + POOL_WIDTH]))
    v = _dot(h, w_in[:, OFF_V:OFF_V + CONV_WIDTH])
    cg = _dot(h, w_in[:, OFF_C:OFF_C + CONV_WIDTH])
    hist.put_c(cg * v)
    cv = wconv[CONV_W - 1:CONV_W, :] * hist.get_c(0)
    for k in range(1, CONV_W):
        cv = cv + wconv[CONV_W - 1 - k:CONV_W - k, :] * hist.get_c(k)
    bg = _dot(h, w_in[:, OFF_B:OFF_B + CONV_WIDTH])
    b_in = (bg * cv).astype(_BF16)

    for g, win in enumerate(POOL_WINDOWS):
        cs = slice(g * POOL_GC, (g + 1) * POOL_GC)
        os_ = slice(g * POOL_OUT_GC, (g + 1) * POOL_OUT_GC)
        ug = hist.get_u(0, cs)
        s = ug
        for k in range(1, win):
            s = s + hist.get_u(k, cs)
        pooled = (pool_div(s, win) - ug).astype(_BF16)
        a = _dot(pooled, w_pool[g]) * pscale[:, os_]
        ga = _dot(h, w_in[:, OFF_GA + g * POOL_OUT_GC:OFF_GA + (g + 1) * POOL_OUT_GC])
        gb = _dot(h, w_in[:, OFF_GB + g * POOL_OUT_GC:OFF_GB + (g + 1) * POOL_OUT_GC])
        b = _dot(b_in, wco[:, os_])
        mbuf[:, os_] = (_sigmoid(ga) * a + _sigmoid(gb) * b).astype(_BF16)
        if g + 1 == N_POOL_GROUPS // 2:
            yield


def _mixer_out(w, mbuf, rows=slice(None)):
    return _dot(mbuf[rows, :], w[5][...])


def _mixer_residual(x, mixed, w):
    return x + _rms(mixed, w[10][...])


def _ffn_norm(x1, w):
    return _rms(x1, w[11][...]).astype(_BF16)


def _ffn_up(h2, w, hist, actbuf):
    wup, wfc = w[6], w[7]
    for c in range(D_FF // FF_CHUNK):
        conv = []
        for col in (c * FF_CHUNK, D_FF + c * FF_CHUNK):
            cols = slice(col, col + FF_CHUNK)
            hist.put_up(_dot(h2, wup[:, cols]), cols)
            y = wfc[CONV_W - 1:CONV_W, cols] * hist.get_up(0, cols)
            for k in range(1, CONV_W):
                y = y + wfc[CONV_W - 1 - k:CONV_W - k, cols] * hist.get_up(k, cols)
            conv.append(y)
        gate, val = conv
        actbuf[:, c * FF_CHUNK:(c + 1) * FF_CHUNK] = (gate * _sigmoid(gate) * val).astype(_BF16)


def _ffn_down(w, actbuf, rows=slice(None)):
    return _dot(actbuf[rows, :], w[8][...])


def _ffn_residual(x1, down, w):
    return x1 + _rms(down, w[12][...])


def _full_windows(s, win):
    return s * (1.0 / win)


class _RowHistory:
    def __init__(self, ubuf, cbuf, upbuf, m):
        self.ubuf, self.cbuf, self.upbuf, self.m = ubuf, cbuf, upbuf, m

    def put_u(self, val):
        self.ubuf[POOL_HALO:POOL_HALO + self.m, :] = val

    def get_u(self, k, cs):
        return self.ubuf[POOL_HALO - k:POOL_HALO - k + self.m, cs]

    def put_c(self, val):
        self.cbuf[CONV_HALO:CONV_HALO + self.m, :] = val

    def get_c(self, k):
        return self.cbuf[CONV_HALO - k:CONV_HALO - k + self.m, :]

    def put_up(self, val, cols):
        self.upbuf[CONV_HALO:CONV_HALO + self.m, cols] = val

    def get_up(self, k, cols):
        return self.upbuf[CONV_HALO - k:CONV_HALO - k + self.m, cols]


class _SegmentHistory:
    def __init__(self, ubuf, cbuf, upbuf, utail, ctail, ftail, tm):
        self.ubuf, self.cbuf, self.upbuf, self.tm = ubuf, cbuf, upbuf, tm
        self.utail, self.ctail, self.ftail = utail, ctail, ftail

    def _put(self, buf, tail, val, cols, depth):
        tm = self.tm
        buf[depth * SUBLANES:depth * SUBLANES + tm, cols] = val
        sublane = lax.broadcasted_iota(jnp.int32, (SUBLANES, val.shape[1]), 0)
        for j in range(depth):
            rows = slice(j * SUBLANES, (j + 1) * SUBLANES)
            cur = val[tm - (depth - j) * SUBLANES:tm - (depth - j - 1) * SUBLANES, :]
            mixed = jnp.where(sublane == SUBLANES - 1, tail[rows, cols], cur)
            buf[rows, cols] = pltpu.roll(mixed, 1, axis=0)
            tail[rows, cols] = cur

    def _get(self, buf, k, cols, depth):
        start = (depth - k) * SUBLANES
        return buf[start:start + self.tm, cols]

    def put_u(self, val):
        self._put(self.ubuf, self.utail, val, slice(None), POOL_BUF)

    def get_u(self, k, cs):
        return self._get(self.ubuf, k, cs, POOL_BUF)

    def put_c(self, val):
        self._put(self.cbuf, self.ctail, val, slice(None), CONV_W - 1)

    def get_c(self, k):
        return self._get(self.cbuf, k, slice(None), CONV_W - 1)

    def put_up(self, val, cols):
        self._put(self.upbuf, self.ftail, val, cols, CONV_W - 1)

    def get_up(self, k, cols):
        return self._get(self.upbuf, k, cols, CONV_W - 1)


class _StateHistory:
    def __init__(self, sp_ref, sc_ref, sf_ref, u_ref, c_ref, f_ref):
        self.sp, self.sc, self.sf = sp_ref, sc_ref, sf_ref
        self.u, self.c, self.f = u_ref, c_ref, f_ref

    def put_u(self, val):
        self.u[...] = val

    def get_u(self, k, cs):
        return self.u[:, cs] if k == 0 else self.sp[POOL_BUF - k, :, cs]

    def put_c(self, val):
        self.c[...] = val

    def get_c(self, k):
        return self.c[...] if k == 0 else self.sc[:, CONV_W - 1 - k, :]

    def put_up(self, val, cols):
        self.f[:, cols] = val

    def get_up(self, k, cols):
        return self.f[:, cols] if k == 0 else self.sf[CONV_W - 1 - k, :, cols]


class _StackedHistory:
    def __init__(self, top, bottom, split):
        self.top, self.bottom, self.split = top, bottom, split

    def _both(self, get, *args):
        return jnp.concatenate([get(self.top)(*args), get(self.bottom)(*args)], axis=0)

    def put_u(self, val):
        self.top.put_u(val[:self.split])
        self.bottom.put_u(val[self.split:])

    def get_u(self, k, cs):
        return self._both(lambda h: h.get_u, k, cs)

    def put_c(self, val):
        self.top.put_c(val[:self.split])
        self.bottom.put_c(val[self.split:])

    def get_c(self, k):
        return self._both(lambda h: h.get_c, k)

    def put_up(self, val, cols):
        self.top.put_up(val[:self.split], cols)
        self.bottom.put_up(val[self.split:], cols)

    def get_up(self, k, cols):
        return self._both(lambda h: h.get_up, k, cols)


def _fill_tail(tail, state_ref, first_row, depth):
    for j in range(depth):
        r = first_row + j
        tail[j * SUBLANES:(j + 1) * SUBLANES, :] = jnp.broadcast_to(
            state_ref[r:r + 1, :], (SUBLANES, tail.shape[1]))


def _read_tail(out_ref, tail, depth):
    for j in range(depth):
        out_ref[j:j + 1, :] = tail[(j + 1) * SUBLANES - 1:(j + 1) * SUBLANES, :]


def _cast_batches(pairs):
    jobs = []
    for src, dst in pairs:
        if len(src.shape) == 3:
            assert src.shape[1] == CAST_ROWS
            jobs += [(src.at[g], dst.at[g], src.shape[2]) for g in range(src.shape[0])]
        else:
            assert src.shape[0] % CAST_ROWS == 0
            jobs += [(src.at[pl.ds(r, CAST_ROWS), :], dst.at[pl.ds(r, CAST_ROWS), :], src.shape[1])
                     for r in range(0, src.shape[0], CAST_ROWS)]
    batches, cur, used = [], [], 0
    for s, d, cols in jobs:
        if used + cols > STAGE_COLS:
            batches.append(cur)
            cur, used = [], 0
        cur.append((s, d, used, cols))
        used += cols
    batches.append(cur)
    return batches


def _cast_weights(first_pairs, later_pairs, between, stage, sem):
    slots = stage.shape[0]
    first = _cast_batches(first_pairs)
    batches = first + _cast_batches(later_pairs)
    hooks = {len(first) + 2 * i: fn for i, fn in enumerate(between)}
    assert not hooks or max(hooks) < len(batches)

    def copies(b):
        return [pltpu.make_async_copy(s, stage.at[b % slots, :, pl.ds(lane, cols)],
                                      sem.at[b % slots]) for s, _, lane, cols in batches[b]]

    for b in range(min(slots, len(batches))):
        for cp in copies(b):
            cp.start()
    for b in range(len(batches)):
        for cp in copies(b):
            cp.wait()
        for _, d, lane, cols in batches[b]:
            d[...] = stage[b % slots, :, lane:lane + cols].astype(_BF16)
        if b + slots < len(batches):
            for cp in copies(b + slots):
                cp.start()
        if b in hooks:
            hooks[b]()


def _fused_kernel(x_hbm, xs_hbm, sp_hbm, sc_hbm, sf_hbm, meta_ref,
                  w_in_hbm, w_pool_hbm, wco_hbm, wo_hbm, wup_hbm, wdown_hbm,
                  pscale, wconv_hbm, wfc_hbm, g1, g2, g3, g4,
                  y_hbm, ys_hbm, op_ref, oc_ref, of_ref, spo_hbm, sco_hbm, sfo_hbm,
                  w_in, w_pool, wco, wo, wup, wdown, wconv, wfc, stage, sp_v, sc_v,
                  m_ubuf, m_cbuf, m_upbuf,
                  s_mbuf, s_actbuf, s_h2, xs_v, ys_v, us_v, cs_v, sf_ring, xbuf, ybuf,
                  in_sem, out_sem, cast_sem, state_sem, ring_sem,
                  ubuf, cbuf, upbuf, utail, ctail, ftail, mbuf, actbuf, h2buf, x1_new, x1_old,
                  ubuf_first, cbuf_first, mbuf_first, upbuf_last, actbuf_last,
                  *, tm, nt, total, layer):
    w = (w_in, w_pool, pscale, wconv, wco, wo, wup, wfc, wdown, g1, g2, g3, g4)
    seg = tm // SUBLANES
    n = pl.program_id(0)
    slot = n % 2

    def x_copies(tile, sl):
        b, t0 = tile // nt, (tile % nt) * tm
        return [pltpu.make_async_copy(x_hbm.at[b, pl.ds(t0 + s * seg, seg), :],
                                      xbuf.at[sl, :, s, :], in_sem.at[sl]) for s in range(SUBLANES)]

    def y_copies(tile, sl):
        b, t0 = tile // nt, (tile % nt) * tm
        return [pltpu.make_async_copy(ybuf.at[sl, :, s, :],
                                      y_hbm.at[b, pl.ds(t0 + s * seg, seg), :], out_sem.at[sl])
                for s in range(SUBLANES)]

    ring_slots, group = sf_ring.shape[:2]
    n_sample = ys_v.shape[0]
    n_groups = n_sample // group
    pool_out_cps = [
        pltpu.make_async_copy(sp_hbm.at[pl.ds(1, POOL_BUF - 1)], spo_hbm.at[pl.ds(0, POOL_BUF - 1)],
                              state_sem.at[_SEM_POOL_OUT]),
        pltpu.make_async_copy(us_v, spo_hbm.at[POOL_BUF - 1], state_sem.at[_SEM_POOL_OUT])]
    conv_out_cp = pltpu.make_async_copy(sc_v, sco_hbm.at[layer], state_sem.at[_SEM_CONV_OUT])

    def ffn_in_copy(g):
        return pltpu.make_async_copy(sf_hbm.at[layer, pl.ds(g * group, group)],
                                     sf_ring.at[g % ring_slots], ring_sem.at[0, g % ring_slots])

    def ffn_out_copy(g):
        return pltpu.make_async_copy(sf_ring.at[g % ring_slots],
                                     sfo_hbm.at[layer, pl.ds(g * group, group)],
                                     ring_sem.at[1, g % ring_slots])

    def wait_tile(buf, sem):
        pltpu.make_async_copy(buf, buf, sem).wait()

    def start_sequence_mixer_history():
        _fill_tail(utail, m_ubuf, POOL_HALO + N_META - POOL_BUF, POOL_BUF)
        _fill_tail(ctail, m_cbuf, CONV_HALO + N_META - (CONV_W - 1), CONV_W - 1)

    def token_mixer_halves(work):
        ub, cb, mb = work
        hist = _SegmentHistory(ub, cb, None, utail, ctail, None, tm)
        x = xbuf[slot].reshape(tm, D_MODEL)
        halves = _mixer_halves(x, w, hist, mb, _full_windows)
        next(halves)
        yield
        for _ in halves:
            pass
        x1 = _mixer_residual(x, _mixer_out(w, mb), w)
        x1_new[...] = x1
        h2buf[...] = _ffn_norm(x1, w)

    def token_mixer(work):
        for _ in token_mixer_halves(work):
            pass

    def channel_mixer_up(work):
        upb, ab = work
        _ffn_up(h2buf[...], w, _SegmentHistory(None, None, upb, None, None, ftail, tm), ab)

    def channel_mixer_out(work, x1_ref):
        y = _ffn_residual(x1_ref[...], _ffn_down(w, work[1]), w)
        ybuf[1 - slot] = y.reshape(seg, SUBLANES, D_MODEL)

    def prefetch_next_tile():
        for cp in x_copies(jnp.minimum(n + 1, total - 1), 1 - slot):
            cp.start()

    def send_previous_tile():
        for cp in y_copies(n - 1, 1 - slot):
            cp.start()

    @pl.when(n == 0)
    def _():
        for cp in x_copies(n, slot):
            cp.start()
        pool_cp = pltpu.make_async_copy(sp_hbm, sp_v, state_sem.at[_SEM_POOL_IN])
        conv_cp = pltpu.make_async_copy(sc_hbm.at[layer], sc_v, state_sem.at[_SEM_CONV_IN])
        wconv_cp = pltpu.make_async_copy(wconv_hbm.at[:, layer, :], wconv,
                                         state_sem.at[_SEM_CONV_WEIGHTS])
        wfc_cp = pltpu.make_async_copy(wfc_hbm.at[:, layer, :], wfc,
                                       state_sem.at[_SEM_CONV_WEIGHTS])
        wconv_cp.start()
        wfc_cp.start()
        xs_cp = pltpu.make_async_copy(xs_hbm.at[:, 0, :], xs_v.at[pl.ds(0, n_sample)],
                                      state_sem.at[_SEM_X_SAMPLE])
        ys_cp = pltpu.make_async_copy(ys_v, ys_hbm.at[:, 0, :], state_sem.at[_SEM_Y_SAMPLE])
        pool_cp.start()
        conv_cp.start()
        xs_cp.start()
        for g in range(min(ring_slots - 1, n_groups)):
            ffn_in_copy(g).start()

        xs_v[n_sample:n_sample + N_META, :] = meta_ref[...]
        m_ubuf[0:POOL_HALO, :] = jnp.zeros((POOL_HALO, POOL_WIDTH), _F32)
        m_cbuf[0:CONV_HALO, :] = jnp.zeros((CONV_HALO, CONV_WIDTH), _F32)
        m_upbuf[0:CONV_HALO, :] = jnp.zeros((CONV_HALO, 2 * D_FF), _F32)
        row = lax.broadcasted_iota(jnp.int32, (n_sample + N_META, POOL_GC), 0)

        def short_windows(s, win):
            count = jnp.where(row < n_sample, win, jnp.minimum(win, row - n_sample + 1))
            return s / count.astype(_F32)

        s_up = stage.at[CONV_W - 1]
        short_hist = _StackedHistory(_StateHistory(sp_v, sc_v, stage, us_v, cs_v, s_up),
                                     _RowHistory(m_ubuf, m_cbuf, m_upbuf, N_META), n_sample)

        def short_token_mixer():
            wconv_cp.wait()
            wfc_cp.wait()
            pool_cp.wait()
            conv_cp.wait()
            xs_cp.wait()
            _mixer(xs_v[...], w, short_hist, s_mbuf, short_windows)
            x1 = _mixer_residual(xs_v[...], _mixer_out(w, s_mbuf), w)
            ys_v[...] = x1[:n_sample]
            s_h2[...] = _ffn_norm(x1, w)
            for cp in pool_out_cps:
                cp.start()
            for j in range(CONV_W - 2):
                sc_v[:, j, :] = sc_v[:, j + 1, :]
            sc_v[:, CONV_W - 2, :] = cs_v[...]
            conv_out_cp.start()

        first_tile = []

        def first_tile_first_half():
            wait_tile(xbuf.at[slot], in_sem.at[slot])
            prefetch_next_tile()
            start_sequence_mixer_history()
            first_tile.append(token_mixer_halves((ubuf_first, cbuf_first, mbuf_first)))
            next(first_tile[0])

        def first_tile_second_half():
            for _ in first_tile[0]:
                pass

        _cast_weights([(w_in_hbm, w_in), (w_pool_hbm, w_pool), (wco_hbm, wco), (wo_hbm, wo)],
                      [(wup_hbm, wup), (wdown_hbm, wdown)],
                      [short_token_mixer, first_tile_first_half, first_tile_second_half],
                      stage, cast_sem)
        for g in range(n_groups):
            if g + ring_slots - 1 < n_groups:
                ffn_in_copy(g + ring_slots - 1).start()
            ffn_in_copy(g).wait()
            for k in range(CONV_W - 1):
                stage[k, g * group:(g + 1) * group, :] = sf_ring[g % ring_slots, :, k, :]

        _ffn_up(s_h2[...], w, short_hist, s_actbuf)
        ys_v[...] = _ffn_residual(ys_v[...], _ffn_down(w, s_actbuf, slice(0, n_sample)), w)
        ys_cp.start()

        for g in range(n_groups):
            if g >= ring_slots:
                ffn_out_copy(g - ring_slots).wait()
            rows = slice(g * group, (g + 1) * group)
            for k in range(1, CONV_W - 1):
                sf_ring[g % ring_slots, :, k - 1, :] = stage[k, rows, :]
            sf_ring[g % ring_slots, :, CONV_W - 2, :] = s_up[rows, :]
            ffn_out_copy(g).start()
        ys_cp.wait()

    @pl.when((n > 0) & (n < total))
    def _():
        wait_tile(xbuf.at[slot], in_sem.at[slot])

    @pl.when(n >= 3)
    def _():
        wait_tile(ybuf.at[1 - slot], out_sem.at[1 - slot])

    @pl.when((n % nt == 0) & (n > 0))
    def _():
        start_sequence_mixer_history()

    @pl.when(n % nt == 1)
    def _():
        _fill_tail(ftail, m_upbuf, CONV_HALO + N_META - (CONV_W - 1), CONV_W - 1)

    @pl.when((n > 0) & (n < total))
    def _():
        prefetch_next_tile()
        x1_old[...] = x1_new[...]
        channel_mixer_up((upbuf, actbuf))
        token_mixer((ubuf, cbuf, mbuf))
        channel_mixer_out((upbuf, actbuf), x1_old)
        send_previous_tile()

    @pl.when(n == total)
    def _():
        channel_mixer_up((upbuf_last, actbuf_last))
        channel_mixer_out((upbuf_last, actbuf_last), x1_new)
        send_previous_tile()

    @pl.when((n % nt == nt - 1) & (n < total))
    def _():
        for j in range(POOL_BUF):
            op_ref[j, pl.ds(n // nt, 1), :] = utail[(j + 1) * SUBLANES - 1:(j + 1) * SUBLANES, :]
        _read_tail(oc_ref.at[n // nt], ctail, CONV_W - 1)

    @pl.when((n % nt == 0) & (n >= 1))
    def _():
        _read_tail(of_ref.at[(n - 1) // nt], ftail, CONV_W - 1)

    @pl.when(n == total)
    def _():
        wait_tile(ybuf.at[slot], out_sem.at[slot])
        wait_tile(ybuf.at[1 - slot], out_sem.at[1 - slot])
        wait_tile(xbuf.at[slot], in_sem.at[slot])
        for cp in pool_out_cps:
            cp.wait()
        conv_out_cp.wait()
        for g in range(max(n_groups - ring_slots, 0), n_groups):
            ffn_out_copy(g).wait()


def _resident(shape, layer=0):
    index = (layer if shape[0] is None else 0,) + (0,) * (len(shape) - 1)
    return pl.BlockSpec(shape, lambda *_: index, pipeline_mode=pl.Buffered(1))


def _fused_call(x, xs, sp, sc, sf, meta, big_weights, pscale, w_conv, w_ffn_conv, gains, l, tm):
    nb, t, d = x.shape
    ns = xs.shape[0]
    seg = tm // SUBLANES
    assert t % tm == 0 and tm % SUBLANES == 0 and seg >= POOL_BUF
    assert xs.shape == (ns, 1, d) and ns == CAST_ROWS and ns % SF_GROUP == 0
    assert sf.shape[1:] == (ns, CONV_W - 1, STAGE_COLS) and CAST_SLOTS >= CONV_W
    assert sp.shape == (POOL_BUF, ns, POOL_WIDTH) and sc.shape[1:] == (ns, CONV_W - 1, CONV_WIDTH)
    nt = t // tm
    assert nt >= 2
    total = nb * nt
    hbm = pl.BlockSpec(memory_space=pl.ANY)
    rows_with_head = lambda depth: (depth + seg) * SUBLANES
    return pl.pallas_call(
        functools.partial(_fused_kernel, tm=tm, nt=nt, total=total, layer=l),
        grid=(total + 1,),
        in_specs=[hbm, hbm, hbm, hbm, hbm, _resident(meta.shape)]
                 + [hbm] * len(big_weights)
                 + [_resident(pscale.shape), hbm, hbm]
                 + [_resident(g.shape) for g in gains],
        out_specs=[hbm, hbm,
                   _resident((POOL_BUF, nb, POOL_WIDTH)),
                   _resident((None, nb, CONV_W - 1, CONV_WIDTH), l),
                   _resident((None, nb, CONV_W - 1, 2 * D_FF), l),
                   hbm, hbm, hbm],
        out_shape=[jax.ShapeDtypeStruct((nb, t, d), _F32),
                   jax.ShapeDtypeStruct((ns, 1, d), _F32),
                   jax.ShapeDtypeStruct((POOL_BUF, nb, POOL_WIDTH), _F32),
                   jax.ShapeDtypeStruct((1, nb, CONV_W - 1, CONV_WIDTH), _F32),
                   jax.ShapeDtypeStruct((1, nb, CONV_W - 1, 2 * D_FF), _F32),
                   jax.ShapeDtypeStruct(sp.shape, _F32),
                   jax.ShapeDtypeStruct(sc.shape, _F32),
                   jax.ShapeDtypeStruct(sf.shape, _F32)],
        scratch_shapes=[pltpu.VMEM((d, IN_TOTAL), _BF16),
                        pltpu.VMEM((N_POOL_GROUPS, POOL_GC, POOL_OUT_GC), _BF16),
                        pltpu.VMEM((CONV_WIDTH, d), _BF16),
                        pltpu.VMEM((d, d), _BF16),
                        pltpu.VMEM((d, 2 * D_FF), _BF16),
                        pltpu.VMEM((D_FF, d), _BF16),
                        pltpu.VMEM((CONV_W, CONV_WIDTH), _F32),
                        pltpu.VMEM((CONV_W, 2 * D_FF), _F32),
                        pltpu.VMEM((CAST_SLOTS, CAST_ROWS, STAGE_COLS), _F32),
                        pltpu.VMEM(sp.shape, _F32),
                        pltpu.VMEM(sc.shape[1:], _F32),
                        pltpu.VMEM((POOL_HALO + N_META, POOL_WIDTH), _F32),
                        pltpu.VMEM((CONV_HALO + N_META, CONV_WIDTH), _F32),
                        pltpu.VMEM((CONV_HALO + N_META, 2 * D_FF), _F32),
                        pltpu.VMEM((ns + N_META, d), _BF16),
                        pltpu.VMEM((ns + N_META, D_FF), _BF16),
                        pltpu.VMEM((ns + N_META, d), _BF16),
                        pltpu.VMEM((ns + N_META, d), _F32),
                        pltpu.VMEM((ns, d), _F32),
                        pltpu.VMEM((ns, POOL_WIDTH), _F32),
                        pltpu.VMEM((ns, CONV_WIDTH), _F32),
                        pltpu.VMEM((SF_SLOTS, SF_GROUP, CONV_W - 1, 2 * D_FF), _F32),
                        pltpu.VMEM((2, seg, SUBLANES, d), _F32),
                        pltpu.VMEM((2, seg, SUBLANES, d), _F32),
                        pltpu.SemaphoreType.DMA((2,)),
                        pltpu.SemaphoreType.DMA((2,)),
                        pltpu.SemaphoreType.DMA((CAST_SLOTS,)),
                        pltpu.SemaphoreType.DMA((_N_STATE_SEMS,)),
                        pltpu.SemaphoreType.DMA((2, SF_SLOTS)),
                        pltpu.VMEM((rows_with_head(POOL_BUF), POOL_WIDTH), _F32),
                        pltpu.VMEM((rows_with_head(CONV_W - 1), CONV_WIDTH), _F32),
                        pltpu.VMEM((rows_with_head(CONV_W - 1), 2 * D_FF), _F32),
                        pltpu.VMEM((POOL_BUF * SUBLANES, POOL_WIDTH), _F32),
                        pltpu.VMEM(((CONV_W - 1) * SUBLANES, CONV_WIDTH), _F32),
                        pltpu.VMEM(((CONV_W - 1) * SUBLANES, 2 * D_FF), _F32),
                        pltpu.VMEM((tm, d), _BF16),
                        pltpu.VMEM((tm, D_FF), _BF16),
                        pltpu.VMEM((tm, d), _BF16),
                        pltpu.VMEM((tm, d), _F32),
                        pltpu.VMEM((tm, d), _F32),
                        pltpu.VMEM((rows_with_head(POOL_BUF), POOL_WIDTH), _F32),
                        pltpu.VMEM((rows_with_head(CONV_W - 1), CONV_WIDTH), _F32),
                        pltpu.VMEM((tm, d), _BF16),
                        pltpu.VMEM((rows_with_head(CONV_W - 1), 2 * D_FF), _F32),
                        pltpu.VMEM((tm, D_FF), _BF16)],
        compiler_params=pltpu.CompilerParams(
            dimension_semantics=("arbitrary",), vmem_limit_bytes=VMEM_LIMIT_BYTES),
        name="hybrid_step",
    )(x, xs, sp, sc, sf, meta, *big_weights, pscale, w_conv, w_ffn_conv, *gains)


def kernel(x_prompt, x_sample, state_pool, state_conv, state_ffn, meta_tokens, w_in, w_pool,
           pool_scale, w_conv, w_conv_out, w_o, w_up, w_ffn_conv, w_down, g_pre_mix,
           g_post_mix, g_pre_ffn, g_post_ffn):
    depth = w_in.shape[0]
    assert depth == 1, "the prompt / sample streams are chained for a single layer only"
    assert x_sample.shape[1] == 1 and meta_tokens.shape[0] == N_META
    assert N_META >= max(POOL_WINDOWS)
    row = lambda a: a.reshape(1, -1)
    l = 0
    big = (w_in[l], w_pool[l], w_conv_out[l], w_o[l], w_up[l], w_down[l])
    gains = (row(g_pre_mix[l]), row(g_post_mix[l]), row(g_pre_ffn[l]), row(g_post_ffn[l]))

    rows_first = lambda s: jnp.transpose(s, (1, 0, 2))
    y_prompt, y_sample, pp, pc, pf, sp_new, sc_new, sf_new = _fused_call(
        x_prompt, x_sample, rows_first(state_pool[l]), state_conv, state_ffn,
        meta_tokens.astype(_F32), big, row(pool_scale[l]), rows_first(w_conv),
        rows_first(w_ffn_conv), gains, l, TM_PROMPT)

    return (y_prompt, y_sample, rows_first(pp)[None], pc, pf, rows_first(sp_new)[None],
            sc_new, sf_new)
```

```python
import functools

import jax
import jax.numpy as jnp
from jax import lax
from jax.experimental import pallas as pl
from jax.experimental.pallas import tpu as pltpu

D_MODEL = 1024
N_META = 16
POOL_WINDOWS = (2, 4, 8, 16)
N_POOL_GROUPS = len(POOL_WINDOWS)
POOL_WIDTH = D_MODEL // 2
POOL_GC = POOL_WIDTH // N_POOL_GROUPS
POOL_OUT_GC = D_MODEL // N_POOL_GROUPS
POOL_BUF = max(POOL_WINDOWS) - 1
CONV_WIDTH = D_MODEL // 2
CONV_W = 3
D_FF = 2816
IN_TOTAL = 2 * POOL_WIDTH + 2 * CONV_WIDTH + 2 * D_MODEL
EPS = 1e-6
LOG2_E = 1.4426950408889634
OFF_U, OFF_V, OFF_B, OFF_C, OFF_GA, OFF_GB = 0, 512, 1024, 1536, 2048, 3072

SUBLANES = 8
FF_CHUNK = 256
POOL_HALO = 16
CONV_HALO = 8
TM_PROMPT = 256
CAST_ROWS = 128
CAST_SLOTS = 3
SF_GROUP = 8
SF_SLOTS = 4
STAGE_COLS = 2 * D_FF
VMEM_LIMIT_BYTES = 58 * 1024 * 1024

_BF16 = jnp.bfloat16
_F32 = jnp.float32

(_SEM_POOL_IN, _SEM_CONV_IN, _SEM_X_SAMPLE, _SEM_Y_SAMPLE, _SEM_POOL_OUT, _SEM_CONV_OUT,
 _SEM_CONV_WEIGHTS, _N_STATE_SEMS) = range(8)


def _dot(a, b):
    return jnp.dot(a, b, preferred_element_type=_F32)


def _rms(x, g):
    ms = jnp.mean(x * x, axis=-1, keepdims=True)
    return x * lax.rsqrt(ms + EPS) * g


def _sigmoid(x):
    return 1.0 / (1.0 + jnp.exp2(x * (-LOG2_E)))


def _mixer(x, w, hist, mbuf, pool_div):
    (w_in, w_pool, pscale, wconv, wco, _, _, _, _, g1, _, _, _) = w
    h = _rms(x, g1[...]).astype(_BF16)

    hist.put_u(_dot(h, w_in[:, OFF_U:OFF_U + POOL_WIDTH]))
    v = _dot(h, w_in[:, OFF_V:OFF_V + CONV_WIDTH])
    cg = _dot(h, w_in[:, OFF_C:OFF_C + CONV_WIDTH])
    hist.put_c(cg * v)
    cv = wconv[CONV_W - 1:CONV_W, :] * hist.get_c(0)
    for k in range(1, CONV_W):
        cv = cv + wconv[CONV_W - 1 - k:CONV_W - k, :] * hist.get_c(k)
    bg = _dot(h, w_in[:, OFF_B:OFF_B + CONV_WIDTH])
    b_in = (bg * cv).astype(_BF16)

    for g, win in enumerate(POOL_WINDOWS):
        cs = slice(g * POOL_GC, (g + 1) * POOL_GC)
        os_ = slice(g * POOL_OUT_GC, (g + 1) * POOL_OUT_GC)
        ug = hist.get_u(0, cs)
        s = ug
        for k in range(1, win):
            s = s + hist.get_u(k, cs)
        pooled = (pool_div(s, win) - ug).astype(_BF16)
        a = _dot(pooled, w_pool[g]) * pscale[:, os_]
        ga = _dot(h, w_in[:, OFF_GA + g * POOL_OUT_GC:OFF_GA + (g + 1) * POOL_OUT_GC])
        gb = _dot(h, w_in[:, OFF_GB + g * POOL_OUT_GC:OFF_GB + (g + 1) * POOL_OUT_GC])
        b = _dot(b_in, wco[:, os_])
        mbuf[:, os_] = (_sigmoid(ga) * a + _sigmoid(gb) * b).astype(_BF16)


def _mixer_out(w, mbuf, rows=slice(None)):
    return _dot(mbuf[rows, :], w[5][...])


def _mixer_residual(x, mixed, w):
    return x + _rms(mixed, w[10][...])


def _ffn_norm(x1, w):
    return _rms(x1, w[11][...]).astype(_BF16)


def _ffn_up(h2, w, hist, actbuf):
    wup, wfc = w[6], w[7]
    for c in range(D_FF // FF_CHUNK):
        conv = []
        for col in (c * FF_CHUNK, D_FF + c * FF_CHUNK):
            cols = slice(col, col + FF_CHUNK)
            hist.put_up(_dot(h2, wup[:, cols]), cols)
            y = wfc[CONV_W - 1:CONV_W, cols] * hist.get_up(0, cols)
            for k in range(1, CONV_W):
                y = y + wfc[CONV_W - 1 - k:CONV_W - k, cols] * hist.get_up(k, cols)
            conv.append(y)
        gate, val = conv
        actbuf[:, c * FF_CHUNK:(c + 1) * FF_CHUNK] = (gate * _sigmoid(gate) * val).astype(_BF16)


def _ffn_down(w, actbuf, rows=slice(None)):
    return _dot(actbuf[rows, :], w[8][...])


def _ffn_residual(x1, down, w):
    return x1 + _rms(down, w[12][...])


def _full_windows(s, win):
    return s * (1.0 / win)


class _RowHistory:
    def __init__(self, ubuf, cbuf, upbuf, m):
        self.ubuf, self.cbuf, self.upbuf, self.m = ubuf, cbuf, upbuf, m

    def put_u(self, val):
        self.ubuf[POOL_HALO:POOL_HALO + self.m, :] = val

    def get_u(self, k, cs):
        return self.ubuf[POOL_HALO - k:POOL_HALO - k + self.m, cs]

    def put_c(self, val):
        self.cbuf[CONV_HALO:CONV_HALO + self.m, :] = val

    def get_c(self, k):
        return self.cbuf[CONV_HALO - k:CONV_HALO - k + self.m, :]

    def put_up(self, val, cols):
        self.upbuf[CONV_HALO:CONV_HALO + self.m, cols] = val

    def get_up(self, k, cols):
        return self.upbuf[CONV_HALO - k:CONV_HALO - k + self.m, cols]


class _SegmentHistory:
    def __init__(self, ubuf, cbuf, upbuf, utail, ctail, ftail, tm):
        self.ubuf, self.cbuf, self.upbuf, self.tm = ubuf, cbuf, upbuf, tm
        self.utail, self.ctail, self.ftail = utail, ctail, ftail

    def _put(self, buf, tail, val, cols, depth):
        tm = self.tm
        buf[depth * SUBLANES:depth * SUBLANES + tm, cols] = val
        sublane = lax.broadcasted_iota(jnp.int32, (SUBLANES, val.shape[1]), 0)
        for j in range(depth):
            rows = slice(j * SUBLANES, (j + 1) * SUBLANES)
            cur = val[tm - (depth - j) * SUBLANES:tm - (depth - j - 1) * SUBLANES, :]
            mixed = jnp.where(sublane == SUBLANES - 1, tail[rows, cols], cur)
            buf[rows, cols] = pltpu.roll(mixed, 1, axis=0)
            tail[rows, cols] = cur

    def _get(self, buf, k, cols, depth):
        start = (depth - k) * SUBLANES
        return buf[start:start + self.tm, cols]

    def put_u(self, val):
        self._put(self.ubuf, self.utail, val, slice(None), POOL_BUF)

    def get_u(self, k, cs):
        return self._get(self.ubuf, k, cs, POOL_BUF)

    def put_c(self, val):
        self._put(self.cbuf, self.ctail, val, slice(None), CONV_W - 1)

    def get_c(self, k):
        return self._get(self.cbuf, k, slice(None), CONV_W - 1)

    def put_up(self, val, cols):
        self._put(self.upbuf, self.ftail, val, cols, CONV_W - 1)

    def get_up(self, k, cols):
        return self._get(self.upbuf, k, cols, CONV_W - 1)


class _StateHistory:
    def __init__(self, sp_ref, sc_ref, sf_ref, u_ref, c_ref, f_ref):
        self.sp, self.sc, self.sf = sp_ref, sc_ref, sf_ref
        self.u, self.c, self.f = u_ref, c_ref, f_ref

    def put_u(self, val):
        self.u[...] = val

    def get_u(self, k, cs):
        return self.u[:, cs] if k == 0 else self.sp[POOL_BUF - k, :, cs]

    def put_c(self, val):
        self.c[...] = val

    def get_c(self, k):
        return self.c[...] if k == 0 else self.sc[:, CONV_W - 1 - k, :]

    def put_up(self, val, cols):
        self.f[:, cols] = val

    def get_up(self, k, cols):
        return self.f[:, cols] if k == 0 else self.sf[CONV_W - 1 - k, :, cols]


class _StackedHistory:
    def __init__(self, top, bottom, split):
        self.top, self.bottom, self.split = top, bottom, split

    def _both(self, get, *args):
        return jnp.concatenate([get(self.top)(*args), get(self.bottom)(*args)], axis=0)

    def put_u(self, val):
        self.top.put_u(val[:self.split])
        self.bottom.put_u(val[self.split:])

    def get_u(self, k, cs):
        return self._both(lambda h: h.get_u, k, cs)

    def put_c(self, val):
        self.top.put_c(val[:self.split])
        self.bottom.put_c(val[self.split:])

    def get_c(self, k):
        return self._both(lambda h: h.get_c, k)

    def put_up(self, val, cols):
        self.top.put_up(val[:self.split], cols)
        self.bottom.put_up(val[self.split:], cols)

    def get_up(self, k, cols):
        return self._both(lambda h: h.get_up, k, cols)


def _fill_tail(tail, state_ref, first_row, depth):
    for j in range(depth):
        r = first_row + j
        tail[j * SUBLANES:(j + 1) * SUBLANES, :] = jnp.broadcast_to(
            state_ref[r:r + 1, :], (SUBLANES, tail.shape[1]))


def _read_tail(out_ref, tail, depth):
    for j in range(depth):
        out_ref[j:j + 1, :] = tail[(j + 1) * SUBLANES - 1:(j + 1) * SUBLANES, :]


def _cast_batches(pairs):
    jobs = []
    for src, dst in pairs:
        if len(src.shape) == 3:
            assert src.shape[1] == CAST_ROWS
            jobs += [(src.at[g], dst.at[g], src.shape[2]) for g in range(src.shape[0])]
        else:
            assert src.shape[0] % CAST_ROWS == 0
            jobs += [(src.at[pl.ds(r, CAST_ROWS), :], dst.at[pl.ds(r, CAST_ROWS), :], src.shape[1])
                     for r in range(0, src.shape[0], CAST_ROWS)]
    batches, cur, used = [], [], 0
    for s, d, cols in jobs:
        if used + cols > STAGE_COLS:
            batches.append(cur)
            cur, used = [], 0
        cur.append((s, d, used, cols))
        used += cols
    batches.append(cur)
    return batches


def _cast_weights(first_pairs, later_pairs, between, stage, sem):
    slots = stage.shape[0]
    first = _cast_batches(first_pairs)
    batches = first + _cast_batches(later_pairs)
    hooks = {len(first) + 2 * i: fn for i, fn in enumerate(between)}
    assert not hooks or max(hooks) < len(batches)

    def copies(b):
        return [pltpu.make_async_copy(s, stage.at[b % slots, :, pl.ds(lane, cols)],
                                      sem.at[b % slots]) for s, _, lane, cols in batches[b]]

    for b in range(min(slots, len(batches))):
        for cp in copies(b):
            cp.start()
    for b in range(len(batches)):
        for cp in copies(b):
            cp.wait()
        for _, d, lane, cols in batches[b]:
            d[...] = stage[b % slots, :, lane:lane + cols].astype(_BF16)
        if b + slots < len(batches):
            for cp in copies(b + slots):
                cp.start()
        if b in hooks:
            hooks[b]()


def _fused_kernel(x_hbm, xs_hbm, sp_hbm, sc_hbm, sf_hbm, meta_ref,
                  w_in_hbm, w_pool_hbm, wco_hbm, wo_hbm, wup_hbm, wdown_hbm,
                  pscale, wconv_hbm, wfc_hbm, g1, g2, g3, g4,
                  y_hbm, ys_hbm, op_ref, oc_ref, of_ref, spo_hbm, sco_hbm, sfo_hbm,
                  w_in, w_pool, wco, wo, wup, wdown, wconv, wfc, stage, sp_v, sc_v,
                  m_ubuf, m_cbuf, m_upbuf,
                  s_mbuf, s_actbuf, s_h2, xs_v, ys_v, us_v, cs_v, sf_ring, xbuf, ybuf,
                  in_sem, out_sem, cast_sem, state_sem, ring_sem,
                  ubuf, cbuf, upbuf, utail, ctail, ftail, mbuf, actbuf, h2buf, x1_new, x1_old,
                  ubuf_first, cbuf_first, mbuf_first, upbuf_last, actbuf_last,
                  *, tm, nt, total, layer):
    w = (w_in, w_pool, pscale, wconv, wco, wo, wup, wfc, wdown, g1, g2, g3, g4)
    seg = tm // SUBLANES
    n, slot = 0, 0

    def x_copies(tile, sl):
        b, t0 = tile // nt, (tile % nt) * tm
        return [pltpu.make_async_copy(x_hbm.at[b, pl.ds(t0 + s * seg, seg), :],
                                      xbuf.at[sl, :, s, :], in_sem.at[sl]) for s in range(SUBLANES)]

    def y_copies(tile, sl):
        b, t0 = tile // nt, (tile % nt) * tm
        return [pltpu.make_async_copy(ybuf.at[sl, :, s, :],
                                      y_hbm.at[b, pl.ds(t0 + s * seg, seg), :], out_sem.at[sl])
                for s in range(SUBLANES)]

    ring_slots, group = sf_ring.shape[:2]
    n_sample = ys_v.shape[0]
    n_groups = n_sample // group
    pool_out_cps = [
        pltpu.make_async_copy(sp_hbm.at[pl.ds(1, POOL_BUF - 1)], spo_hbm.at[pl.ds(0, POOL_BUF - 1)],
                              state_sem.at[_SEM_POOL_OUT]),
        pltpu.make_async_copy(us_v, spo_hbm.at[POOL_BUF - 1], state_sem.at[_SEM_POOL_OUT])]
    conv_out_cp = pltpu.make_async_copy(sc_v, sco_hbm.at[layer], state_sem.at[_SEM_CONV_OUT])

    def ffn_in_copy(g):
        return pltpu.make_async_copy(sf_hbm.at[layer, pl.ds(g * group, group)],
                                     sf_ring.at[g % ring_slots], ring_sem.at[0, g % ring_slots])

    def ffn_out_copy(g):
        return pltpu.make_async_copy(sf_ring.at[g % ring_slots],
                                     sfo_hbm.at[layer, pl.ds(g * group, group)],
                                     ring_sem.at[1, g % ring_slots])

    def wait_tile(buf, sem):
        pltpu.make_async_copy(buf, buf, sem).wait()

    def start_sequence_mixer_history():
        _fill_tail(utail, m_ubuf, POOL_HALO + N_META - POOL_BUF, POOL_BUF)
        _fill_tail(ctail, m_cbuf, CONV_HALO + N_META - (CONV_W - 1), CONV_W - 1)

    def token_mixer(work):
        ub, cb, mb = work
        hist = _SegmentHistory(ub, cb, None, utail, ctail, None, tm)
        x = xbuf[slot].reshape(tm, D_MODEL)
        _mixer(x, w, hist, mb, _full_windows)
        x1 = _mixer_residual(x, _mixer_out(w, mb), w)
        x1_new[...] = x1
        h2buf[...] = _ffn_norm(x1, w)

    def channel_mixer_up(work):
        upb, ab = work
        _ffn_up(h2buf[...], w, _SegmentHistory(None, None, upb, None, None, ftail, tm), ab)

    def channel_mixer_out(work, x1_ref):
        y = _ffn_residual(x1_ref[...], _ffn_down(w, work[1]), w)
        ybuf[1 - slot] = y.reshape(seg, SUBLANES, D_MODEL)

    def prefetch_next_tile():
        for cp in x_copies(jnp.minimum(n + 1, total - 1), 1 - slot):
            cp.start()

    def send_previous_tile():
        for cp in y_copies(n - 1, 1 - slot):
            cp.start()

    def first_step():
        for cp in x_copies(n, slot):
            cp.start()
        pool_cp = pltpu.make_async_copy(sp_hbm, sp_v, state_sem.at[_SEM_POOL_IN])
        conv_cp = pltpu.make_async_copy(sc_hbm.at[layer], sc_v, state_sem.at[_SEM_CONV_IN])
        wconv_cp = pltpu.make_async_copy(wconv_hbm.at[:, layer, :], wconv,
                                         state_sem.at[_SEM_CONV_WEIGHTS])
        wfc_cp = pltpu.make_async_copy(wfc_hbm.at[:, layer, :], wfc,
                                       state_sem.at[_SEM_CONV_WEIGHTS])
        wconv_cp.start()
        wfc_cp.start()
        xs_cp = pltpu.make_async_copy(xs_hbm.at[:, 0, :], xs_v.at[pl.ds(0, n_sample)],
                                      state_sem.at[_SEM_X_SAMPLE])
        ys_cp = pltpu.make_async_copy(ys_v, ys_hbm.at[:, 0, :], state_sem.at[_SEM_Y_SAMPLE])
        pool_cp.start()
        conv_cp.start()
        xs_cp.start()
        for g in range(min(ring_slots - 1, n_groups)):
            ffn_in_copy(g).start()

        xs_v[n_sample:n_sample + N_META, :] = meta_ref[...]
        m_ubuf[0:POOL_HALO, :] = jnp.zeros((POOL_HALO, POOL_WIDTH), _F32)
        m_cbuf[0:CONV_HALO, :] = jnp.zeros((CONV_HALO, CONV_WIDTH), _F32)
        m_upbuf[0:CONV_HALO, :] = jnp.zeros((CONV_HALO, 2 * D_FF), _F32)
        row = lax.broadcasted_iota(jnp.int32, (n_sample + N_META, POOL_GC), 0)

        def short_windows(s, win):
            count = jnp.where(row < n_sample, win, jnp.minimum(win, row - n_sample + 1))
            return s / count.astype(_F32)

        s_up = stage.at[CONV_W - 1]
        short_hist = _StackedHistory(_StateHistory(sp_v, sc_v, stage, us_v, cs_v, s_up),
                                     _RowHistory(m_ubuf, m_cbuf, m_upbuf, N_META), n_sample)

        def short_token_mixer():
            wconv_cp.wait()
            wfc_cp.wait()
            pool_cp.wait()
            conv_cp.wait()
            xs_cp.wait()
            _mixer(xs_v[...], w, short_hist, s_mbuf, short_windows)
            x1 = _mixer_residual(xs_v[...], _mixer_out(w, s_mbuf), w)
            ys_v[...] = x1[:n_sample]
            s_h2[...] = _ffn_norm(x1, w)
            for cp in pool_out_cps:
                cp.start()
            for j in range(CONV_W - 2):
                sc_v[:, j, :] = sc_v[:, j + 1, :]
            sc_v[:, CONV_W - 2, :] = cs_v[...]
            conv_out_cp.start()

        def first_tile_token_mixer():
            wait_tile(xbuf.at[slot], in_sem.at[slot])
            prefetch_next_tile()
            start_sequence_mixer_history()
            token_mixer((ubuf_first, cbuf_first, mbuf_first))

        _cast_weights([(w_in_hbm, w_in), (w_pool_hbm, w_pool), (wco_hbm, wco), (wo_hbm, wo)],
                      [(wup_hbm, wup), (wdown_hbm, wdown)],
                      [short_token_mixer, first_tile_token_mixer],
                      stage, cast_sem)
        for g in range(n_groups):
            if g + ring_slots - 1 < n_groups:
                ffn_in_copy(g + ring_slots - 1).start()
            ffn_in_copy(g).wait()
            for k in range(CONV_W - 1):
                stage[k, g * group:(g + 1) * group, :] = sf_ring[g % ring_slots, :, k, :]

        _ffn_up(s_h2[...], w, short_hist, s_actbuf)
        ys_v[...] = _ffn_residual(ys_v[...], _ffn_down(w, s_actbuf, slice(0, n_sample)), w)
        ys_cp.start()

        for g in range(n_groups):
            if g >= ring_slots:
                ffn_out_copy(g - ring_slots).wait()
            rows = slice(g * group, (g + 1) * group)
            for k in range(1, CONV_W - 1):
                sf_ring[g % ring_slots, :, k - 1, :] = stage[k, rows, :]
            sf_ring[g % ring_slots, :, CONV_W - 2, :] = s_up[rows, :]
            ffn_out_copy(g).start()
        ys_cp.wait()

    def token_mixer_sequence_end():
        for j in range(POOL_BUF):
            op_ref[j, pl.ds(n // nt, 1), :] = utail[(j + 1) * SUBLANES - 1:(j + 1) * SUBLANES, :]
        _read_tail(oc_ref.at[layer, n // nt], ctail, CONV_W - 1)

    def channel_mixer_sequence_end():
        _read_tail(of_ref.at[layer, (n - 1) // nt], ftail, CONV_W - 1)

    def channel_mixer_sequence_start():
        _fill_tail(ftail, m_upbuf, CONV_HALO + N_META - (CONV_W - 1), CONV_W - 1)

    def steady_step(i, carry):
        nonlocal n, slot
        n, slot = i, i % 2
        wait_tile(xbuf.at[slot], in_sem.at[slot])
        pl.when(n >= 3)(lambda: wait_tile(ybuf.at[1 - slot], out_sem.at[1 - slot]))
        pl.when(n % nt == 0)(start_sequence_mixer_history)
        pl.when(n % nt == 1)(channel_mixer_sequence_start)

        prefetch_next_tile()
        x1_old[...] = x1_new[...]
        channel_mixer_up((upbuf, actbuf))
        token_mixer((ubuf, cbuf, mbuf))
        channel_mixer_out((upbuf, actbuf), x1_old)
        send_previous_tile()

        pl.when(n % nt == nt - 1)(token_mixer_sequence_end)
        pl.when(n % nt == 0)(channel_mixer_sequence_end)
        return carry

    n, slot = 0, 0
    first_step()
    if nt == 1:
        token_mixer_sequence_end()
    lax.fori_loop(1, total, steady_step, 0)

    n, slot = total, total % 2
    if total >= 3:
        wait_tile(ybuf.at[1 - slot], out_sem.at[1 - slot])
    if total % nt == 1:
        channel_mixer_sequence_start()
    channel_mixer_up((upbuf_last, actbuf_last))
    channel_mixer_out((upbuf_last, actbuf_last), x1_new)
    send_previous_tile()
    channel_mixer_sequence_end()
    wait_tile(ybuf.at[slot], out_sem.at[slot])
    wait_tile(ybuf.at[1 - slot], out_sem.at[1 - slot])
    wait_tile(xbuf.at[slot], in_sem.at[slot])
    for cp in pool_out_cps:
        cp.wait()
    conv_out_cp.wait()
    for g in range(max(n_groups - ring_slots, 0), n_groups):
        ffn_out_copy(g).wait()


def _fused_call(x, xs, sp, sc, sf, meta, big_weights, pscale, w_conv, w_ffn_conv, gains, l, tm):
    nb, t, d = x.shape
    ns = xs.shape[0]
    seg = tm // SUBLANES
    assert t % tm == 0 and tm % SUBLANES == 0 and seg >= POOL_BUF
    assert xs.shape == (ns, 1, d) and ns == CAST_ROWS and ns % SF_GROUP == 0
    assert sf.shape[1:] == (ns, CONV_W - 1, STAGE_COLS) and CAST_SLOTS >= CONV_W
    assert sp.shape == (POOL_BUF, ns, POOL_WIDTH) and sc.shape[1:] == (ns, CONV_W - 1, CONV_WIDTH)
    nt = t // tm
    assert nt >= 2
    total = nb * nt
    hbm = pl.BlockSpec(memory_space=pl.ANY)
    vmem = pl.BlockSpec(memory_space=pltpu.MemorySpace.VMEM)
    rows_with_head = lambda depth: (depth + seg) * SUBLANES
    return pl.pallas_call(
        functools.partial(_fused_kernel, tm=tm, nt=nt, total=total, layer=l),
        in_specs=[hbm, hbm, hbm, hbm, hbm, vmem] + [hbm] * len(big_weights)
                 + [vmem, hbm, hbm] + [vmem] * len(gains),
        out_specs=[hbm, hbm, vmem, vmem, vmem, hbm, hbm, hbm],
        out_shape=[jax.ShapeDtypeStruct((nb, t, d), _F32),
                   jax.ShapeDtypeStruct((ns, 1, d), _F32),
                   jax.ShapeDtypeStruct((POOL_BUF, nb, POOL_WIDTH), _F32),
                   jax.ShapeDtypeStruct((1, nb, CONV_W - 1, CONV_WIDTH), _F32),
                   jax.ShapeDtypeStruct((1, nb, CONV_W - 1, 2 * D_FF), _F32),
                   jax.ShapeDtypeStruct(sp.shape, _F32),
                   jax.ShapeDtypeStruct(sc.shape, _F32),
                   jax.ShapeDtypeStruct(sf.shape, _F32)],
        scratch_shapes=[pltpu.VMEM((d, IN_TOTAL), _BF16),
                        pltpu.VMEM((N_POOL_GROUPS, POOL_GC, POOL_OUT_GC), _BF16),
                        pltpu.VMEM((CONV_WIDTH, d), _BF16),
                        pltpu.VMEM((d, d), _BF16),
                        pltpu.VMEM((d, 2 * D_FF), _BF16),
                        pltpu.VMEM((D_FF, d), _BF16),
                        pltpu.VMEM((CONV_W, CONV_WIDTH), _F32),
                        pltpu.VMEM((CONV_W, 2 * D_FF), _F32),
                        pltpu.VMEM((CAST_SLOTS, CAST_ROWS, STAGE_COLS), _F32),
                        pltpu.VMEM(sp.shape, _F32),
                        pltpu.VMEM(sc.shape[1:], _F32),
                        pltpu.VMEM((POOL_HALO + N_META, POOL_WIDTH), _F32),
                        pltpu.VMEM((CONV_HALO + N_META, CONV_WIDTH), _F32),
                        pltpu.VMEM((CONV_HALO + N_META, 2 * D_FF), _F32),
                        pltpu.VMEM((ns + N_META, d), _BF16),
                        pltpu.VMEM((ns + N_META, D_FF), _BF16),
                        pltpu.VMEM((ns + N_META, d), _BF16),
                        pltpu.VMEM((ns + N_META, d), _F32),
                        pltpu.VMEM((ns, d), _F32),
                        pltpu.VMEM((ns, POOL_WIDTH), _F32),
                        pltpu.VMEM((ns, CONV_WIDTH), _F32),
                        pltpu.VMEM((SF_SLOTS, SF_GROUP, CONV_W - 1, 2 * D_FF), _F32),
                        pltpu.VMEM((2, seg, SUBLANES, d), _F32),
                        pltpu.VMEM((2, seg, SUBLANES, d), _F32),
                        pltpu.SemaphoreType.DMA((2,)),
                        pltpu.SemaphoreType.DMA((2,)),
                        pltpu.SemaphoreType.DMA((CAST_SLOTS,)),
                        pltpu.SemaphoreType.DMA((_N_STATE_SEMS,)),
                        pltpu.SemaphoreType.DMA((2, SF_SLOTS)),
                        pltpu.VMEM((rows_with_head(POOL_BUF), POOL_WIDTH), _F32),
                        pltpu.VMEM((rows_with_head(CONV_W - 1), CONV_WIDTH), _F32),
                        pltpu.VMEM((rows_with_head(CONV_W - 1), 2 * D_FF), _F32),
                        pltpu.VMEM((POOL_BUF * SUBLANES, POOL_WIDTH), _F32),
                        pltpu.VMEM(((CONV_W - 1) * SUBLANES, CONV_WIDTH), _F32),
                        pltpu.VMEM(((CONV_W - 1) * SUBLANES, 2 * D_FF), _F32),
                        pltpu.VMEM((tm, d), _BF16),
                        pltpu.VMEM((tm, D_FF), _BF16),
                        pltpu.VMEM((tm, d), _BF16),
                        pltpu.VMEM((tm, d), _F32),
                        pltpu.VMEM((tm, d), _F32),
                        pltpu.VMEM((rows_with_head(POOL_BUF), POOL_WIDTH), _F32),
                        pltpu.VMEM((rows_with_head(CONV_W - 1), CONV_WIDTH), _F32),
                        pltpu.VMEM((tm, d), _BF16),
                        pltpu.VMEM((rows_with_head(CONV_W - 1), 2 * D_FF), _F32),
                        pltpu.VMEM((tm, D_FF), _BF16)],
        compiler_params=pltpu.CompilerParams(
            vmem_limit_bytes=VMEM_LIMIT_BYTES),
        name="hybrid_step",
    )(x, xs, sp, sc, sf, meta, *big_weights, pscale, w_conv, w_ffn_conv, *gains)


def kernel(x_prompt, x_sample, state_pool, state_conv, state_ffn, meta_tokens, w_in, w_pool,
           pool_scale, w_conv, w_conv_out, w_o, w_up, w_ffn_conv, w_down, g_pre_mix,
           g_post_mix, g_pre_ffn, g_post_ffn):
    depth = w_in.shape[0]
    assert depth == 1, "the prompt / sample streams are chained for a single layer only"
    assert x_sample.shape[1] == 1 and meta_tokens.shape[0] == N_META
    assert N_META >= max(POOL_WINDOWS)
    row = lambda a: a.reshape(1, -1)
    l = 0
    big = (w_in[l], w_pool[l], w_conv_out[l], w_o[l], w_up[l], w_down[l])
    gains = (row(g_pre_mix[l]), row(g_post_mix[l]), row(g_pre_ffn[l]), row(g_post_ffn[l]))

    rows_first = lambda s: jnp.transpose(s, (1, 0, 2))
    y_prompt, y_sample, pp, pc, pf, sp_new, sc_new, sf_new = _fused_call(
        x_prompt, x_sample, rows_first(state_pool[l]), state_conv, state_ffn,
        meta_tokens.astype(_F32), big, row(pool_scale[l]), rows_first(w_conv),
        rows_first(w_ffn_conv), gains, l, TM_PROMPT)

    return (y_prompt, y_sample, rows_first(pp)[None], pc, pf, rows_first(sp_new)[None],
            sc_new, sf_new)
```

```python
import functools

import jax
import jax.numpy as jnp
from jax import lax
from jax.experimental import pallas as pl
from jax.experimental.pallas import tpu as pltpu

D_MODEL = 1024
N_META = 16
POOL_WINDOWS = (2, 4, 8, 16)
N_POOL_GROUPS = len(POOL_WINDOWS)
POOL_WIDTH = D_MODEL // 2
POOL_GC = POOL_WIDTH // N_POOL_GROUPS
POOL_OUT_GC = D_MODEL // N_POOL_GROUPS
POOL_BUF = max(POOL_WINDOWS) - 1
CONV_WIDTH = D_MODEL // 2
CONV_W = 3
D_FF = 2816
IN_TOTAL = 2 * POOL_WIDTH + 2 * CONV_WIDTH + 2 * D_MODEL
EPS = 1e-6
LOG2_E = 1.4426950408889634
OFF_U, OFF_V, OFF_B, OFF_C, OFF_GA, OFF_GB = 0, 512, 1024, 1536, 2048, 3072

SUBLANES = 8
FF_CHUNK = 256
POOL_HALO = 16
CONV_HALO = 8
TM_PROMPT = 256
CAST_ROWS = 128
CAST_SLOTS = 3
SF_GROUP = 8
SF_SLOTS = 4
STAGE_COLS = 2 * D_FF
VMEM_LIMIT_BYTES = 58 * 1024 * 1024

_BF16 = jnp.bfloat16
_F32 = jnp.float32

(_SEM_POOL_IN, _SEM_CONV_IN, _SEM_X_SAMPLE, _SEM_Y_SAMPLE, _SEM_POOL_OUT, _SEM_CONV_OUT,
 _SEM_CONV_WEIGHTS, _N_STATE_SEMS) = range(8)


def _dot(a, b):
    return jnp.dot(a, b, preferred_element_type=_F32)


def _rms(x, g):
    ms = jnp.mean(x * x, axis=-1, keepdims=True)
    return x * lax.rsqrt(ms + EPS) * g


def _sigmoid(x):
    return 1.0 / (1.0 + jnp.exp2(x * (-LOG2_E)))


def _mixer(x, w, hist, mbuf, pool_div):
    (w_in, w_pool, pscale, wconv, wco, _, _, _, _, g1, _, _, _) = w
    h = _rms(x, g1[...]).astype(_BF16)

    hist.put_u(_dot(h, w_in[:, OFF_U:OFF_U + POOL_WIDTH]))
    v = _dot(h, w_in[:, OFF_V:OFF_V + CONV_WIDTH])
    cg = _dot(h, w_in[:, OFF_C:OFF_C + CONV_WIDTH])
    hist.put_c(cg * v)
    cv = wconv[CONV_W - 1:CONV_W, :] * hist.get_c(0)
    for k in range(1, CONV_W):
        cv = cv + wconv[CONV_W - 1 - k:CONV_W - k, :] * hist.get_c(k)
    bg = _dot(h, w_in[:, OFF_B:OFF_B + CONV_WIDTH])
    b_in = (bg * cv).astype(_BF16)

    for g, win in enumerate(POOL_WINDOWS):
        cs = slice(g * POOL_GC, (g + 1) * POOL_GC)
        os_ = slice(g * POOL_OUT_GC, (g + 1) * POOL_OUT_GC)
        ug = hist.get_u(0, cs)
        s = ug
        for k in range(1, win):
            s = s + hist.get_u(k, cs)
        pooled = (pool_div(s, win) - ug).astype(_BF16)
        a = _dot(pooled, w_pool[g]) * pscale[:, os_]
        ga = _dot(h, w_in[:, OFF_GA + g * POOL_OUT_GC:OFF_GA + (g + 1) * POOL_OUT_GC])
        gb = _dot(h, w_in[:, OFF_GB + g * POOL_OUT_GC:OFF_GB + (g + 1) * POOL_OUT_GC])
        b = _dot(b_in, wco[:, os_])
        mbuf[:, os_] = (_sigmoid(ga) * a + _sigmoid(gb) * b).astype(_BF16)


def _mixer_out(w, mbuf, rows=slice(None)):
    return _dot(mbuf[rows, :], w[5][...])


def _mixer_residual(x, mixed, w):
    return x + _rms(mixed, w[10][...])


def _ffn_norm(x1, w):
    return _rms(x1, w[11][...]).astype(_BF16)


def _ffn_up(h2, w, hist, actbuf):
    wup, wfc = w[6], w[7]
    for c in range(D_FF // FF_CHUNK):
        conv = []
        for col in (c * FF_CHUNK, D_FF + c * FF_CHUNK):
            cols = slice(col, col + FF_CHUNK)
            hist.put_up(_dot(h2, wup[:, cols]), cols)
            y = wfc[CONV_W - 1:CONV_W, cols] * hist.get_up(0, cols)
            for k in range(1, CONV_W):
                y = y + wfc[CONV_W - 1 - k:CONV_W - k, cols] * hist.get_up(k, cols)
            conv.append(y)
        gate, val = conv
        actbuf[:, c * FF_CHUNK:(c + 1) * FF_CHUNK] = (gate * _sigmoid(gate) * val).astype(_BF16)


def _ffn_down(w, actbuf, rows=slice(None)):
    return _dot(actbuf[rows, :], w[8][...])


def _ffn_residual(x1, down, w):
    return x1 + _rms(down, w[12][...])


def _full_windows(s, win):
    return s * (1.0 / win)


class _RowHistory:
    def __init__(self, ubuf, cbuf, upbuf, m):
        self.ubuf, self.cbuf, self.upbuf, self.m = ubuf, cbuf, upbuf, m

    def put_u(self, val):
        self.ubuf[POOL_HALO:POOL_HALO + self.m, :] = val

    def get_u(self, k, cs):
        return self.ubuf[POOL_HALO - k:POOL_HALO - k + self.m, cs]

    def put_c(self, val):
        self.cbuf[CONV_HALO:CONV_HALO + self.m, :] = val

    def get_c(self, k):
        return self.cbuf[CONV_HALO - k:CONV_HALO - k + self.m, :]

    def put_up(self, val, cols):
        self.upbuf[CONV_HALO:CONV_HALO + self.m, cols] = val

    def get_up(self, k, cols):
        return self.upbuf[CONV_HALO - k:CONV_HALO - k + self.m, cols]


class _SegmentHistory:
    def __init__(self, ubuf, cbuf, upbuf, utail, ctail, ftail, tm):
        self.ubuf, self.cbuf, self.upbuf, self.tm = ubuf, cbuf, upbuf, tm
        self.utail, self.ctail, self.ftail = utail, ctail, ftail

    def _put(self, buf, tail, val, cols, depth):
        tm = self.tm
        buf[depth * SUBLANES:depth * SUBLANES + tm, cols] = val
        sublane = lax.broadcasted_iota(jnp.int32, (SUBLANES, val.shape[1]), 0)
        for j in range(depth):
            rows = slice(j * SUBLANES, (j + 1) * SUBLANES)
            cur = val[tm - (depth - j) * SUBLANES:tm - (depth - j - 1) * SUBLANES, :]
            mixed = jnp.where(sublane == SUBLANES - 1, tail[rows, cols], cur)
            buf[rows, cols] = pltpu.roll(mixed, 1, axis=0)
            tail[rows, cols] = cur

    def _get(self, buf, k, cols, depth):
        start = (depth - k) * SUBLANES
        return buf[start:start + self.tm, cols]

    def put_u(self, val):
        self._put(self.ubuf, self.utail, val, slice(None), POOL_BUF)

    def get_u(self, k, cs):
        return self._get(self.ubuf, k, cs, POOL_BUF)

    def put_c(self, val):
        self._put(self.cbuf, self.ctail, val, slice(None), CONV_W - 1)

    def get_c(self, k):
        return self._get(self.cbuf, k, slice(None), CONV_W - 1)

    def put_up(self, val, cols):
        self._put(self.upbuf, self.ftail, val, cols, CONV_W - 1)

    def get_up(self, k, cols):
        return self._get(self.upbuf, k, cols, CONV_W - 1)


class _StateHistory:
    def __init__(self, sp_ref, sc_ref, sf_ref, u_ref, c_ref, f_ref):
        self.sp, self.sc, self.sf = sp_ref, sc_ref, sf_ref
        self.u, self.c, self.f = u_ref, c_ref, f_ref

    def put_u(self, val):
        self.u[...] = val

    def get_u(self, k, cs):
        return self.u[:, cs] if k == 0 else self.sp[POOL_BUF - k, :, cs]

    def put_c(self, val):
        self.c[...] = val

    def get_c(self, k):
        return self.c[...] if k == 0 else self.sc[:, CONV_W - 1 - k, :]

    def put_up(self, val, cols):
        self.f[:, cols] = val

    def get_up(self, k, cols):
        return self.f[:, cols] if k == 0 else self.sf[CONV_W - 1 - k, :, cols]


class _StackedHistory:
    def __init__(self, top, bottom, split):
        self.top, self.bottom, self.split = top, bottom, split

    def _both(self, get, *args):
        return jnp.concatenate([get(self.top)(*args), get(self.bottom)(*args)], axis=0)

    def put_u(self, val):
        self.top.put_u(val[:self.split])
        self.bottom.put_u(val[self.split:])

    def get_u(self, k, cs):
        return self._both(lambda h: h.get_u, k, cs)

    def put_c(self, val):
        self.top.put_c(val[:self.split])
        self.bottom.put_c(val[self.split:])

    def get_c(self, k):
        return self._both(lambda h: h.get_c, k)

    def put_up(self, val, cols):
        self.top.put_up(val[:self.split], cols)
        self.bottom.put_up(val[self.split:], cols)

    def get_up(self, k, cols):
        return self._both(lambda h: h.get_up, k, cols)


def _fill_tail(tail, state_ref, first_row, depth):
    for j in range(depth):
        r = first_row + j
        tail[j * SUBLANES:(j + 1) * SUBLANES, :] = jnp.broadcast_to(
            state_ref[r:r + 1, :], (SUBLANES, tail.shape[1]))


def _read_tail(out_ref, tail, depth):
    for j in range(depth):
        out_ref[j:j + 1, :] = tail[(j + 1) * SUBLANES - 1:(j + 1) * SUBLANES, :]


def _cast_batches(pairs):
    jobs = []
    for src, dst in pairs:
        if len(src.shape) == 3:
            assert src.shape[1] == CAST_ROWS
            jobs += [(src.at[g], dst.at[g], src.shape[2]) for g in range(src.shape[0])]
        else:
            assert src.shape[0] % CAST_ROWS == 0
            jobs += [(src.at[pl.ds(r, CAST_ROWS), :], dst.at[pl.ds(r, CAST_ROWS), :], src.shape[1])
                     for r in range(0, src.shape[0], CAST_ROWS)]
    batches, cur, used = [], [], 0
    for s, d, cols in jobs:
        if used + cols > STAGE_COLS:
            batches.append(cur)
            cur, used = [], 0
        cur.append((s, d, used, cols))
        used += cols
    batches.append(cur)
    return batches


def _cast_weights(first_pairs, later_pairs, between, stage, sem):
    slots = stage.shape[0]
    first = _cast_batches(first_pairs)
    batches = first + _cast_batches(later_pairs)
    hooks = {len(first) + 2 * i: fn for i, fn in enumerate(between)}
    assert not hooks or max(hooks) < len(batches)

    def copies(b):
        return [pltpu.make_async_copy(s, stage.at[b % slots, :, pl.ds(lane, cols)],
                                      sem.at[b % slots]) for s, _, lane, cols in batches[b]]

    def start(b):
        for i, cp in enumerate(copies(b)):
            cp.start(priority=(b + i) % 2)

    for b in range(min(slots, len(batches))):
        start(b)
    for b in range(len(batches)):
        for cp in copies(b):
            cp.wait()
        for _, d, lane, cols in batches[b]:
            d[...] = stage[b % slots, :, lane:lane + cols].astype(_BF16)
        if b + slots < len(batches):
            start(b + slots)
        if b in hooks:
            hooks[b]()


def _fused_kernel(x_hbm, xs_hbm, sp_hbm, sc_hbm, sf_hbm, meta_ref,
                  w_in_hbm, w_pool_hbm, wco_hbm, wo_hbm, wup_hbm, wdown_hbm,
                  pscale, wconv_hbm, wfc_hbm, g1, g2, g3, g4,
                  y_hbm, ys_hbm, op_ref, oc_ref, of_ref, spo_hbm, sco_hbm, sfo_hbm,
                  w_in, w_pool, wco, wo, wup, wdown, wconv, wfc, stage, sp_v, sc_v,
                  m_ubuf, m_cbuf, m_upbuf,
                  s_mbuf, s_actbuf, s_h2, xs_v, ys_v, us_v, cs_v, sf_ring, xbuf, ybuf,
                  in_sem, out_sem, cast_sem, state_sem, ring_sem,
                  ubuf, cbuf, upbuf, utail, ctail, ftail, mbuf, actbuf, h2buf, x1_new, x1_old,
                  ubuf_first, cbuf_first, mbuf_first, upbuf_last, actbuf_last,
                  *, tm, nt, total, layer):
    w = (w_in, w_pool, pscale, wconv, wco, wo, wup, wfc, wdown, g1, g2, g3, g4)
    seg = tm // SUBLANES
    n, slot = 0, 0

    def x_copies(tile, sl):
        b, t0 = tile // nt, (tile % nt) * tm
        return [pltpu.make_async_copy(x_hbm.at[b, pl.ds(t0 + s * seg, seg), :],
                                      xbuf.at[sl, :, s, :], in_sem.at[sl]) for s in range(SUBLANES)]

    def y_copies(tile, sl):
        b, t0 = tile // nt, (tile % nt) * tm
        return [pltpu.make_async_copy(ybuf.at[sl, :, s, :],
                                      y_hbm.at[b, pl.ds(t0 + s * seg, seg), :], out_sem.at[sl])
                for s in range(SUBLANES)]

    ring_slots, group = sf_ring.shape[:2]
    n_sample = ys_v.shape[0]
    n_groups = n_sample // group
    pool_out_cps = [
        pltpu.make_async_copy(sp_hbm.at[pl.ds(1, POOL_BUF - 1)], spo_hbm.at[pl.ds(0, POOL_BUF - 1)],
                              state_sem.at[_SEM_POOL_OUT]),
        pltpu.make_async_copy(us_v, spo_hbm.at[POOL_BUF - 1], state_sem.at[_SEM_POOL_OUT])]
    conv_out_cp = pltpu.make_async_copy(sc_v, sco_hbm.at[layer], state_sem.at[_SEM_CONV_OUT])

    def ffn_in_copy(g):
        return pltpu.make_async_copy(sf_hbm.at[layer, pl.ds(g * group, group)],
                                     sf_ring.at[g % ring_slots], ring_sem.at[0, g % ring_slots])

    def ffn_out_copy(g):
        return pltpu.make_async_copy(sf_ring.at[g % ring_slots],
                                     sfo_hbm.at[layer, pl.ds(g * group, group)],
                                     ring_sem.at[1, g % ring_slots])

    def wait_tile(buf, sem):
        pltpu.make_async_copy(buf, buf, sem).wait()

    def start_sequence_mixer_history():
        _fill_tail(utail, m_ubuf, POOL_HALO + N_META - POOL_BUF, POOL_BUF)
        _fill_tail(ctail, m_cbuf, CONV_HALO + N_META - (CONV_W - 1), CONV_W - 1)

    def token_mixer(work):
        ub, cb, mb = work
        hist = _SegmentHistory(ub, cb, None, utail, ctail, None, tm)
        x = xbuf[slot].reshape(tm, D_MODEL)
        _mixer(x, w, hist, mb, _full_windows)
        x1 = _mixer_residual(x, _mixer_out(w, mb), w)
        x1_new[...] = x1
        h2buf[...] = _ffn_norm(x1, w)

    def channel_mixer_up(work):
        upb, ab = work
        _ffn_up(h2buf[...], w, _SegmentHistory(None, None, upb, None, None, ftail, tm), ab)

    def channel_mixer_out(work, x1_ref):
        y = _ffn_residual(x1_ref[...], _ffn_down(w, work[1]), w)
        ybuf[1 - slot] = y.reshape(seg, SUBLANES, D_MODEL)

    def prefetch_next_tile():
        for cp in x_copies(jnp.minimum(n + 1, total - 1), 1 - slot):
            cp.start()

    def send_previous_tile():
        for cp in y_copies(n - 1, 1 - slot):
            cp.start()

    def first_step():
        for cp in x_copies(n, slot):
            cp.start()
        pool_cp = pltpu.make_async_copy(sp_hbm, sp_v, state_sem.at[_SEM_POOL_IN])
        conv_cp = pltpu.make_async_copy(sc_hbm.at[layer], sc_v, state_sem.at[_SEM_CONV_IN])
        wconv_cp = pltpu.make_async_copy(wconv_hbm.at[:, layer, :], wconv,
                                         state_sem.at[_SEM_CONV_WEIGHTS])
        wfc_cp = pltpu.make_async_copy(wfc_hbm.at[:, layer, :], wfc,
                                       state_sem.at[_SEM_CONV_WEIGHTS])
        wconv_cp.start()
        wfc_cp.start()
        xs_cp = pltpu.make_async_copy(xs_hbm.at[:, 0, :], xs_v.at[pl.ds(0, n_sample)],
                                      state_sem.at[_SEM_X_SAMPLE])
        ys_cp = pltpu.make_async_copy(ys_v, ys_hbm.at[:, 0, :], state_sem.at[_SEM_Y_SAMPLE])
        pool_cp.start()
        conv_cp.start()
        xs_cp.start()
        for g in range(min(ring_slots - 1, n_groups)):
            ffn_in_copy(g).start(priority=g % 2)

        xs_v[n_sample:n_sample + N_META, :] = meta_ref[...]
        m_ubuf[0:POOL_HALO, :] = jnp.zeros((POOL_HALO, POOL_WIDTH), _F32)
        m_cbuf[0:CONV_HALO, :] = jnp.zeros((CONV_HALO, CONV_WIDTH), _F32)
        m_upbuf[0:CONV_HALO, :] = jnp.zeros((CONV_HALO, 2 * D_FF), _F32)
        row = lax.broadcasted_iota(jnp.int32, (n_sample + N_META, POOL_GC), 0)

        def short_windows(s, win):
            count = jnp.where(row < n_sample, win, jnp.minimum(win, row - n_sample + 1))
            return s / count.astype(_F32)

        s_up = stage.at[CONV_W - 1]
        short_hist = _StackedHistory(_StateHistory(sp_v, sc_v, stage, us_v, cs_v, s_up),
                                     _RowHistory(m_ubuf, m_cbuf, m_upbuf, N_META), n_sample)

        def short_token_mixer():
            wconv_cp.wait()
            wfc_cp.wait()
            pool_cp.wait()
            conv_cp.wait()
            xs_cp.wait()
            _mixer(xs_v[...], w, short_hist, s_mbuf, short_windows)
            x1 = _mixer_residual(xs_v[...], _mixer_out(w, s_mbuf), w)
            ys_v[...] = x1[:n_sample]
            s_h2[...] = _ffn_norm(x1, w)
            for cp in pool_out_cps:
                cp.start()
            for j in range(CONV_W - 2):
                sc_v[:, j, :] = sc_v[:, j + 1, :]
            sc_v[:, CONV_W - 2, :] = cs_v[...]
            conv_out_cp.start()

        def first_tile_token_mixer():
            wait_tile(xbuf.at[slot], in_sem.at[slot])
            prefetch_next_tile()
            start_sequence_mixer_history()
            token_mixer((ubuf_first, cbuf_first, mbuf_first))

        _cast_weights([(w_in_hbm, w_in), (w_pool_hbm, w_pool), (wco_hbm, wco), (wo_hbm, wo)],
                      [(wup_hbm, wup), (wdown_hbm, wdown)],
                      [short_token_mixer, first_tile_token_mixer],
                      stage, cast_sem)
        for g in range(n_groups):
            if g + ring_slots - 1 < n_groups:
                ffn_in_copy(g + ring_slots - 1).start(priority=(g + ring_slots - 1) % 2)
            ffn_in_copy(g).wait()
            for k in range(CONV_W - 1):
                stage[k, g * group:(g + 1) * group, :] = sf_ring[g % ring_slots, :, k, :]

        _ffn_up(s_h2[...], w, short_hist, s_actbuf)
        ys_v[...] = _ffn_residual(ys_v[...], _ffn_down(w, s_actbuf, slice(0, n_sample)), w)
        ys_cp.start()

        for g in range(n_groups):
            if g >= ring_slots:
                ffn_out_copy(g - ring_slots).wait()
            rows = slice(g * group, (g + 1) * group)
            for k in range(1, CONV_W - 1):
                sf_ring[g % ring_slots, :, k - 1, :] = stage[k, rows, :]
            sf_ring[g % ring_slots, :, CONV_W - 2, :] = s_up[rows, :]
            ffn_out_copy(g).start(priority=g % 2)
        ys_cp.wait()

    def token_mixer_sequence_end():
        for j in range(POOL_BUF):
            op_ref[j, pl.ds(n // nt, 1), :] = utail[(j + 1) * SUBLANES - 1:(j + 1) * SUBLANES, :]
        _read_tail(oc_ref.at[layer, n // nt], ctail, CONV_W - 1)

    def channel_mixer_sequence_end():
        _read_tail(of_ref.at[layer, (n - 1) // nt], ftail, CONV_W - 1)

    def channel_mixer_sequence_start():
        _fill_tail(ftail, m_upbuf, CONV_HALO + N_META - (CONV_W - 1), CONV_W - 1)

    def steady_step(i, carry):
        nonlocal n, slot
        n, slot = i, i % 2
        wait_tile(xbuf.at[slot], in_sem.at[slot])
        pl.when(n >= 3)(lambda: wait_tile(ybuf.at[1 - slot], out_sem.at[1 - slot]))
        pl.when(n % nt == 0)(start_sequence_mixer_history)
        pl.when(n % nt == 1)(channel_mixer_sequence_start)

        prefetch_next_tile()
        x1_old[...] = x1_new[...]
        channel_mixer_up((upbuf, actbuf))
        token_mixer((ubuf, cbuf, mbuf))
        channel_mixer_out((upbuf, actbuf), x1_old)
        send_previous_tile()

        pl.when(n % nt == nt - 1)(token_mixer_sequence_end)
        pl.when(n % nt == 0)(channel_mixer_sequence_end)
        return carry

    n, slot = 0, 0
    first_step()
    if nt == 1:
        token_mixer_sequence_end()
    lax.fori_loop(1, total, steady_step, 0)

    n, slot = total, total % 2
    if total >= 3:
        wait_tile(ybuf.at[1 - slot], out_sem.at[1 - slot])
    if total % nt == 1:
        channel_mixer_sequence_start()
    channel_mixer_up((upbuf_last, actbuf_last))
    channel_mixer_out((upbuf_last, actbuf_last), x1_new)
    send_previous_tile()
    channel_mixer_sequence_end()
    wait_tile(ybuf.at[slot], out_sem.at[slot])
    wait_tile(ybuf.at[1 - slot], out_sem.at[1 - slot])
    wait_tile(xbuf.at[slot], in_sem.at[slot])
    for cp in pool_out_cps:
        cp.wait()
    conv_out_cp.wait()
    for g in range(max(n_groups - ring_slots, 0), n_groups):
        ffn_out_copy(g).wait()


def _fused_call(x, xs, sp, sc, sf, meta, big_weights, pscale, w_conv, w_ffn_conv, gains, l, tm):
    nb, t, d = x.shape
    ns = xs.shape[0]
    seg = tm // SUBLANES
    assert t % tm == 0 and tm % SUBLANES == 0 and seg >= POOL_BUF
    assert xs.shape == (ns, 1, d) and ns == CAST_ROWS and ns % SF_GROUP == 0
    assert sf.shape[1:] == (ns, CONV_W - 1, STAGE_COLS) and CAST_SLOTS >= CONV_W
    assert sp.shape == (POOL_BUF, ns, POOL_WIDTH) and sc.shape[1:] == (ns, CONV_W - 1, CONV_WIDTH)
    nt = t // tm
    assert nt >= 2
    total = nb * nt
    hbm = pl.BlockSpec(memory_space=pl.ANY)
    vmem = pl.BlockSpec(memory_space=pltpu.MemorySpace.VMEM)
    rows_with_head = lambda depth: (depth + seg) * SUBLANES
    return pl.pallas_call(
        functools.partial(_fused_kernel, tm=tm, nt=nt, total=total, layer=l),
        in_specs=[hbm, hbm, hbm, hbm, hbm, vmem] + [hbm] * len(big_weights)
                 + [vmem, hbm, hbm] + [vmem] * len(gains),
        out_specs=[hbm, hbm, vmem, vmem, vmem, hbm, hbm, hbm],
        out_shape=[jax.ShapeDtypeStruct((nb, t, d), _F32),
                   jax.ShapeDtypeStruct((ns, 1, d), _F32),
                   jax.ShapeDtypeStruct((POOL_BUF, nb, POOL_WIDTH), _F32),
                   jax.ShapeDtypeStruct((1, nb, CONV_W - 1, CONV_WIDTH), _F32),
                   jax.ShapeDtypeStruct((1, nb, CONV_W - 1, 2 * D_FF), _F32),
                   jax.ShapeDtypeStruct(sp.shape, _F32),
                   jax.ShapeDtypeStruct(sc.shape, _F32),
                   jax.ShapeDtypeStruct(sf.shape, _F32)],
        scratch_shapes=[pltpu.VMEM((d, IN_TOTAL), _BF16),
                        pltpu.VMEM((N_POOL_GROUPS, POOL_GC, POOL_OUT_GC), _BF16),
                        pltpu.VMEM((CONV_WIDTH, d), _BF16),
                        pltpu.VMEM((d, d), _BF16),
                        pltpu.VMEM((d, 2 * D_FF), _BF16),
                        pltpu.VMEM((D_FF, d), _BF16),
                        pltpu.VMEM((CONV_W, CONV_WIDTH), _F32),
                        pltpu.VMEM((CONV_W, 2 * D_FF), _F32),
                        pltpu.VMEM((CAST_SLOTS, CAST_ROWS, STAGE_COLS), _F32),
                        pltpu.VMEM(sp.shape, _F32),
                        pltpu.VMEM(sc.shape[1:], _F32),
                        pltpu.VMEM((POOL_HALO + N_META, POOL_WIDTH), _F32),
                        pltpu.VMEM((CONV_HALO + N_META, CONV_WIDTH), _F32),
                        pltpu.VMEM((CONV_HALO + N_META, 2 * D_FF), _F32),
                        pltpu.VMEM((ns + N_META, d), _BF16),
                        pltpu.VMEM((ns + N_META, D_FF), _BF16),
                        pltpu.VMEM((ns + N_META, d), _BF16),
                        pltpu.VMEM((ns + N_META, d), _F32),
                        pltpu.VMEM((ns, d), _F32),
                        pltpu.VMEM((ns, POOL_WIDTH), _F32),
                        pltpu.VMEM((ns, CONV_WIDTH), _F32),
                        pltpu.VMEM((SF_SLOTS, SF_GROUP, CONV_W - 1, 2 * D_FF), _F32),
                        pltpu.VMEM((2, seg, SUBLANES, d), _F32),
                        pltpu.VMEM((2, seg, SUBLANES, d), _F32),
                        pltpu.SemaphoreType.DMA((2,)),
                        pltpu.SemaphoreType.DMA((2,)),
                        pltpu.SemaphoreType.DMA((CAST_SLOTS,)),
                        pltpu.SemaphoreType.DMA((_N_STATE_SEMS,)),
                        pltpu.SemaphoreType.DMA((2, SF_SLOTS)),
                        pltpu.VMEM((rows_with_head(POOL_BUF), POOL_WIDTH), _F32),
                        pltpu.VMEM((rows_with_head(CONV_W - 1), CONV_WIDTH), _F32),
                        pltpu.VMEM((rows_with_head(CONV_W - 1), 2 * D_FF), _F32),
                        pltpu.VMEM((POOL_BUF * SUBLANES, POOL_WIDTH), _F32),
                        pltpu.VMEM(((CONV_W - 1) * SUBLANES, CONV_WIDTH), _F32),
                        pltpu.VMEM(((CONV_W - 1) * SUBLANES, 2 * D_FF), _F32),
                        pltpu.VMEM((tm, d), _BF16),
                        pltpu.VMEM((tm, D_FF), _BF16),
                        pltpu.VMEM((tm, d), _BF16),
                        pltpu.VMEM((tm, d), _F32),
                        pltpu.VMEM((tm, d), _F32),
                        pltpu.VMEM((rows_with_head(POOL_BUF), POOL_WIDTH), _F32),
                        pltpu.VMEM((rows_with_head(CONV_W - 1), CONV_WIDTH), _F32),
                        pltpu.VMEM((tm, d), _BF16),
                        pltpu.VMEM((rows_with_head(CONV_W - 1), 2 * D_FF), _F32),
                        pltpu.VMEM((tm, D_FF), _BF16)],
        compiler_params=pltpu.CompilerParams(
            vmem_limit_bytes=VMEM_LIMIT_BYTES),
        name="hybrid_step",
    )(x, xs, sp, sc, sf, meta, *big_weights, pscale, w_conv, w_ffn_conv, *gains)


def kernel(x_prompt, x_sample, state_pool, state_conv, state_ffn, meta_tokens, w_in, w_pool,
           pool_scale, w_conv, w_conv_out, w_o, w_up, w_ffn_conv, w_down, g_pre_mix,
           g_post_mix, g_pre_ffn, g_post_ffn):
    depth = w_in.shape[0]
    assert depth == 1, "the prompt / sample streams are chained for a single layer only"
    assert x_sample.shape[1] == 1 and meta_tokens.shape[0] == N_META
    assert N_META >= max(POOL_WINDOWS)
    row = lambda a: a.reshape(1, -1)
    l = 0
    big = (w_in[l], w_pool[l], w_conv_out[l], w_o[l], w_up[l], w_down[l])
    gains = (row(g_pre_mix[l]), row(g_post_mix[l]), row(g_pre_ffn[l]), row(g_post_ffn[l]))

    rows_first = lambda s: jnp.transpose(s, (1, 0, 2))
    y_prompt, y_sample, pp, pc, pf, sp_new, sc_new, sf_new = _fused_call(
        x_prompt, x_sample, rows_first(state_pool[l]), state_conv, state_ffn,
        meta_tokens.astype(_F32), big, row(pool_scale[l]), rows_first(w_conv),
        rows_first(w_ffn_conv), gains, l, TM_PROMPT)

    return (y_prompt, y_sample, rows_first(pp)[None], pc, pf, rows_first(sp_new)[None],
            sc_new, sf_new)
```
